```python
import math
import jax
import jax.numpy as jnp
from jax import lax
import numpy as np

D_MODEL = 1024
BATCH = 16
SEQ = 4096
DEPTH = 4
DEC_BATCH = 2
DEC_SEQ = 8192
PAST_LEN = 128

N_HEADS = 4
HEAD_DIM = 128
MIX_W = N_HEADS * HEAD_DIM
N_BRANCH = 4
RG_CONV = 4
SHORT_CONV = 3
LRU_C = 8.0
MLSTM_CHUNK = 128
HGRN_CHUNK = 32
HYENA_EMB = 33
HYENA_BANDS = (HYENA_EMB - 1) // 2
HYENA_HID = 64
HYENA_FAST = 0.3
HYENA_SLOW = 1.5
HYENA_TARGET = 1e-2
D_FF = 2 * D_MODEL
FFN_CONV = 3
D_PLE = 256
EPS = 1e-6
STAB_INIT = -1e30

SEG_A = 2 * MIX_W
SEG_B = 4 * MIX_W + 4 * N_HEADS
SEG_C = 3 * MIX_W
SEG_D = 5 * MIX_W
SEG_G = N_BRANCH * D_MODEL
OFF_A = 0
OFF_B = OFF_A + SEG_A
OFF_C = OFF_B + SEG_B
OFF_D = OFF_C + SEG_C
OFF_G = OFF_D + SEG_D
D_IN = OFF_G + SEG_G

SHARED_PARAMS = ('hgrn_lb_logits', 'final_norm')

kernel_name = 'hybrid_bidir_encoder_rglru_mlstm_hyena_hgrn2'


def _rmsnorm(x, g):
    xf = x.astype(jnp.float32)
    y = xf * lax.rsqrt(jnp.mean(xf * xf, axis=-1, keepdims=True) + EPS)
    return (y * g.astype(jnp.float32)).astype(x.dtype)


def _head_rmsnorm(x, g):
    shp = x.shape
    xh = x.reshape(shp[:-1] + (N_HEADS, HEAD_DIM))
    xh = xh * lax.rsqrt(jnp.mean(xh * xh, axis=-1, keepdims=True) + EPS)
    return xh.reshape(shp) * g.astype(jnp.float32)


def _proj(xn, w_in, b_in, off, width):
    return xn @ w_in[:, off:off + width] + b_in[off:off + width]


def _dwconv(x, w, b):
    kw = w.shape[0]
    s = x.shape[1]
    lo = (kw - 1) // 2
    xp = jnp.pad(x, ((0, 0), (lo, kw - 1 - lo), (0, 0)))
    y = b + w[0] * xp[:, 0:s]
    for j in range(1, kw):
        y = y + w[j] * xp[:, j:j + s]
    return y


def _linear_scan(a, u, reverse):
    def combine(left, right):
        a_l, u_l = left
        a_r, u_r = right
        return a_l * a_r, a_r * u_l + u_r
    _, h = lax.associative_scan(combine, (a, u), reverse=reverse, axis=1)
    return h


def _rglru_branch(xn, w_in, b_in, conv_w, conv_b, gate_w, gate_b, lam):
    f32 = jnp.float32
    bsz, s, _ = xn.shape
    x = _dwconv(_proj(xn, w_in, b_in, OFF_A, MIX_W), conv_w, conv_b).astype(f32)
    y = jax.nn.gelu(_proj(xn, w_in, b_in, OFF_A + MIX_W, MIX_W).astype(f32))
    xh = x.reshape(bsz, s, N_HEADS, HEAD_DIM)
    hs = []
    for d in range(2):
        g = jnp.einsum('bshi,ghij->gbshj', xh, gate_w[d].astype(f32)).reshape(2, bsz, s, MIX_W)
        g = jax.nn.sigmoid(g + gate_b[d][:, None, None, :].astype(f32))
        log_a = -LRU_C * g[0] * jax.nn.softplus(-lam[d].astype(f32))
        u = jnp.sqrt(-jnp.expm1(2.0 * log_a)) * (g[1] * x)
        hs.append(_linear_scan(jnp.exp(log_a), u, reverse=(d == 1)))
    return ((hs[0] + hs[1]) * y).astype(xn.dtype)


def _to_chunks(t, cl):
    bsz, s, _ = t.shape
    return t.reshape(bsz, s // cl, cl, N_HEADS, -1).transpose(0, 3, 1, 2, 4)


def _gate_chunks(t, cl):
    bsz, s, _ = t.shape
    return t.reshape(bsz, s // cl, cl, N_HEADS).transpose(0, 3, 1, 2)


def _from_chunks(t):
    bsz, nh, nc, cl, d = t.shape
    return t.transpose(0, 2, 3, 1, 4).reshape(bsz, nc * cl, nh * d)


def _mlstm_chunkwise(q, k, v, ig, lf):
    bsz, nh, nc, cl, dk = q.shape
    dv = v.shape[-1]
    b = jnp.cumsum(lf, axis=-1)
    b_last = b[..., -1]
    lw = b_last[..., None] - b + ig
    m_loc = jnp.max(lw, axis=-1)
    w = jnp.exp(lw - m_loc[..., None])
    c_loc = jnp.einsum('bhcs,bhcsk,bhcsv->bhckv', w, k, v)
    n_loc = jnp.einsum('bhcs,bhcsk->bhck', w, k)

    def step(carry, xs):
        c_st, n_st, m_st = carry
        c_l, n_l, m_l, bl = xs
        m_new = jnp.maximum(bl + m_st, m_l)
        s_old = jnp.exp(bl + m_st - m_new)
        s_new = jnp.exp(m_l - m_new)
        c_next = s_old[..., None, None] * c_st + s_new[..., None, None] * c_l
        n_next = s_old[..., None] * n_st + s_new[..., None] * n_l
        return (c_next, n_next, m_new), (c_st, n_st, m_st)

    init = (jnp.zeros((bsz, nh, dk, dv), q.dtype), jnp.zeros((bsz, nh, dk), q.dtype),
            jnp.full((bsz, nh), STAB_INIT, q.dtype))
    xs = (jnp.moveaxis(c_loc, 2, 0), jnp.moveaxis(n_loc, 2, 0), jnp.moveaxis(m_loc, 2, 0), jnp.moveaxis(b_last, 2, 0))
    _, (c_prev, n_prev, m_prev) = lax.scan(step, init, xs)
    c_prev = jnp.moveaxis(c_prev, 0, 2)
    n_prev = jnp.moveaxis(n_prev, 0, 2)
    m_prev = jnp.moveaxis(m_prev, 0, 2)

    lower = jnp.tril(jnp.ones((cl, cl), dtype=bool))
    d_mat = jnp.where(lower, b[..., :, None] - b[..., None, :] + ig[..., None, :], -jnp.inf)
    m_inter = b + m_prev[..., None]
    m_all = jnp.maximum(jnp.max(d_mat, axis=-1), m_inter)
    a_mat = jnp.exp(d_mat - m_all[..., None]) * jnp.einsum('bhcjd,bhcsd->bhcjs', q, k)
    s_inter = jnp.exp(m_inter - m_all)
    num = (jnp.einsum('bhcjs,bhcsv->bhcjv', a_mat, v)
           + s_inter[..., None] * jnp.einsum('bhcjd,bhcdv->bhcjv', q, c_prev))
    den = jnp.sum(a_mat, axis=-1) + s_inter * jnp.einsum('bhcjd,bhcd->bhcj', q, n_prev)
    return num / jnp.maximum(jnp.abs(den), jnp.exp(-m_all))[..., None]


def _mlstm_branch(xn, w_in, b_in, norm_g):
    f32 = jnp.float32
    bsz, s, _ = xn.shape
    q = _proj(xn, w_in, b_in, OFF_B, MIX_W).astype(f32)
    k = _proj(xn, w_in, b_in, OFF_B + MIX_W, MIX_W).astype(f32) * (HEAD_DIM ** -0.5)
    v = _proj(xn, w_in, b_in, OFF_B + 2 * MIX_W, MIX_W).astype(f32)
    o = _proj(xn, w_in, b_in, OFF_B + 3 * MIX_W, MIX_W).astype(f32)
    g = _proj(xn, w_in, b_in, OFF_B + 4 * MIX_W, 4 * N_HEADS).astype(f32).reshape(bsz, s, 4, N_HEADS)

    def run(qd, kd, vd, gi, gf):
        cl = MLSTM_CHUNK
        h = _mlstm_chunkwise(_to_chunks(qd, cl), _to_chunks(kd, cl), _to_chunks(vd, cl),
                             _gate_chunks(gi, cl), _gate_chunks(jax.nn.log_sigmoid(gf), cl))
        return _from_chunks(h)

    def rv(t):
        return jnp.flip(t, axis=1)

    h_fwd = run(q, k, v, g[:, :, 0], g[:, :, 1])
    h_bwd = rv(run(rv(q), rv(k), rv(v), rv(g[:, :, 2]), rv(g[:, :, 3])))
    return (jax.nn.sigmoid(o) * _head_rmsnorm(h_fwd + h_bwd, norm_g)).astype(xn.dtype)


def _hyena_filters(seq_len, w1, b1, fr1, w2, b2, fr2, w3):
    f32 = jnp.float32
    t = jnp.linspace(0.0, 1.0, seq_len, dtype=f32)[:, None]
    omega = (2.0 * math.pi / seq_len) * jnp.arange(seq_len, dtype=f32)[:, None]
    bands = jnp.linspace(1e-4, HYENA_BANDS - 1, HYENA_BANDS, dtype=f32)[None, :]
    feat = jnp.concatenate([t, jnp.cos(omega * bands), -jnp.sin(omega * bands)], axis=-1)
    hid = jnp.sin(fr1.astype(f32) * (feat @ w1.astype(f32) + b1.astype(f32)))
    hid = jnp.sin(fr2.astype(f32) * (hid @ w2.astype(f32) + b2.astype(f32)))
    filt = (hid @ w3.astype(f32)).reshape(seq_len, 2, MIX_W)
    rate = jnp.abs(jnp.linspace(math.log(HYENA_TARGET) / HYENA_FAST, math.log(HYENA_TARGET) / HYENA_SLOW,
                                MIX_W, dtype=f32))
    return filt * jnp.exp(-t * rate)[:, None, :]


def _fft_conv(z, h):
    seq_len = z.shape[1]
    n = 2 * seq_len
    zf = jnp.fft.rfft(z, n=n, axis=1)
    hf = jnp.fft.rfft(h, n=n, axis=0)
    return jnp.fft.irfft(zf * hf[None], n=n, axis=1)[:, :seq_len]


def _hyena_branch(xn, w_in, b_in, conv_w, conv_b, w1, b1, fr1, w2, b2, fr2, w3, skip):
    f32 = jnp.float32
    u = _dwconv(_proj(xn, w_in, b_in, OFF_C, SEG_C), conv_w, conv_b).astype(f32)
    v = u[..., :MIX_W]
    x1 = u[..., MIX_W:2 * MIX_W]
    x2 = u[..., 2 * MIX_W:]
    filt = _hyena_filters(xn.shape[1], w1, b1, fr1, w2, b2, fr2, w3)
    z = x2 * v
    z = (_fft_conv(z, filt[:, 0]) + jnp.flip(_fft_conv(jnp.flip(z, axis=1), filt[:, 1]), axis=1)
         + skip.astype(f32) * z)
    return (x1 * z).astype(xn.dtype)


def _hgrn_lower_bounds(lb_logits):
    p = jax.nn.softmax(lb_logits.astype(jnp.float32), axis=0)
    c = jnp.cumsum(p, axis=0)
    return c - c[0:1]


def _to_scan_chunks(t, cl):
    bsz, s, _ = t.shape
    return t.reshape(bsz, s // cl, cl, N_HEADS, -1).transpose(1, 0, 3, 2, 4)


def _from_scan_chunks(t):
    nc, bsz, nh, cl, d = t.shape
    return t.transpose(1, 0, 3, 2, 4).reshape(bsz, nc * cl, nh * d)


def _hgrn2_chunkwise(q, k, lf, v):
    cl = q.shape[3]
    lower = jnp.tril(jnp.ones((cl, cl), dtype=bool))[:, :, None]

    def step(state, xs):
        qc, kc, lfc, vc = xs
        b = jnp.cumsum(lfc, axis=2)
        o_inter = jnp.einsum('bhjk,bhkv->bhjv', qc * jnp.exp(b), state)
        dec = jnp.exp(jnp.where(lower, b[:, :, :, None, :] - b[:, :, None, :, :], -jnp.inf))
        a_mat = jnp.einsum('bhjk,bhsk,bhjsk->bhjs', qc, kc, dec)
        o_intra = jnp.einsum('bhjs,bhsv->bhjv', a_mat, vc)
        b_last = b[:, :, -1:, :]
        state_new = (jnp.exp(b_last[:, :, 0, :])[..., None] * state
                     + jnp.einsum('bhsk,bhsv->bhkv', kc * jnp.exp(b_last - b), vc))
        return state_new, o_inter + o_intra

    nc, bsz, nh, _, dk = q.shape
    init = jnp.zeros((bsz, nh, dk, v.shape[-1]), q.dtype)
    _, o = lax.scan(step, init, (q, k, lf, v))
    return o


def _hgrn2_branch(xn, w_in, b_in, lb, norm_g):
    f32 = jnp.float32
    q = jax.nn.silu(_proj(xn, w_in, b_in, OFF_D, MIX_W).astype(f32))
    f_fwd = _proj(xn, w_in, b_in, OFF_D + MIX_W, MIX_W).astype(f32)
    f_bwd = _proj(xn, w_in, b_in, OFF_D + 2 * MIX_W, MIX_W).astype(f32)
    i = _proj(xn, w_in, b_in, OFF_D + 3 * MIX_W, MIX_W).astype(f32)
    g = _proj(xn, w_in, b_in, OFF_D + 4 * MIX_W, MIX_W).astype(f32)
    log_lb = jnp.log(lb)
    log_1mlb = jnp.log1p(-lb)

    def run(qd, fd, idd):
        lf = jnp.logaddexp(log_lb, log_1mlb + jax.nn.log_sigmoid(fd))
        kd = (1.0 - lb) * jax.nn.sigmoid(-fd)
        cl = HGRN_CHUNK
        o = _hgrn2_chunkwise(_to_scan_chunks(qd, cl), _to_scan_chunks(kd, cl),
                             _to_scan_chunks(lf, cl), _to_scan_chunks(idd, cl))
        return _from_scan_chunks(o)

    def rv(t):
        return jnp.flip(t, axis=1)

    o_fwd = run(q, f_fwd, i)
    o_bwd = rv(run(rv(q), rv(f_bwd), rv(i)))
    return (jax.nn.sigmoid(g) * _head_rmsnorm(o_fwd + o_bwd, norm_g)).astype(xn.dtype)


def _layer(h, p_i, lp, lb):
    xn = _rmsnorm(h, lp['norm_mix'])
    w_in = lp['w_in']
    b_in = lp['b_in']
    branches = (
        _rglru_branch(xn, w_in, b_in, lp['conv_a_w'], lp['conv_a_b'], lp['rglru_w'], lp['rglru_b'], lp['rglru_lam']),
        _mlstm_branch(xn, w_in, b_in, lp['mlstm_norm']),
        _hyena_branch(xn, w_in, b_in, lp['hyena_conv_w'], lp['hyena_conv_b'], lp['hyena_w1'], lp['hyena_b1'],
                      lp['hyena_freq1'], lp['hyena_w2'], lp['hyena_b2'], lp['hyena_freq2'], lp['hyena_w3'],
                      lp['hyena_skip']),
        _hgrn2_branch(xn, w_in, b_in, lb, lp['hgrn_norm']),
    )
    merged = None
    for idx in range(N_BRANCH):
        gate = jax.nn.sigmoid(_proj(xn, w_in, b_in, OFF_G + idx * D_MODEL, D_MODEL))
        term = gate * (branches[idx] @ lp['w_branch'][idx])
        merged = term if merged is None else merged + term
    h = h + merged @ lp['w_out']
    xn = _rmsnorm(h, lp['norm_ffn'])
    u = _dwconv(xn @ lp['w_up'], lp['ffn_conv_w'], lp['ffn_conv_b'])
    h = h + (jax.nn.gelu(u[..., :D_FF]) * u[..., D_FF:]) @ lp['w_down']
    gate = jax.nn.sigmoid(_rmsnorm(h, lp['norm_ple']) @ lp['w_ple_gate'])
    h = h + gate * (p_i @ lp['w_ple'])
    return h


def _trunk(x, p, params):
    lb_all = _hgrn_lower_bounds(params['hgrn_lb_logits'])
    h = x
    for i in range(DEPTH):
        lp = {name: arr[i] for name, arr in params.items() if name not in SHARED_PARAMS}
        h = _layer(h, p[i], lp, lb_all[i])
    return _rmsnorm(h, params['final_norm'])


def setup_inputs(seed: int = 0) -> dict:
    key = jax.random.key(seed)
    ks = jax.random.split(key, 40)
    f32 = jnp.float32

    def nrm(i, shape, scale):
        return jax.random.normal(ks[i], shape, f32) * scale

    def gain(i, shape):
        return 1.0 + nrm(i, shape, 0.01)

    b_in = nrm(6, (DEPTH, D_IN), 0.01)
    f_cols = np.array([OFF_B + 4 * MIX_W + N_HEADS + j for j in range(N_HEADS)]
                      + [OFF_B + 4 * MIX_W + 3 * N_HEADS + j for j in range(N_HEADS)])
    b_in = b_in.at[:, f_cols].add(jnp.tile(jnp.linspace(3.0, 6.0, N_HEADS, dtype=f32), 2))
    a_c = jax.random.uniform(ks[11], (DEPTH, 2, MIX_W), f32, minval=0.9, maxval=0.999)
    a_base = a_c ** (1.0 / LRU_C)
    rglru_lam = jnp.log(a_base) - jnp.log1p(-a_base)
    return {
        'x_prompt': nrm(0, (BATCH, SEQ, D_MODEL), 1.0),
        'x_sample': nrm(1, (DEC_BATCH, DEC_SEQ, D_MODEL), 1.0),
        'p_prompt': nrm(2, (DEPTH, BATCH, SEQ, D_PLE), 1.0),
        'p_sample': nrm(3, (DEPTH, DEC_BATCH, DEC_SEQ, D_PLE), 1.0),
        'norm_mix': gain(4, (DEPTH, D_MODEL)),
        'w_in': nrm(5, (DEPTH, D_MODEL, D_IN), D_MODEL ** -0.5),
        'b_in': b_in,
        'conv_a_w': nrm(7, (DEPTH, RG_CONV, MIX_W), RG_CONV ** -0.5),
        'conv_a_b': nrm(8, (DEPTH, MIX_W), 0.01),
        'rglru_w': nrm(9, (DEPTH, 2, 2, N_HEADS, HEAD_DIM, HEAD_DIM), HEAD_DIM ** -0.5),
        'rglru_b': nrm(10, (DEPTH, 2, 2, MIX_W), 0.01),
        'rglru_lam': rglru_lam,
        'mlstm_norm': gain(12, (DEPTH, MIX_W)),
        'hyena_conv_w': nrm(13, (DEPTH, SHORT_CONV, SEG_C), SHORT_CONV ** -0.5),
        'hyena_conv_b': nrm(14, (DEPTH, SEG_C), 0.01),
        'hyena_w1': nrm(15, (DEPTH, HYENA_EMB, HYENA_HID), HYENA_EMB ** -0.5),
        'hyena_b1': nrm(16, (DEPTH, HYENA_HID), 0.01),
        'hyena_freq1': 1.0 + nrm(17, (DEPTH, HYENA_HID), 0.1),
        'hyena_w2': nrm(18, (DEPTH, HYENA_HID, HYENA_HID), HYENA_HID ** -0.5),
        'hyena_b2': nrm(19, (DEPTH, HYENA_HID), 0.01),
        'hyena_freq2': 1.0 + nrm(20, (DEPTH, HYENA_HID), 0.1),
        'hyena_w3': nrm(21, (DEPTH, HYENA_HID, 2 * MIX_W), 0.04 * HYENA_HID ** -0.5),
        'hyena_skip': nrm(22, (DEPTH, MIX_W), 1.0),
        'hgrn_lb_logits': nrm(23, (DEPTH, MIX_W), 0.1),
        'hgrn_norm': gain(24, (DEPTH, MIX_W)),
        'w_branch': nrm(25, (DEPTH, N_BRANCH, MIX_W, D_MODEL), MIX_W ** -0.5),
        'w_out': nrm(26, (DEPTH, D_MODEL, D_MODEL), D_MODEL ** -0.5),
        'norm_ffn': gain(27, (DEPTH, D_MODEL)),
        'w_up': nrm(28, (DEPTH, D_MODEL, 2 * D_FF), D_MODEL ** -0.5),
        'ffn_conv_w': nrm(29, (DEPTH, FFN_CONV, 2 * D_FF), FFN_CONV ** -0.5),
        'ffn_conv_b': nrm(30, (DEPTH, 2 * D_FF), 0.01),
        'w_down': nrm(31, (DEPTH, D_FF, D_MODEL), D_FF ** -0.5),
        'norm_ple': gain(32, (DEPTH, D_MODEL)),
        'w_ple_gate': nrm(33, (DEPTH, D_MODEL, D_MODEL), D_MODEL ** -0.5),
        'w_ple': nrm(34, (DEPTH, D_PLE, D_MODEL), D_PLE ** -0.5),
        'final_norm': gain(35, (D_MODEL,)),
    }


def reference(x_prompt, x_sample, p_prompt, p_sample, norm_mix, w_in, b_in, conv_a_w, conv_a_b, rglru_w,
              rglru_b, rglru_lam, mlstm_norm, hyena_conv_w, hyena_conv_b, hyena_w1, hyena_b1, hyena_freq1,
              hyena_w2, hyena_b2, hyena_freq2, hyena_w3, hyena_skip, hgrn_lb_logits, hgrn_norm, w_branch,
              w_out, norm_ffn, w_up, ffn_conv_w, ffn_conv_b, w_down, norm_ple, w_ple_gate, w_ple, final_norm):
    params = {
        'norm_mix': norm_mix, 'w_in': w_in, 'b_in': b_in,
        'conv_a_w': conv_a_w, 'conv_a_b': conv_a_b,
        'rglru_w': rglru_w, 'rglru_b': rglru_b, 'rglru_lam': rglru_lam,
        'mlstm_norm': mlstm_norm,
        'hyena_conv_w': hyena_conv_w, 'hyena_conv_b': hyena_conv_b,
        'hyena_w1': hyena_w1, 'hyena_b1': hyena_b1, 'hyena_freq1': hyena_freq1,
        'hyena_w2': hyena_w2, 'hyena_b2': hyena_b2, 'hyena_freq2': hyena_freq2,
        'hyena_w3': hyena_w3, 'hyena_skip': hyena_skip,
        'hgrn_lb_logits': hgrn_lb_logits, 'hgrn_norm': hgrn_norm,
        'w_branch': w_branch, 'w_out': w_out,
        'norm_ffn': norm_ffn, 'w_up': w_up, 'ffn_conv_w': ffn_conv_w, 'ffn_conv_b': ffn_conv_b, 'w_down': w_down,
        'norm_ple': norm_ple, 'w_ple_gate': w_ple_gate, 'w_ple': w_ple,
        'final_norm': final_norm,
    }
    y_prompt = _trunk(x_prompt, p_prompt, params)
    y_sample = _trunk(x_sample, p_sample, params)
    return (y_prompt, y_sample)
```

```python
import functools
import math

import numpy as np
import jax
import jax.numpy as jnp
from jax import lax
from jax.experimental import pallas as pl
from jax.experimental.pallas import tpu as pltpu

F32 = jnp.float32
BF16 = jnp.bfloat16
HIGHEST = lax.Precision.HIGHEST

N_HEADS = 4
HEAD_DIM = 128
MIX_W = N_HEADS * HEAD_DIM
LRU_C = 8.0
EPS = 1e-6
STAB_INIT = -1e30
HYENA_BANDS = 16
HYENA_HID = 64
HYENA_FAST = 0.3
HYENA_SLOW = 1.5
HYENA_TARGET = 1e-2

MLSTM_CHUNK = 128
HGRN_CHUNK = 128
HGRN_BAND = 8
SCAN_CHUNK = 256
HALO = 16
DFT_N2 = 128
V7X_VMEM_LIMIT = 56 * 1024 * 1024

BF_C, BF_D3, BF_G, BF_A, BF_B = 0, 1536, 3072, 7168, 8192
BF_COLS = 10240
FP_FF, FP_FB, FP_GATE = 0, 512, 1024
FP_COLS = 1536


def _cp(sem, vmem=V7X_VMEM_LIMIT):
    return pltpu.CompilerParams(dimension_semantics=sem, vmem_limit_bytes=vmem)


def _rms(x, g):
    return x * lax.rsqrt(jnp.mean(x * x, axis=-1, keepdims=True) + EPS) * g


def _log_sigmoid(x):
    return jnp.minimum(x, 0.0) - jnp.log1p(jnp.exp(-jnp.abs(x)))


def _dot(a, b):
    return jnp.dot(a, b, preferred_element_type=F32)


def _dot_nt(a, b):
    return lax.dot_general(a, b, (((1,), (1,)), ((), ())), preferred_element_type=F32)


def _dot_tn(a, b):
    return lax.dot_general(a, b, (((0,), (0,)), ((), ())), preferred_element_type=F32)


def _cumsum_mm(tri, x):
    hi = x.astype(BF16)
    r = x - hi.astype(F32)
    mid = r.astype(BF16)
    lo = (r - mid.astype(F32)).astype(BF16)
    return _dot(tri, hi) + _dot(tri, mid) + _dot(tri, lo)


def _proj_body(h_ref, g_ref, w_ref, b_ref, o_ref, xn_ref):
    @pl.when(pl.program_id(1) == 0)
    def _():
        xn_ref[...] = _rms(h_ref[...], g_ref[...]).astype(BF16)

    o_ref[...] = (_dot(xn_ref[...], w_ref[...]) + b_ref[...]).astype(o_ref.dtype)


def _proj(h, g, w, b, out_dtype, tn):
    t, d = h.shape
    n = w.shape[1]
    tm = min(1024, t)
    return pl.pallas_call(
        _proj_body,
        grid=(t // tm, n // tn),
        in_specs=[
            pl.BlockSpec((tm, d), lambda i, j: (i, 0)),
            pl.BlockSpec((1, d), lambda i, j: (0, 0)),
            pl.BlockSpec((d, tn), lambda i, j: (0, j)),
            pl.BlockSpec((1, tn), lambda i, j: (0, j)),
        ],
        out_specs=pl.BlockSpec((tm, tn), lambda i, j: (i, j)),
        out_shape=jax.ShapeDtypeStruct((t, n), out_dtype),
        scratch_shapes=[pltpu.VMEM((tm, d), BF16)],
        compiler_params=_cp(("parallel", "arbitrary")),
        name="in_proj",
    )(h, g, w, b)


def _scan(a, u, rows, n, reverse):
    d = 1
    while d < n:
        if reverse:
            m = rows < n - d
            sh = n - d
        else:
            m = rows >= d
            sh = d
        a_s = jnp.where(m, pltpu.roll(a, sh, 0), 1.0)
        u_s = jnp.where(m, pltpu.roll(u, sh, 0), 0.0)
        u = a * u_s + u
        a = a * a_s
        d *= 2
    return a, u


def _rglru_body(xa_ref, ya_ref, cw_ref, cb_ref, gw_ref, gb_ref, lam_ref, o_ref, xc_ref, hf_ref):
    s = xa_ref.shape[0]
    tc = min(SCAN_CHUNK, s)
    nc = s // tc
    n = tc + 2 * HALO
    rows = lax.broadcasted_iota(jnp.int32, (tc, HEAD_DIM), 0)
    cw = cw_ref[...]
    cb = cb_ref[...]
    sp = jax.nn.softplus(-lam_ref[...])

    def gates(xc, d):
        w = gw_ref[0, :, d * 256:(d + 1) * 256]
        g = jax.nn.sigmoid(_dot(xc.astype(BF16), w) + gb_ref[0, :, d * 256:(d + 1) * 256])
        log_a = (-LRU_C) * g[:, :HEAD_DIM] * sp[d:d + 1]
        a = jnp.exp(log_a)
        x2 = 2.0 * log_a
        one_m_a2 = jnp.where(x2 > -0.01, -x2 * (1.0 + 0.5 * x2 * (1.0 + x2 * (1.0 / 3.0))), 1.0 - a * a)
        u = jnp.sqrt(one_m_a2) * (g[:, HEAD_DIM:] * xc)
        return a, u

    def fwd(c, carry):
        r0 = pl.multiple_of(c * tc, tc)
        x = xa_ref[pl.ds(r0, tc), :].astype(F32)
        rp = pl.multiple_of(jnp.maximum(r0 - HALO, 0), HALO)
        rn = pl.multiple_of(jnp.minimum(r0 + tc, s - HALO), HALO)
        prev = jnp.where(c > 0, xa_ref[pl.ds(rp, HALO), :].astype(F32), 0.0)
        nxt = jnp.where(c < nc - 1, xa_ref[pl.ds(rn, HALO), :].astype(F32), 0.0)
        cat = jnp.concatenate([prev, x, nxt], axis=0)
        xc = (cb + cw[0:1] * pltpu.roll(cat, 1, 0)[HALO:HALO + tc] + cw[1:2] * x
              + cw[2:3] * pltpu.roll(cat, n - 1, 0)[HALO:HALO + tc]
              + cw[3:4] * pltpu.roll(cat, n - 2, 0)[HALO:HALO + tc])
        a, u = gates(xc, 0)
        a, u = _scan(a, u, rows, tc, False)
        h = u + a * carry
        hf_ref[pl.ds(r0, tc), :] = h
        xc_ref[pl.ds(r0, tc), :] = xc
        return h[tc - 1:tc, :]

    lax.fori_loop(0, nc, fwd, jnp.zeros((1, HEAD_DIM), F32))

    def bwd(ci, carry):
        r0 = pl.multiple_of((nc - 1 - ci) * tc, tc)
        xc = xc_ref[pl.ds(r0, tc), :]
        a, u = gates(xc, 1)
        a, u = _scan(a, u, rows, tc, True)
        h = u + a * carry
        y = jax.nn.gelu(ya_ref[pl.ds(r0, tc), :].astype(F32))
        o_ref[pl.ds(r0, tc), :] = ((hf_ref[pl.ds(r0, tc), :] + h) * y).astype(o_ref.dtype)
        return h[0:1, :]

    lax.fori_loop(0, nc, bwd, jnp.zeros((1, HEAD_DIM), F32))


def _rglru(pbf, lp, bsz, s):
    t = pbf.shape[0]
    ca, cy = BF_A // HEAD_DIM, (BF_A + MIX_W) // HEAD_DIM
    return pl.pallas_call(
        _rglru_body,
        grid=(bsz, N_HEADS),
        in_specs=[
            pl.BlockSpec((s, HEAD_DIM), lambda b, h: (b, ca + h)),
            pl.BlockSpec((s, HEAD_DIM), lambda b, h: (b, cy + h)),
            pl.BlockSpec((4, HEAD_DIM), lambda b, h: (0, h)),
            pl.BlockSpec((1, HEAD_DIM), lambda b, h: (0, h)),
            pl.BlockSpec((1, HEAD_DIM, 4 * HEAD_DIM), lambda b, h: (h, 0, 0)),
            pl.BlockSpec((1, 1, 4 * HEAD_DIM), lambda b, h: (h, 0, 0)),
            pl.BlockSpec((2, HEAD_DIM), lambda b, h: (0, h)),
        ],
        out_specs=pl.BlockSpec((s, HEAD_DIM), lambda b, h: (b, h)),
        out_shape=jax.ShapeDtypeStruct((t, MIX_W), BF16),
        scratch_shapes=[pltpu.VMEM((s, HEAD_DIM), F32), pltpu.VMEM((s, HEAD_DIM), F32)],
        compiler_params=_cp(("parallel", "parallel")),
        name="rglru",
    )(pbf, pbf, lp["conv_a_w"], lp["conv_a_b"], lp["rglru_w"], lp["rglru_b"], lp["rglru_lam"])


def _mlstm_body(q_ref, k_ref, v_ref, og_ref, g_ref, ng_ref, o_ref, hf_ref, st_ref):
    s = q_ref.shape[0]
    cl = min(MLSTM_CHUNK, s)
    nc = s // cl
    ri = lax.broadcasted_iota(jnp.int32, (cl, cl), 0)
    ci = lax.broadcasted_iota(jnp.int32, (cl, cl), 1)
    masks = (ri >= ci, ri <= ci)
    tris = (masks[0].astype(BF16), masks[1].astype(BF16))
    e0 = (lax.broadcasted_iota(jnp.int32, (cl, HEAD_DIM), 1) == 0).astype(BF16)
    kscale = HEAD_DIM ** -0.5

    def chunk(r0, d, m_st):
        g = g_ref[pl.ds(r0, cl), :]
        cum = _cumsum_mm(tris[d], _log_sigmoid(g))
        g_t = g.T
        cum_t = cum.T
        ic, fc = 2 * d, 2 * d + 1
        ig_c, ig_r = g[:, ic:ic + 1], g_t[ic:ic + 1, :]
        b_c, b_r = cum[:, fc:fc + 1], cum_t[fc:fc + 1, :]
        tot = b_c[0:1] if d else b_c[cl - 1:cl]
        q = q_ref[pl.ds(r0, cl), :]
        k = k_ref[pl.ds(r0, cl), :].astype(F32) * kscale
        v1 = jnp.concatenate([v_ref[pl.ds(r0, cl), :], e0], axis=1)
        st = st_ref[...]
        lw = tot - b_c + ig_c
        m_loc = jnp.max(lw, axis=0, keepdims=True)
        kw = (k * jnp.exp(lw - m_loc)).astype(BF16)
        cn_loc = _dot_tn(kw, v1)
        dm = jnp.where(masks[d], b_c - b_r + ig_r, -jnp.inf)
        m_inter = b_c + m_st
        m_all = jnp.maximum(jnp.max(dm, axis=1, keepdims=True), m_inter)
        am = jnp.exp(dm - m_all) * _dot_nt(q, k.astype(BF16))
        nd = _dot(am.astype(BF16), v1) + jnp.exp(m_inter - m_all) * _dot(q, st.astype(BF16))
        out = nd[:, :HEAD_DIM] / jnp.maximum(jnp.abs(nd[:, HEAD_DIM:HEAD_DIM + 1]), jnp.exp(-m_all))
        m_new = jnp.maximum(tot + m_st, m_loc)
        st_ref[...] = jnp.exp(tot + m_st - m_new) * st + jnp.exp(m_loc - m_new) * cn_loc
        return out, m_new

    m0 = jnp.full((1, 1), STAB_INIT, F32)
    st_ref[...] = jnp.zeros_like(st_ref)

    def fwd(c, m_st):
        r0 = pl.multiple_of(c * cl, cl)
        out, m_new = chunk(r0, 0, m_st)
        hf_ref[pl.ds(r0, cl), :] = out
        return m_new

    lax.fori_loop(0, nc, fwd, m0)
    st_ref[...] = jnp.zeros_like(st_ref)
    ng = ng_ref[...]

    def bwd(c, m_st):
        r0 = pl.multiple_of((nc - 1 - c) * cl, cl)
        out, m_new = chunk(r0, 1, m_st)
        hs = hf_ref[pl.ds(r0, cl), :] + out
        y = _rms(hs, ng)
        og = jax.nn.sigmoid(og_ref[pl.ds(r0, cl), :].astype(F32))
        o_ref[pl.ds(r0, cl), :] = (og * y).astype(o_ref.dtype)
        return m_new

    lax.fori_loop(0, nc, bwd, m0)


def _mlstm(pbf, pfp, lp, bsz, s):
    t = pbf.shape[0]
    cb = BF_B // HEAD_DIM
    cg = FP_GATE // HEAD_DIM
    blk = lambda off: pl.BlockSpec((s, HEAD_DIM), lambda b, h: (b, off + h))
    return pl.pallas_call(
        _mlstm_body,
        grid=(bsz, N_HEADS),
        in_specs=[blk(cb), blk(cb + 4), blk(cb + 8), blk(cb + 12), blk(cg),
                  pl.BlockSpec((1, HEAD_DIM), lambda b, h: (0, h))],
        out_specs=pl.BlockSpec((s, HEAD_DIM), lambda b, h: (b, h)),
        out_shape=jax.ShapeDtypeStruct((t, MIX_W), BF16),
        scratch_shapes=[pltpu.VMEM((s, HEAD_DIM), F32), pltpu.VMEM((HEAD_DIM, 2 * HEAD_DIM), F32)],
        compiler_params=_cp(("parallel", "parallel")),
        name="mlstm",
    )(pbf, pbf, pbf, pbf, pfp, lp["mlstm_norm"])


def _block_ref(b, m, rev):
    c, w = b.shape
    parts = []
    for blk in range(c // (2 * m)):
        row = blk * 2 * m + (m if rev else m - 1)
        parts.append(jnp.broadcast_to(b[row:row + 1, :], (2 * m, w)))
    return parts[0] if len(parts) == 1 else jnp.concatenate(parts, axis=0)


def _hgrn_body(layer, q_ref, ff_ref, fb_ref, i_ref, g_ref, lbl_ref, ng_ref, o_ref, of_ref, st_ref):
    s = q_ref.shape[0]
    c = min(HGRN_CHUNK, s)
    nc = s // c
    band = min(HGRN_BAND, c)
    lg = lbl_ref[...]
    e = jnp.exp(lg - jnp.max(lg, axis=0, keepdims=True))
    p = e / jnp.sum(e, axis=0, keepdims=True)
    lb = jnp.zeros((1, HEAD_DIM), F32)
    for r in range(1, layer + 1):
        lb = lb + p[r:r + 1, :]
    log_lb = jnp.log(lb)
    log_1m = jnp.log1p(-lb)
    one_m = 1.0 - lb

    ri = lax.broadcasted_iota(jnp.int32, (c, c), 0)
    ci = lax.broadcasted_iota(jnp.int32, (c, c), 1)
    tris = ((ri >= ci).astype(BF16), (ri <= ci).astype(BF16))
    rw = lax.broadcasted_iota(jnp.int32, (c, HEAD_DIM), 0)
    r1 = lax.broadcasted_iota(jnp.int32, (c, 1), 0)

    def chunk(r0, rev):
        f = (fb_ref if rev else ff_ref)[pl.ds(r0, c), :]
        x = log_1m + _log_sigmoid(f)
        mx = jnp.maximum(log_lb, x)
        lf = mx + jnp.log(jnp.exp(log_lb - mx) + jnp.exp(x - mx))
        kd = one_m * jax.nn.sigmoid(-f)
        b = _cumsum_mm(tris[rev], lf)
        tot = b[0:1] if rev else b[c - 1:c]
        qs = jax.nn.silu(q_ref[pl.ds(r0, c), :].astype(F32))
        v = i_ref[pl.ds(r0, c), :]
        st = st_ref[...]
        o = _dot_nt((qs * jnp.exp(b)).astype(BF16), st.astype(BF16))
        k_out = (kd * jnp.exp(tot - b)).astype(BF16)
        st_ref[...] = jnp.exp(tot) * st + _dot_tn(v, k_out)
        amat = jnp.zeros((c, c), F32)
        m = c // 2
        while m >= band:
            bref = _block_ref(b, m, rev)
            up = (rw & (2 * m - 1)) >= m
            qm = jnp.logical_not(up) if rev else up
            km = up if rev else jnp.logical_not(up)
            qt = jnp.where(qm, qs * jnp.exp(jnp.where(qm, b - bref, 0.0)), 0.0).astype(BF16)
            kt = jnp.where(km, kd * jnp.exp(jnp.where(km, bref - b, 0.0)), 0.0).astype(BF16)
            am = _dot_nt(qt, kt)
            if 2 * m < c:
                sh = int(math.log2(2 * m))
                am = jnp.where((ri >> sh) == (ci >> sh), am, 0.0)
            amat = amat + am
            m //= 2
        for dl in range(band):
            if dl == 0:
                ksh, bsh = kd, b
                a = jnp.sum(qs * ksh, axis=1, keepdims=True)
            else:
                sh = (c - dl) if rev else dl
                ksh = pltpu.roll(kd, sh, 0)
                bsh = pltpu.roll(b, sh, 0)
                pos = r1 & (band - 1)
                valid = (pos + dl < band) if rev else (pos >= dl)
                ex = jnp.exp(jnp.where(valid, b - bsh, 0.0))
                a = jnp.where(valid, jnp.sum(qs * ksh * ex, axis=1, keepdims=True), 0.0)
            diag = (ci - ri == dl) if rev else (ri - ci == dl)
            amat = amat + jnp.where(diag, a, 0.0)
        return o + _dot(amat.astype(BF16), v)

    st_ref[...] = jnp.zeros_like(st_ref)

    def fwd(ci_, carry):
        r0 = pl.multiple_of(ci_ * c, c)
        of_ref[pl.ds(r0, c), :] = chunk(r0, 0)
        return carry

    lax.fori_loop(0, nc, fwd, 0)
    st_ref[...] = jnp.zeros_like(st_ref)
    ng = ng_ref[...]

    def bwd(ci_, carry):
        r0 = pl.multiple_of((nc - 1 - ci_) * c, c)
        hs = of_ref[pl.ds(r0, c), :] + chunk(r0, 1)
        gate = jax.nn.sigmoid(g_ref[pl.ds(r0, c), :].astype(F32))
        o_ref[pl.ds(r0, c), :] = (gate * _rms(hs, ng)).astype(o_ref.dtype)
        return carry

    lax.fori_loop(0, nc, bwd, 0)


def _hgrn(pbf, pfp, lp, lb_logits, layer, bsz, s):
    t = pbf.shape[0]
    cd = BF_D3 // HEAD_DIM
    blk = lambda off: pl.BlockSpec((s, HEAD_DIM), lambda b, h: (b, off + h))
    depth = lb_logits.shape[0]
    return pl.pallas_call(
        functools.partial(_hgrn_body, layer),
        grid=(bsz, N_HEADS),
        in_specs=[blk(cd), blk(FP_FF // HEAD_DIM), blk(FP_FB // HEAD_DIM), blk(cd + 4), blk(cd + 8),
                  pl.BlockSpec((depth, HEAD_DIM), lambda b, h: (0, h)),
                  pl.BlockSpec((1, HEAD_DIM), lambda b, h: (0, h))],
        out_specs=pl.BlockSpec((s, HEAD_DIM), lambda b, h: (b, h)),
        out_shape=jax.ShapeDtypeStruct((t, MIX_W), BF16),
        scratch_shapes=[pltpu.VMEM((s, HEAD_DIM), F32), pltpu.VMEM((HEAD_DIM, HEAD_DIM), F32)],
        compiler_params=_cp(("parallel", "parallel")),
        name="hgrn2",
    )(pbf, pfp, pfp, pbf, pbf, lb_logits, lp["hgrn_norm"])


def _shift3(cat, w, bias, ts):
    n = ts + 2 * HALO
    return (bias + w[0:1] * pltpu.roll(cat, 1, 0)[HALO:HALO + ts] + w[1:2] * cat[HALO:HALO + ts]
            + w[2:3] * pltpu.roll(cat, n - 1, 0)[HALO:HALO + ts])


def _hyena_prep_body(x_ref, xp_ref, xn_ref, cw_ref, cb_ref, z_ref, x1_ref):
    i = pl.program_id(1)
    ts = x_ref.shape[0]
    prev = jnp.where(i > 0, xp_ref[...].astype(F32), 0.0)
    nxt = jnp.where(i < pl.num_programs(1) - 1, xn_ref[...].astype(F32), 0.0)
    cat = jnp.concatenate([prev, x_ref[...].astype(F32), nxt], axis=0)
    u = _shift3(cat, cw_ref[...], cb_ref[...], ts)
    z_ref[...] = (u[:, 2 * MIX_W:] * u[:, :MIX_W]).astype(z_ref.dtype)
    x1_ref[...] = u[:, MIX_W:2 * MIX_W].astype(x1_ref.dtype)


def _hyena_prep(pbf, lp, bsz, s):
    t = pbf.shape[0]
    ts = min(512, s)
    per = s // ts
    hb = ts // HALO
    nb = t // HALO
    w = 3 * MIX_W
    return pl.pallas_call(
        _hyena_prep_body,
        grid=(bsz, per),
        in_specs=[
            pl.BlockSpec((ts, w), lambda b, i: (b * per + i, 0)),
            pl.BlockSpec((HALO, w), lambda b, i: (jnp.maximum((b * per + i) * hb - 1, 0), 0)),
            pl.BlockSpec((HALO, w), lambda b, i: (jnp.minimum((b * per + i + 1) * hb, nb - 1), 0)),
            pl.BlockSpec((3, w), lambda b, i: (0, 0)),
            pl.BlockSpec((1, w), lambda b, i: (0, 0)),
        ],
        out_specs=[pl.BlockSpec((ts, MIX_W), lambda b, i: (b * per + i, 0)),
                   pl.BlockSpec((ts, MIX_W), lambda b, i: (b * per + i, 0))],
        out_shape=[jax.ShapeDtypeStruct((t, MIX_W), BF16), jax.ShapeDtypeStruct((t, MIX_W), BF16)],
        compiler_params=_cp(("parallel", "arbitrary")),
        name="hyena_prep",
    )(pbf, pbf, pbf, lp["hyena_conv_w"], lp["hyena_conv_b"])


def _dft1_body(prec, x_ref, f_ref, o_ref):
    o_ref[...] = jnp.dot(f_ref[...], x_ref[...], preferred_element_type=F32, precision=prec).astype(o_ref.dtype)


def _dft1(x3, f1, out_dtype, prec=None):
    bsz, n1h, lanes = x3.shape
    tl = min(4096, lanes)
    return pl.pallas_call(
        functools.partial(_dft1_body, prec),
        grid=(bsz, lanes // tl),
        in_specs=[pl.BlockSpec((None, n1h, tl), lambda b, j: (b, 0, j)),
                  pl.BlockSpec(f1.shape, lambda b, j: (0, 0))],
        out_specs=pl.BlockSpec((None, f1.shape[0], tl), lambda b, j: (b, 0, j)),
        out_shape=jax.ShapeDtypeStruct((bsz, f1.shape[0], lanes), out_dtype),
        compiler_params=_cp(("parallel", "parallel")),
        name="dft_stage1",
    )(x3, f1)


def _dft2_conv_body(kb, a_ref, f_ref, fi_ref, g_ref, o_ref):
    n2 = a_ref.shape[2]
    for kk in range(kb):
        a = jnp.concatenate([a_ref[0, kk], a_ref[1, kk]], axis=0)
        x = _dot(f_ref[kk], a)
        xr, xi = x[:n2], x[n2:]
        gr, gi = g_ref[kk, 0], g_ref[kk, 1]
        y = jnp.concatenate([xr * gr - xi * gi, xr * gi + xi * gr], axis=0).astype(BF16)
        bq = _dot(fi_ref[kk], y)
        o_ref[0, kk] = bq[:n2].astype(o_ref.dtype)
        o_ref[1, kk] = bq[n2:].astype(o_ref.dtype)


def _dft2_conv(a5, f2, f2i, gspec, kb):
    bsz, _, n1, n2, c = a5.shape
    return pl.pallas_call(
        functools.partial(_dft2_conv_body, kb),
        grid=(n1 // kb, bsz),
        in_specs=[pl.BlockSpec((None, 2, kb, n2, c), lambda k, b: (b, 0, k, 0, 0)),
                  pl.BlockSpec((kb, 2 * n2, 2 * n2), lambda k, b: (k, 0, 0)),
                  pl.BlockSpec((kb, 2 * n2, 2 * n2), lambda k, b: (k, 0, 0)),
                  pl.BlockSpec((kb, 2, n2, c), lambda k, b: (k, 0, 0, 0))],
        out_specs=pl.BlockSpec((None, 2, kb, n2, c), lambda k, b: (b, 0, k, 0, 0)),
        out_shape=jax.ShapeDtypeStruct(a5.shape, BF16),
        compiler_params=_cp(("parallel", "arbitrary")),
        name="dft_stage2_conv",
    )(a5, f2, f2i, gspec)


def _dft2_filter_body(kb, a_ref, f_ref, o_ref):
    n2 = a_ref.shape[3]
    k0 = pl.program_id(0) * kb
    for kk in range(kb):
        sgn = (1 - 2 * ((k0 + kk) & 1)).astype(F32)
        xs = []
        for e in range(2):
            a = jnp.concatenate([a_ref[e, 0, kk], a_ref[e, 1, kk]], axis=0)
            xs.append(jnp.dot(f_ref[kk], a, preferred_element_type=F32, precision=HIGHEST))
        x = xs[0] + sgn * xs[1]
        o_ref[kk, 0] = x[:n2]
        o_ref[kk, 1] = x[n2:]


def _dft2_filter(a5, f2, kb):
    _, _, n1, n2, c = a5.shape
    return pl.pallas_call(
        functools.partial(_dft2_filter_body, kb),
        grid=(n1 // kb,),
        in_specs=[pl.BlockSpec((2, 2, kb, n2, c), lambda k: (0, 0, k, 0, 0)),
                  pl.BlockSpec((kb, 2 * n2, 2 * n2), lambda k: (k, 0, 0))],
        out_specs=pl.BlockSpec((kb, 2, n2, c), lambda k: (k, 0, 0, 0)),
        out_shape=jax.ShapeDtypeStruct((n1, 2, n2, c), F32),
        compiler_params=_cp(("parallel",)),
        name="dft_stage2_filter",
    )(a5, f2)


def _dft3_body(b_ref, f_ref, x1_ref, z_ref, sk_ref, o_ref):
    y = _dot(f_ref[...], b_ref[...])
    z = z_ref[...].astype(F32)
    o_ref[...] = (x1_ref[...].astype(F32) * (y + sk_ref[...] * z)).astype(o_ref.dtype)


def _dft3(b3, f1i, x1_3, z3, skip_t):
    bsz, n1x2, lanes = b3.shape
    n1h = f1i.shape[0]
    tl = skip_t.shape[1]
    return pl.pallas_call(
        _dft3_body,
        grid=(bsz, lanes // tl),
        in_specs=[pl.BlockSpec((None, n1x2, tl), lambda b, j: (b, 0, j)),
                  pl.BlockSpec(f1i.shape, lambda b, j: (0, 0)),
                  pl.BlockSpec((None, n1h, tl), lambda b, j: (b, 0, j)),
                  pl.BlockSpec((None, n1h, tl), lambda b, j: (b, 0, j)),
                  pl.BlockSpec((1, tl), lambda b, j: (0, 0))],
        out_specs=pl.BlockSpec((None, n1h, tl), lambda b, j: (b, 0, j)),
        out_shape=jax.ShapeDtypeStruct((bsz, n1h, lanes), BF16),
        compiler_params=_cp(("parallel", "parallel")),
        name="dft_stage3",
    )(b3, f1i, x1_3, z3, skip_t)


def _hyena_filter_body(seq_len, pos_ref, sc_ref, bands_ref, w1t_ref, w1c_ref, w1s_ref, b1_ref, fr1_ref,
                       w2_ref, b2_ref, fr2_ref, w3_ref, rate_ref, o_ref):
    pos = pos_ref[...]
    t = pos * (1.0 / (seq_len - 1))
    arg = (pos * (2.0 * math.pi / seq_len)) * bands_ref[...]
    hdot = lambda a, b: jnp.dot(a, b, preferred_element_type=F32, precision=HIGHEST)
    pre = t * w1t_ref[...] + hdot(jnp.cos(arg), w1c_ref[...]) - hdot(jnp.sin(arg), w1s_ref[...]) + b1_ref[...]
    hid = jnp.sin(fr1_ref[...] * pre)
    hid = jnp.sin(fr2_ref[...] * (hdot(hid, w2_ref[...]) + b2_ref[...]))
    o_ref[...] = hdot(hid, w3_ref[...]) * jnp.exp(-t * rate_ref[...]) * sc_ref[...]


def _hyena_filter(lp, s):
    ts = min(256, s)
    r = 2 * s + ts
    pos = np.concatenate([np.arange(s), s - np.arange(s), np.zeros(ts)]).astype(np.float32)[:, None]
    sc = np.ones((r, 1), np.float32)
    sc[s] = 0.0
    pad = HEAD_DIM - HYENA_HID
    bands = np.zeros((1, HEAD_DIM), np.float32)
    bands[0, :HYENA_BANDS] = np.linspace(1e-4, HYENA_BANDS - 1, HYENA_BANDS)
    rate = np.abs(np.linspace(math.log(HYENA_TARGET) / HYENA_FAST, math.log(HYENA_TARGET) / HYENA_SLOW, MIX_W))
    rate = np.tile(rate, 2).astype(np.float32)[None, :]
    w1 = lp["hyena_w1"]
    padc = lambda a: jnp.pad(a, ((0, 0), (0, pad)))
    w1t = padc(w1[0:1])
    w1c = jnp.pad(w1[1:1 + HYENA_BANDS], ((0, HEAD_DIM - HYENA_BANDS), (0, pad)))
    w1s = jnp.pad(w1[1 + HYENA_BANDS:], ((0, HEAD_DIM - HYENA_BANDS), (0, pad)))
    w2 = jnp.pad(lp["hyena_w2"], ((0, pad), (0, pad)))
    w3 = jnp.pad(lp["hyena_w3"], ((0, pad), (0, 0)))
    vec = lambda a: padc(a[None, :])
    full = lambda a: pl.BlockSpec(a.shape, lambda i: (0, 0))
    args = [jnp.asarray(bands), w1t, w1c, w1s, vec(lp["hyena_b1"]), vec(lp["hyena_freq1"]), w2,
            vec(lp["hyena_b2"]), vec(lp["hyena_freq2"]), w3, jnp.asarray(rate)]
    out = pl.pallas_call(
        functools.partial(_hyena_filter_body, s),
        grid=(r // ts,),
        in_specs=[pl.BlockSpec((ts, 1), lambda i: (i, 0)), pl.BlockSpec((ts, 1), lambda i: (i, 0))]
                 + [full(a) for a in args],
        out_specs=pl.BlockSpec((ts, 2 * MIX_W), lambda i: (i, 0)),
        out_shape=jax.ShapeDtypeStruct((r, 2 * MIX_W), F32),
        compiler_params=_cp(("parallel",)),
        name="hyena_filter",
    )(jnp.asarray(pos), jnp.asarray(sc), *args)
    return out[:s, :MIX_W], out[s:2 * s, MIX_W:], out[2 * s:2 * s + 1, MIX_W:]


def _dft_tables(s):
    n = 2 * s
    n2 = min(DFT_N2, s // 8)
    n1 = n // n2
    n1h = n1 // 2
    two_pi = 2.0 * math.pi
    k1 = jnp.arange(n1, dtype=jnp.int32)
    a1 = (two_pi / n1) * ((k1[:, None] * k1[None, :n1h]) % n1).astype(F32)
    f1 = jnp.concatenate([jnp.cos(a1), -jnp.sin(a1)], axis=0)
    f1i = jnp.concatenate([jnp.cos(a1).T, -jnp.sin(a1).T], axis=1) * (1.0 / n)
    j = jnp.arange(n2, dtype=jnp.int32)
    ph = (j[None, :, None] * j[None, None, :] * n1 + j[None, None, :] * k1[:, None, None]) % n
    a2 = (two_pi / n) * ph.astype(F32)
    tr, ti = jnp.cos(a2), -jnp.sin(a2)
    f2 = jnp.concatenate([jnp.concatenate([tr, -ti], axis=2), jnp.concatenate([ti, tr], axis=2)], axis=1)
    return dict(n1=n1, n2=n2, f1=f1, f1i=f1i, f2=f2, f2i=jnp.swapaxes(f2, 1, 2))


def _hyena_spectrum(lp, s, tab):
    gpos, gneg, hb0 = _hyena_filter(lp, s)
    n1, n2 = tab["n1"], tab["n2"]
    g3 = jnp.stack([gpos, gneg]).reshape(2, n1 // 2, n2 * MIX_W)
    a = _dft1(g3, tab["f1"], F32, HIGHEST)
    kb = min(2, n1)
    gspec = _dft2_filter(a.reshape(2, 2, n1, n2, MIX_W), tab["f2"], kb)
    return gspec, hb0


def _hyena(pbf, lp, gspec, hb0, tab, bsz, s):
    n1, n2 = tab["n1"], tab["n2"]
    lanes = n2 * MIX_W
    z, x1 = _hyena_prep(pbf, lp, bsz, s)
    z3 = z.reshape(bsz, n1 // 2, lanes)
    a = _dft1(z3, tab["f1"].astype(BF16), BF16)
    kb = min(2, n1)
    bq = _dft2_conv(a.reshape(bsz, 2, n1, n2, MIX_W), tab["f2"].astype(BF16), tab["f2i"].astype(BF16), gspec, kb)
    tl = min(4096, lanes)
    skip_t = jnp.tile(lp["hyena_skip"][None, :] + hb0, (1, tl // MIX_W))
    out = _dft3(bq.reshape(bsz, 2 * n1, lanes), tab["f1i"].astype(BF16), x1.reshape(bsz, n1 // 2, lanes), z3, skip_t)
    return out.reshape(bsz * s, MIX_W)


def _merge_body(h_ref, g0_ref, g1_ref, g2_ref, g3_ref, ba_ref, bb_ref, bc_ref, bd_ref, wb_ref, wo_ref, o_ref):
    merged = None
    for idx, (g_ref, br) in enumerate(zip((g0_ref, g1_ref, g2_ref, g3_ref), (ba_ref, bb_ref, bc_ref, bd_ref))):
        term = jax.nn.sigmoid(g_ref[...].astype(F32)) * _dot(br[...], wb_ref[idx])
        merged = term if merged is None else merged + term
    o_ref[...] = h_ref[...] + _dot(merged.astype(BF16), wo_ref[...])


def _merge(h, pbf, branches, lp):
    t, d = h.shape
    tm = min(512, t)
    gcol = BF_G // d
    row = lambda w: pl.BlockSpec((tm, w), lambda i: (i, 0))
    gate = lambda idx: pl.BlockSpec((tm, d), lambda i: (i, gcol + idx))
    return pl.pallas_call(
        _merge_body,
        grid=(t // tm,),
        in_specs=[row(d), gate(0), gate(1), gate(2), gate(3),
                  row(MIX_W), row(MIX_W), row(MIX_W), row(MIX_W),
                  pl.BlockSpec(lp["w_branch"].shape, lambda i: (0, 0, 0)),
                  pl.BlockSpec(lp["w_out"].shape, lambda i: (0, 0))],
        out_specs=row(d),
        out_shape=jax.ShapeDtypeStruct((t, d), F32),
        compiler_params=_cp(("parallel",)),
        name="merge",
    )(h, pbf, pbf, pbf, pbf, *branches, lp["w_branch"], lp["w_out"])


def _ffn_body(per, final, h_ref, hp_ref, hn_ref, p_ref, gf_ref, wu_ref, cw_ref, cb_ref, wd_ref, gp_ref, wg_ref,
              wp_ref, gl_ref, o_ref):
    i = pl.program_id(0)
    tm = h_ref.shape[0]
    dff = wd_ref.shape[0]
    gf = gf_ref[...]
    h = h_ref[...]
    first = (i % per) == 0
    last = (i % per) == per - 1
    xp = jnp.where(first, 0.0, _rms(hp_ref[...], gf))
    xn = jnp.where(last, 0.0, _rms(hn_ref[...], gf))
    cat = jnp.concatenate([xp, _rms(h, gf), xn], axis=0).astype(BF16)
    cb_w = min(512, dff)
    acc = jnp.zeros_like(h)
    for cb in range(dff // cb_w):
        lo, lo2 = cb * cb_w, dff + cb * cb_w
        u1 = _shift3(_dot(cat, wu_ref[:, lo:lo + cb_w]), cw_ref[:, lo:lo + cb_w], cb_ref[:, lo:lo + cb_w], tm)
        u2 = _shift3(_dot(cat, wu_ref[:, lo2:lo2 + cb_w]), cw_ref[:, lo2:lo2 + cb_w], cb_ref[:, lo2:lo2 + cb_w], tm)
        acc = acc + _dot((jax.nn.gelu(u1) * u2).astype(BF16), wd_ref[lo:lo + cb_w, :])
    h = h + acc
    gate = jax.nn.sigmoid(_dot(_rms(h, gp_ref[...]).astype(BF16), wg_ref[...]))
    h = h + gate * _dot(p_ref[...].astype(BF16), wp_ref[...])
    if final:
        h = _rms(h, gl_ref[...])
    o_ref[...] = h


def _ffn(h, p, lp, final_norm, s, final):
    t, d = h.shape
    tm = min(512, s)
    per = s // tm
    hb = tm // HALO
    nb = t // HALO
    full = lambda a: pl.BlockSpec(a.shape, lambda i: (0,) * a.ndim)
    row = lambda w: pl.BlockSpec((tm, w), lambda i: (i, 0))
    ws = [lp["norm_ffn"], lp["w_up"], lp["ffn_conv_w"], lp["ffn_conv_b"], lp["w_down"], lp["norm_ple"],
          lp["w_ple_gate"], lp["w_ple"], final_norm]
    return pl.pallas_call(
        functools.partial(_ffn_body, per, final),
        grid=(t // tm,),
        in_specs=[row(d),
                  pl.BlockSpec((HALO, d), lambda i: (jnp.maximum(i * hb - 1, 0), 0)),
                  pl.BlockSpec((HALO, d), lambda i: (jnp.minimum((i + 1) * hb, nb - 1), 0)),
                  row(p.shape[1])] + [full(a) for a in ws],
        out_specs=row(d),
        out_shape=jax.ShapeDtypeStruct((t, d), F32),
        compiler_params=_cp(("parallel",)),
        name="ffn_ple",
    )(h, h, h, p, *ws)


def _prepare_params(prm):
    d_model = prm["w_in"].shape[1]
    depth = prm["w_in"].shape[0]
    off_b = 2 * MIX_W
    off_gate = off_b + 4 * MIX_W
    off_c = off_gate + 4 * N_HEADS
    off_d = off_c + 3 * MIX_W
    off_g = off_d + 5 * MIX_W

    def regroup(a):
        sl = lambda lo, w: a[..., lo:lo + w]
        bf = jnp.concatenate([sl(off_c, 3 * MIX_W), sl(off_d, MIX_W), sl(off_d + 3 * MIX_W, 2 * MIX_W),
                              sl(off_g, 4 * d_model), sl(0, 2 * MIX_W), sl(off_b, 4 * MIX_W)], axis=-1)
        gates = sl(off_gate, 4 * N_HEADS).reshape(a.shape[:-1] + (4, N_HEADS))
        gates = jnp.swapaxes(gates, -1, -2)
        gates = jnp.pad(gates, [(0, 0)] * (gates.ndim - 1) + [(0, HEAD_DIM - 4)])
        fp = jnp.concatenate([sl(off_d + MIX_W, 2 * MIX_W), gates.reshape(a.shape[:-1] + (N_HEADS * HEAD_DIM,))],
                             axis=-1)
        return bf, fp

    w_bf, w_fp = regroup(prm["w_in"])
    b_bf, b_fp = regroup(prm["b_in"][:, None, :])
    gw = jnp.transpose(prm["rglru_w"], (0, 3, 4, 1, 2, 5)).reshape(depth, N_HEADS, HEAD_DIM, 4 * HEAD_DIM)
    gb = prm["rglru_b"].reshape(depth, 2, 2, N_HEADS, HEAD_DIM)
    gb = jnp.transpose(gb, (0, 3, 1, 2, 4)).reshape(depth, N_HEADS, 1, 4 * HEAD_DIM)
    row = lambda a: a[:, None, :]
    out = dict(
        norm_mix=row(prm["norm_mix"]), w_bf=w_bf.astype(BF16), b_bf=b_bf, w_fp=w_fp.astype(BF16), b_fp=b_fp,
        conv_a_w=prm["conv_a_w"], conv_a_b=row(prm["conv_a_b"]), rglru_w=gw.astype(BF16), rglru_b=gb,
        rglru_lam=prm["rglru_lam"], mlstm_norm=row(prm["mlstm_norm"]),
        hyena_conv_w=prm["hyena_conv_w"], hyena_conv_b=row(prm["hyena_conv_b"]),
        hyena_w1=prm["hyena_w1"], hyena_b1=prm["hyena_b1"], hyena_freq1=prm["hyena_freq1"],
        hyena_w2=prm["hyena_w2"], hyena_b2=prm["hyena_b2"], hyena_freq2=prm["hyena_freq2"],
        hyena_w3=prm["hyena_w3"], hyena_skip=prm["hyena_skip"], hgrn_norm=row(prm["hgrn_norm"]),
        w_branch=prm["w_branch"].astype(BF16), w_out=prm["w_out"].astype(BF16),
        norm_ffn=row(prm["norm_ffn"]), w_up=prm["w_up"].astype(BF16), ffn_conv_w=prm["ffn_conv_w"],
        ffn_conv_b=row(prm["ffn_conv_b"]), w_down=prm["w_down"].astype(BF16), norm_ple=row(prm["norm_ple"]),
        w_ple_gate=prm["w_ple_gate"].astype(BF16), w_ple=prm["w_ple"].astype(BF16),
    )
    return out


def _trunk(x, p, prm, lb_logits, final_norm):
    bsz, s, d = x.shape
    depth = p.shape[0]
    h = x.reshape(bsz * s, d)
    tab = _dft_tables(s)
    for i in range(depth):
        lp = {k: v[i] for k, v in prm.items()}
        pbf = _proj(h, lp["norm_mix"], lp["w_bf"], lp["b_bf"], BF16, 1280)
        pfp = _proj(h, lp["norm_mix"], lp["w_fp"], lp["b_fp"], F32, 512)
        gspec, hb0 = _hyena_spectrum(lp, s, tab)
        branches = (_rglru(pbf, lp, bsz, s), _mlstm(pbf, pfp, lp, bsz, s),
                    _hyena(pbf, lp, gspec, hb0, tab, bsz, s), _hgrn(pbf, pfp, lp, lb_logits, i, bsz, s))
        h = _merge(h, pbf, branches, lp)
        h = _ffn(h, p[i].reshape(bsz * s, -1), lp, final_norm, s, i == depth - 1)
    return h.reshape(bsz, s, d)


def kernel(x_prompt, x_sample, p_prompt, p_sample, norm_mix, w_in, b_in, conv_a_w, conv_a_b, rglru_w, rglru_b, rglru_lam, mlstm_norm, hyena_conv_w, hyena_conv_b, hyena_w1, hyena_b1, hyena_freq1, hyena_w2, hyena_b2, hyena_freq2, hyena_w3, hyena_skip, hgrn_lb_logits, hgrn_norm, w_branch, w_out, norm_ffn, w_up, ffn_conv_w, ffn_conv_b, w_down, norm_ple, w_ple_gate, w_ple, final_norm):
    prm = _prepare_params(dict(
        norm_mix=norm_mix, w_in=w_in, b_in=b_in, conv_a_w=conv_a_w, conv_a_b=conv_a_b, rglru_w=rglru_w,
        rglru_b=rglru_b, rglru_lam=rglru_lam, mlstm_norm=mlstm_norm, hyena_conv_w=hyena_conv_w,
        hyena_conv_b=hyena_conv_b, hyena_w1=hyena_w1, hyena_b1=hyena_b1, hyena_freq1=hyena_freq1,
        hyena_w2=hyena_w2, hyena_b2=hyena_b2, hyena_freq2=hyena_freq2, hyena_w3=hyena_w3,
        hyena_skip=hyena_skip, hgrn_norm=hgrn_norm, w_branch=w_branch, w_out=w_out, norm_ffn=norm_ffn,
        w_up=w_up, ffn_conv_w=ffn_conv_w, ffn_conv_b=ffn_conv_b, w_down=w_down, norm_ple=norm_ple,
        w_ple_gate=w_ple_gate, w_ple=w_ple))
    fn = final_norm[None, :]
    y_prompt = _trunk(x_prompt, p_prompt, prm, hgrn_lb_logits, fn)
    y_sample = _trunk(x_sample, p_sample, prm, hgrn_lb_logits, fn)
    return (y_prompt, y_sample)
```

```python
import functools
import math

import numpy as np
import jax
import jax.numpy as jnp
from jax import lax
from jax.experimental import pallas as pl
from jax.experimental.pallas import tpu as pltpu

F32 = jnp.float32
BF16 = jnp.bfloat16
HIGHEST = lax.Precision.HIGHEST

N_HEADS = 4
HEAD_DIM = 128
MIX_W = N_HEADS * HEAD_DIM
LRU_C = 8.0
EPS = 1e-6
STAB_INIT = -1e30
HYENA_BANDS = 16
HYENA_HID = 64
HYENA_FAST = 0.3
HYENA_SLOW = 1.5
HYENA_TARGET = 1e-2

MLSTM_CHUNK = 128
HGRN_CHUNK = 128
HGRN_BAND = 4
SCAN_CHUNK = 256
HALO = 16
SUBLANES = 8
DFT_N2 = 128
V7X_VMEM_LIMIT = 56 * 1024 * 1024

BF_C, BF_D3, BF_G, BF_A, BF_B = 0, 1536, 3072, 7168, 8192
BF_COLS = 10240
FP_FF, FP_FB, FP_GATE = 0, 512, 1024
FP_COLS = 1536


def _cp(sem, vmem=V7X_VMEM_LIMIT):
    return pltpu.CompilerParams(dimension_semantics=sem, vmem_limit_bytes=vmem)


def _rms(x, g):
    return x * lax.rsqrt(jnp.mean(x * x, axis=-1, keepdims=True) + EPS) * g


def _log_sigmoid(x):
    return jnp.minimum(x, 0.0) - jnp.log(1.0 + jnp.exp(-jnp.abs(x)))


def _dot(a, b):
    return jnp.dot(a, b, preferred_element_type=F32)


def _dot_nt(a, b):
    return lax.dot_general(a, b, (((1,), (1,)), ((), ())), preferred_element_type=F32)


def _dot_tn(a, b):
    return lax.dot_general(a, b, (((0,), (0,)), ((), ())), preferred_element_type=F32)


def _cumsum_mm(tri, x):
    hi = x.astype(BF16)
    r = x - hi.astype(F32)
    mid = r.astype(BF16)
    lo = (r - mid.astype(F32)).astype(BF16)
    return _dot(tri, hi) + _dot(tri, mid) + _dot(tri, lo)


def _proj_body(h_ref, g_ref, w_ref, b_ref, o_ref, xn_ref):
    @pl.when(pl.program_id(1) == 0)
    def _():
        xn_ref[...] = _rms(h_ref[...], g_ref[...]).astype(BF16)

    o_ref[...] = (_dot(xn_ref[...], w_ref[...]) + b_ref[...]).astype(o_ref.dtype)


def _proj(h, g, w, b, out_dtype, tn):
    t, d = h.shape
    n = w.shape[1]
    tm = min(1024, t)
    return pl.pallas_call(
        _proj_body,
        grid=(t // tm, n // tn),
        in_specs=[
            pl.BlockSpec((tm, d), lambda i, j: (i, 0)),
            pl.BlockSpec((1, d), lambda i, j: (0, 0)),
            pl.BlockSpec((d, tn), lambda i, j: (0, j)),
            pl.BlockSpec((1, tn), lambda i, j: (0, j)),
        ],
        out_specs=pl.BlockSpec((tm, tn), lambda i, j: (i, j)),
        out_shape=jax.ShapeDtypeStruct((t, n), out_dtype),
        scratch_shapes=[pltpu.VMEM((tm, d), BF16)],
        compiler_params=_cp(("parallel", "arbitrary")),
        name="in_proj",
    )(h, g, w, b)


def _scan8(a, u, pos, reverse):
    n = a.shape[0]
    d = 1
    while d < SUBLANES:
        if reverse:
            m = pos < SUBLANES - d
            sh = n - d
        else:
            m = pos >= d
            sh = d
        a_s = jnp.where(m, pltpu.roll(a, sh, 0), 1.0)
        u_s = jnp.where(m, pltpu.roll(u, sh, 0), 0.0)
        u = a * u_s + u
        a = a * a_s
        d *= 2
    return a, u


def _chain8(a, u, carry, reverse):
    groups = a.shape[0] // SUBLANES
    outs = [None] * groups
    for gi in (range(groups - 1, -1, -1) if reverse else range(groups)):
        lo = gi * SUBLANES
        h = u[lo:lo + SUBLANES] + a[lo:lo + SUBLANES] * carry
        outs[gi] = h
        carry = h[0:1] if reverse else h[SUBLANES - 1:SUBLANES]
    return jnp.concatenate(outs, axis=0), carry


def _rglru_body(xa_ref, ya_ref, cw_ref, cb_ref, gw_ref, gb_ref, lam_ref, o_ref, xc_ref, hf_ref, hb_ref):
    s = xa_ref.shape[0]
    tc = min(SCAN_CHUNK, s)
    nc = s // tc
    n = tc + 2 * HALO
    pos = lax.broadcasted_iota(jnp.int32, (tc, HEAD_DIM), 0) & (SUBLANES - 1)
    cw = cw_ref[...]
    cb = cb_ref[...]
    sp = jax.nn.softplus(-lam_ref[...])

    def gates(xc, d):
        w = gw_ref[0, :, d * 256:(d + 1) * 256]
        g = jax.nn.sigmoid(_dot(xc.astype(BF16), w) + gb_ref[0, :, d * 256:(d + 1) * 256])
        log_a = (-LRU_C) * g[:, :HEAD_DIM] * sp[d:d + 1]
        a = jnp.exp(log_a)
        x2 = 2.0 * log_a
        one_m_a2 = jnp.where(x2 > -0.01, -x2 * (1.0 + 0.5 * x2 * (1.0 + x2 * (1.0 / 3.0))), 1.0 - a * a)
        u = jnp.sqrt(one_m_a2) * (g[:, HEAD_DIM:] * xc)
        return a, u

    def conv(c, carry):
        r0 = pl.multiple_of(c * tc, tc)
        x = xa_ref[pl.ds(r0, tc), :].astype(F32)
        rp = pl.multiple_of(jnp.maximum(r0 - HALO, 0), HALO)
        rn = pl.multiple_of(jnp.minimum(r0 + tc, s - HALO), HALO)
        prev = jnp.where(c > 0, xa_ref[pl.ds(rp, HALO), :].astype(F32), 0.0)
        nxt = jnp.where(c < nc - 1, xa_ref[pl.ds(rn, HALO), :].astype(F32), 0.0)
        cat = jnp.concatenate([prev, x, nxt], axis=0)
        xc_ref[pl.ds(r0, tc), :] = (
            cb + cw[0:1] * pltpu.roll(cat, 1, 0)[HALO:HALO + tc] + cw[1:2] * x
            + cw[2:3] * pltpu.roll(cat, n - 1, 0)[HALO:HALO + tc]
            + cw[3:4] * pltpu.roll(cat, n - 2, 0)[HALO:HALO + tc])
        return carry

    lax.fori_loop(0, nc, conv, 0)

    def scan(i, carry):
        c_f, c_b = carry
        rf = pl.multiple_of(i * tc, tc)
        rb = pl.multiple_of((nc - 1 - i) * tc, tc)
        a, u = gates(xc_ref[pl.ds(rf, tc), :], 0)
        a, u = _scan8(a, u, pos, False)
        h, c_f = _chain8(a, u, c_f, False)
        hf_ref[pl.ds(rf, tc), :] = h
        a, u = gates(xc_ref[pl.ds(rb, tc), :], 1)
        a, u = _scan8(a, u, pos, True)
        h, c_b = _chain8(a, u, c_b, True)
        hb_ref[pl.ds(rb, tc), :] = h
        return c_f, c_b

    zero = jnp.zeros((1, HEAD_DIM), F32)
    lax.fori_loop(0, nc, scan, (zero, zero))

    def fin(c, carry):
        r0 = pl.multiple_of(c * tc, tc)
        y = jax.nn.gelu(ya_ref[pl.ds(r0, tc), :].astype(F32))
        o_ref[pl.ds(r0, tc), :] = ((hf_ref[pl.ds(r0, tc), :] + hb_ref[pl.ds(r0, tc), :]) * y).astype(o_ref.dtype)
        return carry

    lax.fori_loop(0, nc, fin, 0)


def _rglru(pbf, lp, bsz, s):
    t = pbf.shape[0]
    ca, cy = BF_A // HEAD_DIM, (BF_A + MIX_W) // HEAD_DIM
    return pl.pallas_call(
        _rglru_body,
        grid=(bsz, N_HEADS),
        in_specs=[
            pl.BlockSpec((s, HEAD_DIM), lambda b, h: (b, ca + h)),
            pl.BlockSpec((s, HEAD_DIM), lambda b, h: (b, cy + h)),
            pl.BlockSpec((4, HEAD_DIM), lambda b, h: (0, h)),
            pl.BlockSpec((1, HEAD_DIM), lambda b, h: (0, h)),
            pl.BlockSpec((1, HEAD_DIM, 4 * HEAD_DIM), lambda b, h: (h, 0, 0)),
            pl.BlockSpec((1, 1, 4 * HEAD_DIM), lambda b, h: (h, 0, 0)),
            pl.BlockSpec((2, HEAD_DIM), lambda b, h: (0, h)),
        ],
        out_specs=pl.BlockSpec((s, HEAD_DIM), lambda b, h: (b, h)),
        out_shape=jax.ShapeDtypeStruct((t, MIX_W), BF16),
        scratch_shapes=[pltpu.VMEM((s, HEAD_DIM), F32)] * 3,
        compiler_params=_cp(("parallel", "parallel")),
        name="rglru",
    )(pbf, pbf, lp["conv_a_w"], lp["conv_a_b"], lp["rglru_w"], lp["rglru_b"], lp["rglru_lam"])


def _mlstm_body(q_ref, k_ref, v_ref, og_ref, g_ref, ng_ref, o_ref, hf_ref, hb_ref, stf_ref, stb_ref):
    s = q_ref.shape[0]
    cl = min(MLSTM_CHUNK, s)
    nc = s // cl
    ri = lax.broadcasted_iota(jnp.int32, (cl, cl), 0)
    ci = lax.broadcasted_iota(jnp.int32, (cl, cl), 1)
    masks = (ri >= ci, ri <= ci)
    tris = (masks[0].astype(BF16), masks[1].astype(BF16))
    e0 = (lax.broadcasted_iota(jnp.int32, (cl, HEAD_DIM), 1) == 0).astype(BF16)
    kscale = HEAD_DIM ** -0.5

    def load(r0, st_ref):
        return (g_ref[pl.ds(r0, cl), :], q_ref[pl.ds(r0, cl), :], k_ref[pl.ds(r0, cl), :],
                v_ref[pl.ds(r0, cl), :], st_ref[...])

    def chunk(vals, d, m_st):
        g, q, k, v, st = vals
        cum = _cumsum_mm(tris[d], _log_sigmoid(g))
        g_t = g.T
        cum_t = cum.T
        ic, fc = 2 * d, 2 * d + 1
        ig_c, ig_r = g[:, ic:ic + 1], g_t[ic:ic + 1, :]
        b_c, b_r = cum[:, fc:fc + 1], cum_t[fc:fc + 1, :]
        tot = b_c[0:1] if d else b_c[cl - 1:cl]
        k = k.astype(F32) * kscale
        v1 = jnp.concatenate([v, e0], axis=1)
        lw = tot - b_c + ig_c
        m_loc = jnp.max(lw, axis=0, keepdims=True)
        kw = (k * jnp.exp(lw - m_loc)).astype(BF16)
        cn_loc = _dot_tn(kw, v1)
        dm = jnp.where(masks[d], b_c - b_r + ig_r, -jnp.inf)
        m_intra = jnp.max(dm, axis=1, keepdims=True)
        am = jnp.exp(dm - m_intra) * _dot_nt(q, k.astype(BF16))
        nd_intra = _dot(am.astype(BF16), v1)
        m_inter = b_c + m_st
        m_all = jnp.maximum(m_intra, m_inter)
        nd = jnp.exp(m_intra - m_all) * nd_intra + jnp.exp(m_inter - m_all) * _dot(q, st.astype(BF16))
        out = nd[:, :HEAD_DIM] / jnp.maximum(jnp.abs(nd[:, HEAD_DIM:HEAD_DIM + 1]), jnp.exp(-m_all))
        m_new = jnp.maximum(tot + m_st, m_loc)
        st_new = jnp.exp(tot + m_st - m_new) * st + jnp.exp(m_loc - m_new) * cn_loc
        return out, m_new, st_new

    m0 = jnp.full((1, 1), STAB_INIT, F32)
    stf_ref[...] = jnp.zeros_like(stf_ref)
    stb_ref[...] = jnp.zeros_like(stb_ref)

    def body(i, carry):
        m_f, m_b = carry
        rf = pl.multiple_of(i * cl, cl)
        rb = pl.multiple_of((nc - 1 - i) * cl, cl)
        vals_f = load(rf, stf_ref)
        vals_b = load(rb, stb_ref)
        out_f, m_f, st_f = chunk(vals_f, 0, m_f)
        out_b, m_b, st_b = chunk(vals_b, 1, m_b)
        hf_ref[pl.ds(rf, cl), :] = out_f
        hb_ref[pl.ds(rb, cl), :] = out_b
        stf_ref[...] = st_f
        stb_ref[...] = st_b
        return m_f, m_b

    lax.fori_loop(0, nc, body, (m0, m0))
    ng = ng_ref[...]

    def fin(c, carry):
        r0 = pl.multiple_of(c * cl, cl)
        y = _rms(hf_ref[pl.ds(r0, cl), :] + hb_ref[pl.ds(r0, cl), :], ng)
        og = jax.nn.sigmoid(og_ref[pl.ds(r0, cl), :].astype(F32))
        o_ref[pl.ds(r0, cl), :] = (og * y).astype(o_ref.dtype)
        return carry

    lax.fori_loop(0, nc, fin, 0)


def _mlstm(pbf, pfp, lp, bsz, s):
    t = pbf.shape[0]
    cb = BF_B // HEAD_DIM
    cg = FP_GATE // HEAD_DIM
    blk = lambda off: pl.BlockSpec((s, HEAD_DIM), lambda b, h: (b, off + h))
    return pl.pallas_call(
        _mlstm_body,
        grid=(bsz, N_HEADS),
        in_specs=[blk(cb), blk(cb + 4), blk(cb + 8), blk(cb + 12), blk(cg),
                  pl.BlockSpec((1, HEAD_DIM), lambda b, h: (0, h))],
        out_specs=pl.BlockSpec((s, HEAD_DIM), lambda b, h: (b, h)),
        out_shape=jax.ShapeDtypeStruct((t, MIX_W), BF16),
        scratch_shapes=[pltpu.VMEM((s, HEAD_DIM), F32)] * 2 + [pltpu.VMEM((HEAD_DIM, 2 * HEAD_DIM), F32)] * 2,
        compiler_params=_cp(("parallel", "parallel")),
        name="mlstm",
    )(pbf, pbf, pbf, pbf, pfp, lp["mlstm_norm"])


def _block_ref(b, m, rev):
    c, w = b.shape
    parts = []
    for blk in range(c // (2 * m)):
        row = blk * 2 * m + (m if rev else m - 1)
        parts.append(jnp.broadcast_to(b[row:row + 1, :], (2 * m, w)))
    return parts[0] if len(parts) == 1 else jnp.concatenate(parts, axis=0)


def _hgrn_body(layer, q_ref, ff_ref, fb_ref, i_ref, g_ref, lbl_ref, ng_ref, o_ref, of_ref, ob_ref, stf_ref,
               stb_ref):
    s = q_ref.shape[0]
    c = min(HGRN_CHUNK, s)
    nc = s // c
    band = min(HGRN_BAND, c)
    if layer > 0:
        lg = lbl_ref[...]
        e = jnp.exp(lg - jnp.max(lg, axis=0, keepdims=True))
        p = e / jnp.sum(e, axis=0, keepdims=True)
        lb = p[1:2, :]
        for r in range(2, layer + 1):
            lb = lb + p[r:r + 1, :]
        log_lb = jnp.log(lb)
        log_1m = jnp.log(1.0 - lb)
        one_m = 1.0 - lb

    ri = lax.broadcasted_iota(jnp.int32, (c, c), 0)
    ci = lax.broadcasted_iota(jnp.int32, (c, c), 1)
    tris = ((ri >= ci).astype(BF16), (ri <= ci).astype(BF16))
    rw = lax.broadcasted_iota(jnp.int32, (c, HEAD_DIM), 0)
    pos = lax.broadcasted_iota(jnp.int32, (c, 1), 0) & (band - 1)
    levels = []
    m = c // 2
    while m >= band:
        levels.append(m)
        m //= 2
    upper = {m: (rw & (2 * m - 1)) >= m for m in levels}
    same_blk = {m: (ri >> int(math.log2(2 * m))) == (ci >> int(math.log2(2 * m))) for m in levels if 2 * m < c}
    diags = ([ri - ci == dl for dl in range(band)], [ci - ri == dl for dl in range(band)])

    def chunk(r0, rev, st_ref):
        f = (fb_ref if rev else ff_ref)[pl.ds(r0, c), :]
        e_f = jnp.exp(-jnp.abs(f))
        den = 1.0 + e_f
        lf = jnp.minimum(f, 0.0) - jnp.log(den)
        kd = jnp.where(f >= 0.0, e_f, 1.0) / den
        if layer > 0:
            x = log_1m + lf
            mx = jnp.maximum(log_lb, x)
            lf = mx + jnp.log(jnp.exp(log_lb - mx) + jnp.exp(x - mx))
            kd = one_m * kd
        b = _cumsum_mm(tris[rev], lf)
        tot = b[0:1] if rev else b[c - 1:c]
        qs = jax.nn.silu(q_ref[pl.ds(r0, c), :].astype(F32))
        v = i_ref[pl.ds(r0, c), :]
        st = st_ref[...]
        o = _dot_nt((qs * jnp.exp(b)).astype(BF16), st.astype(BF16))
        k_out = (kd * jnp.exp(tot - b)).astype(BF16)
        st_ref[...] = jnp.exp(tot) * st + _dot_tn(v, k_out)
        amat = None
        for m in levels:
            bref = _block_ref(b, m, rev)
            up = upper[m]
            qm = jnp.logical_not(up) if rev else up
            km = up if rev else jnp.logical_not(up)
            qt = jnp.where(qm, qs * jnp.exp(jnp.where(qm, b - bref, 0.0)), 0.0).astype(BF16)
            kt = jnp.where(km, kd * jnp.exp(jnp.where(km, bref - b, 0.0)), 0.0).astype(BF16)
            am = _dot_nt(qt, kt)
            if m in same_blk:
                am = jnp.where(same_blk[m], am, 0.0)
            amat = am if amat is None else amat + am
        for dl in range(band):
            if dl == 0:
                a = jnp.sum(qs * kd, axis=1, keepdims=True)
            else:
                sh = (c - dl) if rev else dl
                valid = (pos + dl < band) if rev else (pos >= dl)
                ex = jnp.exp(b - pltpu.roll(b, sh, 0))
                a = jnp.where(valid, jnp.sum(qs * pltpu.roll(kd, sh, 0) * ex, axis=1, keepdims=True), 0.0)
            term = jnp.where(diags[rev][dl], a, 0.0)
            amat = term if amat is None else amat + term
        return o + _dot(amat.astype(BF16), v)

    stf_ref[...] = jnp.zeros_like(stf_ref)
    stb_ref[...] = jnp.zeros_like(stb_ref)

    def body(i, carry):
        rf = pl.multiple_of(i * c, c)
        rb = pl.multiple_of((nc - 1 - i) * c, c)
        of_ref[pl.ds(rf, c), :] = chunk(rf, 0, stf_ref)
        ob_ref[pl.ds(rb, c), :] = chunk(rb, 1, stb_ref)
        return carry

    lax.fori_loop(0, nc, body, 0)
    ng = ng_ref[...]

    def fin(ci_, carry):
        r0 = pl.multiple_of(ci_ * c, c)
        hs = of_ref[pl.ds(r0, c), :] + ob_ref[pl.ds(r0, c), :]
        gate = jax.nn.sigmoid(g_ref[pl.ds(r0, c), :].astype(F32))
        o_ref[pl.ds(r0, c), :] = (gate * _rms(hs, ng)).astype(o_ref.dtype)
        return carry

    lax.fori_loop(0, nc, fin, 0)


def _hgrn(pbf, pfp, lp, lb_logits, layer, bsz, s):
    t = pbf.shape[0]
    cd = BF_D3 // HEAD_DIM
    blk = lambda off: pl.BlockSpec((s, HEAD_DIM), lambda b, h: (b, off + h))
    depth = lb_logits.shape[0]
    return pl.pallas_call(
        functools.partial(_hgrn_body, layer),
        grid=(bsz, N_HEADS),
        in_specs=[blk(cd), blk(FP_FF // HEAD_DIM), blk(FP_FB // HEAD_DIM), blk(cd + 4), blk(cd + 8),
                  pl.BlockSpec((depth, HEAD_DIM), lambda b, h: (0, h)),
                  pl.BlockSpec((1, HEAD_DIM), lambda b, h: (0, h))],
        out_specs=pl.BlockSpec((s, HEAD_DIM), lambda b, h: (b, h)),
        out_shape=jax.ShapeDtypeStruct((t, MIX_W), BF16),
        scratch_shapes=[pltpu.VMEM((s, HEAD_DIM), F32)] * 2 + [pltpu.VMEM((HEAD_DIM, HEAD_DIM), F32)] * 2,
        compiler_params=_cp(("parallel", "parallel")),
        name="hgrn2",
    )(pbf, pfp, pfp, pbf, pbf, lb_logits, lp["hgrn_norm"])


def _shift3(cat, w, bias, ts):
    n = ts + 2 * HALO
    return (bias + w[0:1] * pltpu.roll(cat, 1, 0)[HALO:HALO + ts] + w[1:2] * cat[HALO:HALO + ts]
            + w[2:3] * pltpu.roll(cat, n - 1, 0)[HALO:HALO + ts])


def _hyena_prep_body(x_ref, xp_ref, xn_ref, cw_ref, cb_ref, z_ref, x1_ref):
    i = pl.program_id(1)
    ts = x_ref.shape[0]
    prev = jnp.where(i > 0, xp_ref[...].astype(F32), 0.0)
    nxt = jnp.where(i < pl.num_programs(1) - 1, xn_ref[...].astype(F32), 0.0)
    cat = jnp.concatenate([prev, x_ref[...].astype(F32), nxt], axis=0)
    u = _shift3(cat, cw_ref[...], cb_ref[...], ts)
    z_ref[...] = (u[:, 2 * MIX_W:] * u[:, :MIX_W]).astype(z_ref.dtype)
    x1_ref[...] = u[:, MIX_W:2 * MIX_W].astype(x1_ref.dtype)


def _hyena_prep(pbf, lp, bsz, s):
    t = pbf.shape[0]
    ts = min(512, s)
    per = s // ts
    hb = ts // HALO
    nb = t // HALO
    w = 3 * MIX_W
    return pl.pallas_call(
        _hyena_prep_body,
        grid=(bsz, per),
        in_specs=[
            pl.BlockSpec((ts, w), lambda b, i: (b * per + i, 0)),
            pl.BlockSpec((HALO, w), lambda b, i: (jnp.maximum((b * per + i) * hb - 1, 0), 0)),
            pl.BlockSpec((HALO, w), lambda b, i: (jnp.minimum((b * per + i + 1) * hb, nb - 1), 0)),
            pl.BlockSpec((3, w), lambda b, i: (0, 0)),
            pl.BlockSpec((1, w), lambda b, i: (0, 0)),
        ],
        out_specs=[pl.BlockSpec((ts, MIX_W), lambda b, i: (b * per + i, 0)),
                   pl.BlockSpec((ts, MIX_W), lambda b, i: (b * per + i, 0))],
        out_shape=[jax.ShapeDtypeStruct((t, MIX_W), BF16), jax.ShapeDtypeStruct((t, MIX_W), BF16)],
        compiler_params=_cp(("parallel", "arbitrary")),
        name="hyena_prep",
    )(pbf, pbf, pbf, lp["hyena_conv_w"], lp["hyena_conv_b"])


def _dft1_body(prec, x_ref, f_ref, o_ref):
    n1h, r, c = x_ref.shape
    x = x_ref[...].reshape(n1h * r, c)
    o = jnp.dot(f_ref[...], x, preferred_element_type=F32, precision=prec)
    o_ref[...] = o.reshape(o_ref.shape).astype(o_ref.dtype)


def _dft1(x4, kron1, out_dtype, prec=None):
    bsz, n1h, n2, c = x4.shape
    r = kron1.shape[1] // n1h
    n1 = kron1.shape[0] // (2 * r)
    return pl.pallas_call(
        functools.partial(_dft1_body, prec),
        grid=(bsz, n2 // r),
        in_specs=[pl.BlockSpec((None, n1h, r, c), lambda b, j: (b, 0, j, 0)),
                  pl.BlockSpec(kron1.shape, lambda b, j: (0, 0))],
        out_specs=pl.BlockSpec((None, 2, n1, r, c), lambda b, j: (b, 0, 0, j, 0)),
        out_shape=jax.ShapeDtypeStruct((bsz, 2, n1, n2, c), out_dtype),
        compiler_params=_cp(("parallel", "parallel")),
        name="dft_stage1",
    )(x4, kron1)


def _dft2_conv_body(kb, a_ref, f_ref, fi_ref, g_ref, o_ref):
    n2 = a_ref.shape[2]
    for kk in range(kb):
        a = jnp.concatenate([a_ref[0, kk], a_ref[1, kk]], axis=0)
        x = _dot(f_ref[kk], a)
        xr, xi = x[:n2], x[n2:]
        gr, gi = g_ref[kk, 0], g_ref[kk, 1]
        y = jnp.concatenate([xr * gr - xi * gi, xr * gi + xi * gr], axis=0).astype(BF16)
        bq = _dot(fi_ref[kk], y)
        o_ref[0, kk] = bq[:n2].astype(o_ref.dtype)
        o_ref[1, kk] = bq[n2:].astype(o_ref.dtype)


def _dft2_conv(a5, f2, f2i, gspec, kb):
    bsz, _, n1, n2, c = a5.shape
    return pl.pallas_call(
        functools.partial(_dft2_conv_body, kb),
        grid=(n1 // kb, bsz),
        in_specs=[pl.BlockSpec((None, 2, kb, n2, c), lambda k, b: (b, 0, k, 0, 0)),
                  pl.BlockSpec((kb, 2 * n2, 2 * n2), lambda k, b: (k, 0, 0)),
                  pl.BlockSpec((kb, 2 * n2, 2 * n2), lambda k, b: (k, 0, 0)),
                  pl.BlockSpec((kb, 2, n2, c), lambda k, b: (k, 0, 0, 0))],
        out_specs=pl.BlockSpec((None, 2, kb, n2, c), lambda k, b: (b, 0, k, 0, 0)),
        out_shape=jax.ShapeDtypeStruct(a5.shape, BF16),
        compiler_params=_cp(("parallel", "arbitrary")),
        name="dft_stage2_conv",
    )(a5, f2, f2i, gspec)


def _dft2_filter_body(kb, a_ref, f_ref, o_ref):
    n2 = a_ref.shape[3]
    k0 = pl.program_id(0) * kb
    for kk in range(kb):
        sgn = (1 - 2 * ((k0 + kk) & 1)).astype(F32)
        xs = []
        for e in range(2):
            a = jnp.concatenate([a_ref[e, 0, kk], a_ref[e, 1, kk]], axis=0)
            xs.append(jnp.dot(f_ref[kk], a, preferred_element_type=F32, precision=HIGHEST))
        x = xs[0] + sgn * xs[1]
        o_ref[kk, 0] = x[:n2]
        o_ref[kk, 1] = x[n2:]


def _dft2_filter(a5, f2, kb):
    _, _, n1, n2, c = a5.shape
    return pl.pallas_call(
        functools.partial(_dft2_filter_body, kb),
        grid=(n1 // kb,),
        in_specs=[pl.BlockSpec((2, 2, kb, n2, c), lambda k: (0, 0, k, 0, 0)),
                  pl.BlockSpec((kb, 2 * n2, 2 * n2), lambda k: (k, 0, 0))],
        out_specs=pl.BlockSpec((kb, 2, n2, c), lambda k: (k, 0, 0, 0)),
        out_shape=jax.ShapeDtypeStruct((n1, 2, n2, c), F32),
        compiler_params=_cp(("parallel",)),
        name="dft_stage2_filter",
    )(a5, f2)


def _dft3_body(b_ref, f_ref, x1_ref, z_ref, sk_ref, o_ref):
    _, n1, r, c = b_ref.shape
    y = _dot(f_ref[...], b_ref[...].reshape(2 * n1 * r, c)).reshape(o_ref.shape)
    z = z_ref[...].astype(F32)
    o_ref[...] = (x1_ref[...].astype(F32) * (y + sk_ref[...] * z)).astype(o_ref.dtype)


def _dft3(b5, kron3, x1_4, z4, skip):
    bsz, _, n1, n2, c = b5.shape
    n1h = n1 // 2
    r = kron3.shape[0] // n1h
    row = pl.BlockSpec((None, n1h, r, c), lambda b, j: (b, 0, j, 0))
    return pl.pallas_call(
        _dft3_body,
        grid=(bsz, n2 // r),
        in_specs=[pl.BlockSpec((None, 2, n1, r, c), lambda b, j: (b, 0, 0, j, 0)),
                  pl.BlockSpec(kron3.shape, lambda b, j: (0, 0)),
                  row, row,
                  pl.BlockSpec((1, c), lambda b, j: (0, 0))],
        out_specs=row,
        out_shape=jax.ShapeDtypeStruct((bsz, n1h, n2, c), BF16),
        compiler_params=_cp(("parallel", "parallel")),
        name="dft_stage3",
    )(b5, kron3, x1_4, z4, skip)


def _hyena_filter_body(seq_len, pos_ref, sc_ref, bands_ref, w1t_ref, w1c_ref, w1s_ref, b1_ref, fr1_ref,
                       w2_ref, b2_ref, fr2_ref, w3_ref, rate_ref, o_ref):
    pos = pos_ref[...]
    t = pos * (1.0 / (seq_len - 1))
    arg = (pos * (2.0 * math.pi / seq_len)) * bands_ref[...]
    hdot = lambda a, b: jnp.dot(a, b, preferred_element_type=F32, precision=HIGHEST)
    pre = t * w1t_ref[...] + hdot(jnp.cos(arg), w1c_ref[...]) - hdot(jnp.sin(arg), w1s_ref[...]) + b1_ref[...]
    hid = jnp.sin(fr1_ref[...] * pre)
    hid = jnp.sin(fr2_ref[...] * (hdot(hid, w2_ref[...]) + b2_ref[...]))
    o_ref[...] = hdot(hid, w3_ref[...]) * jnp.exp(-t * rate_ref[...]) * sc_ref[...]


def _hyena_filter(lp, s):
    ts = min(256, s)
    r = 2 * s + ts
    pos = np.concatenate([np.arange(s), s - np.arange(s), np.zeros(ts)]).astype(np.float32)[:, None]
    sc = np.ones((r, 1), np.float32)
    sc[s] = 0.0
    pad = HEAD_DIM - HYENA_HID
    bands = np.zeros((1, HEAD_DIM), np.float32)
    bands[0, :HYENA_BANDS] = np.linspace(1e-4, HYENA_BANDS - 1, HYENA_BANDS)
    rate = np.abs(np.linspace(math.log(HYENA_TARGET) / HYENA_FAST, math.log(HYENA_TARGET) / HYENA_SLOW, MIX_W))
    rate = np.tile(rate, 2).astype(np.float32)[None, :]
    w1 = lp["hyena_w1"]
    padc = lambda a: jnp.pad(a, ((0, 0), (0, pad)))
    w1t = padc(w1[0:1])
    w1c = jnp.pad(w1[1:1 + HYENA_BANDS], ((0, HEAD_DIM - HYENA_BANDS), (0, pad)))
    w1s = jnp.pad(w1[1 + HYENA_BANDS:], ((0, HEAD_DIM - HYENA_BANDS), (0, pad)))
    w2 = jnp.pad(lp["hyena_w2"], ((0, pad), (0, pad)))
    w3 = jnp.pad(lp["hyena_w3"], ((0, pad), (0, 0)))
    vec = lambda a: padc(a[None, :])
    full = lambda a: pl.BlockSpec(a.shape, lambda i: (0, 0))
    args = [jnp.asarray(bands), w1t, w1c, w1s, vec(lp["hyena_b1"]), vec(lp["hyena_freq1"]), w2,
            vec(lp["hyena_b2"]), vec(lp["hyena_freq2"]), w3, jnp.asarray(rate)]
    out = pl.pallas_call(
        functools.partial(_hyena_filter_body, s),
        grid=(r // ts,),
        in_specs=[pl.BlockSpec((ts, 1), lambda i: (i, 0)), pl.BlockSpec((ts, 1), lambda i: (i, 0))]
                 + [full(a) for a in args],
        out_specs=pl.BlockSpec((ts, 2 * MIX_W), lambda i: (i, 0)),
        out_shape=jax.ShapeDtypeStruct((r, 2 * MIX_W), F32),
        compiler_params=_cp(("parallel",)),
        name="hyena_filter",
    )(jnp.asarray(pos), jnp.asarray(sc), *args)
    return out[:s, :MIX_W], out[s:2 * s, MIX_W:], out[2 * s:2 * s + 1, MIX_W:]


def _dft_tables(s):
    n = 2 * s
    n2 = min(DFT_N2, s // 8)
    n1 = n // n2
    n1h = n1 // 2
    two_pi = 2.0 * math.pi
    k1 = jnp.arange(n1, dtype=jnp.int32)
    a1 = (two_pi / n1) * ((k1[:, None] * k1[None, :n1h]) % n1).astype(F32)
    f1 = jnp.concatenate([jnp.cos(a1), -jnp.sin(a1)], axis=0)
    f1i = jnp.concatenate([jnp.cos(a1).T, -jnp.sin(a1).T], axis=1) * (1.0 / n)
    j = jnp.arange(n2, dtype=jnp.int32)
    ph = (j[None, :, None] * j[None, None, :] * n1 + j[None, None, :] * k1[:, None, None]) % n
    a2 = (two_pi / n) * ph.astype(F32)
    tr, ti = jnp.cos(a2), -jnp.sin(a2)
    f2 = jnp.concatenate([jnp.concatenate([tr, -ti], axis=2), jnp.concatenate([ti, tr], axis=2)], axis=1)
    kron = lambda m, r: jnp.kron(m, jnp.eye(r, dtype=F32))
    r32, r16 = min(SUBLANES, n2), min(2 * SUBLANES, n2)
    return dict(n1=n1, n2=n2, f2=f2, f2_bf=f2.astype(BF16), f2i_bf=jnp.swapaxes(f2, 1, 2).astype(BF16),
                kron1_f32=kron(f1, r32), kron1_bf=kron(f1, r16).astype(BF16), kron3_bf=kron(f1i, r16).astype(BF16))


def _hyena_spectrum(lp, s, tab):
    gpos, gneg, hb0 = _hyena_filter(lp, s)
    n1, n2 = tab["n1"], tab["n2"]
    a = _dft1(jnp.stack([gpos, gneg]).reshape(2, n1 // 2, n2, MIX_W), tab["kron1_f32"], F32, HIGHEST)
    return _dft2_filter(a, tab["f2"], min(2, n1)), hb0


def _hyena(pbf, lp, gspec, hb0, tab, bsz, s):
    n1, n2 = tab["n1"], tab["n2"]
    z, x1 = _hyena_prep(pbf, lp, bsz, s)
    z4 = z.reshape(bsz, n1 // 2, n2, MIX_W)
    a = _dft1(z4, tab["kron1_bf"], BF16)
    bq = _dft2_conv(a, tab["f2_bf"], tab["f2i_bf"], gspec, min(2, n1))
    out = _dft3(bq, tab["kron3_bf"], x1.reshape(z4.shape), z4, lp["hyena_skip"][None, :] + hb0)
    return out.reshape(bsz * s, MIX_W)


def _merge_body(h_ref, g0_ref, g1_ref, g2_ref, g3_ref, ba_ref, bb_ref, bc_ref, bd_ref, wb_ref, wo_ref, o_ref):
    merged = None
    for idx, (g_ref, br) in enumerate(zip((g0_ref, g1_ref, g2_ref, g3_ref), (ba_ref, bb_ref, bc_ref, bd_ref))):
        term = jax.nn.sigmoid(g_ref[...].astype(F32)) * _dot(br[...], wb_ref[idx])
        merged = term if merged is None else merged + term
    o_ref[...] = h_ref[...] + _dot(merged.astype(BF16), wo_ref[...])


def _merge(h, pbf, branches, lp):
    t, d = h.shape
    tm = min(512, t)
    gcol = BF_G // d
    row = lambda w: pl.BlockSpec((tm, w), lambda i: (i, 0))
    gate = lambda idx: pl.BlockSpec((tm, d), lambda i: (i, gcol + idx))
    return pl.pallas_call(
        _merge_body,
        grid=(t // tm,),
        in_specs=[row(d), gate(0), gate(1), gate(2), gate(3),
                  row(MIX_W), row(MIX_W), row(MIX_W), row(MIX_W),
                  pl.BlockSpec(lp["w_branch"].shape, lambda i: (0, 0, 0)),
                  pl.BlockSpec(lp["w_out"].shape, lambda i: (0, 0))],
        out_specs=row(d),
        out_shape=jax.ShapeDtypeStruct((t, d), F32),
        compiler_params=_cp(("parallel",)),
        name="merge",
    )(h, pbf, pbf, pbf, pbf, *branches, lp["w_branch"], lp["w_out"])


def _ffn_body(per, final, h_ref, hp_ref, hn_ref, p_ref, gf_ref, wu_ref, cw_ref, cb_ref, wd_ref, gp_ref, wg_ref,
              wp_ref, gl_ref, o_ref):
    i = pl.program_id(0)
    tm = h_ref.shape[0]
    dff = wd_ref.shape[0]
    gf = gf_ref[...]
    h = h_ref[...]
    first = (i % per) == 0
    last = (i % per) == per - 1
    xp = jnp.where(first, 0.0, _rms(hp_ref[...], gf))
    xn = jnp.where(last, 0.0, _rms(hn_ref[...], gf))
    cat = jnp.concatenate([xp, _rms(h, gf), xn], axis=0).astype(BF16)
    cb_w = min(512, dff)
    acc = jnp.zeros_like(h)
    for cb in range(dff // cb_w):
        lo, lo2 = cb * cb_w, dff + cb * cb_w
        u1 = _shift3(_dot(cat, wu_ref[:, lo:lo + cb_w]), cw_ref[:, lo:lo + cb_w], cb_ref[:, lo:lo + cb_w], tm)
        u2 = _shift3(_dot(cat, wu_ref[:, lo2:lo2 + cb_w]), cw_ref[:, lo2:lo2 + cb_w], cb_ref[:, lo2:lo2 + cb_w], tm)
        acc = acc + _dot((jax.nn.gelu(u1) * u2).astype(BF16), wd_ref[lo:lo + cb_w, :])
    h = h + acc
    gate = jax.nn.sigmoid(_dot(_rms(h, gp_ref[...]).astype(BF16), wg_ref[...]))
    h = h + gate * _dot(p_ref[...].astype(BF16), wp_ref[...])
    if final:
        h = _rms(h, gl_ref[...])
    o_ref[...] = h


def _ffn(h, p, lp, final_norm, s, final):
    t, d = h.shape
    tm = min(512, s)
    per = s // tm
    hb = tm // HALO
    nb = t // HALO
    full = lambda a: pl.BlockSpec(a.shape, lambda i: (0,) * a.ndim)
    row = lambda w: pl.BlockSpec((tm, w), lambda i: (i, 0))
    ws = [lp["norm_ffn"], lp["w_up"], lp["ffn_conv_w"], lp["ffn_conv_b"], lp["w_down"], lp["norm_ple"],
          lp["w_ple_gate"], lp["w_ple"], final_norm]
    return pl.pallas_call(
        functools.partial(_ffn_body, per, final),
        grid=(t // tm,),
        in_specs=[row(d),
                  pl.BlockSpec((HALO, d), lambda i: (jnp.maximum(i * hb - 1, 0), 0)),
                  pl.BlockSpec((HALO, d), lambda i: (jnp.minimum((i + 1) * hb, nb - 1), 0)),
                  row(p.shape[1])] + [full(a) for a in ws],
        out_specs=row(d),
        out_shape=jax.ShapeDtypeStruct((t, d), F32),
        compiler_params=_cp(("parallel",)),
        name="ffn_ple",
    )(h, h, h, p, *ws)


def _prepare_params(prm):
    d_model = prm["w_in"].shape[1]
    depth = prm["w_in"].shape[0]
    off_b = 2 * MIX_W
    off_gate = off_b + 4 * MIX_W
    off_c = off_gate + 4 * N_HEADS
    off_d = off_c + 3 * MIX_W
    off_g = off_d + 5 * MIX_W

    def regroup(a):
        sl = lambda lo, w: a[..., lo:lo + w]
        bf = jnp.concatenate([sl(off_c, 3 * MIX_W), sl(off_d, MIX_W), sl(off_d + 3 * MIX_W, 2 * MIX_W),
                              sl(off_g, 4 * d_model), sl(0, 2 * MIX_W), sl(off_b, 4 * MIX_W)], axis=-1)
        gates = sl(off_gate, 4 * N_HEADS).reshape(a.shape[:-1] + (4, N_HEADS))
        gates = jnp.swapaxes(gates, -1, -2)
        gates = jnp.pad(gates, [(0, 0)] * (gates.ndim - 1) + [(0, HEAD_DIM - 4)])
        fp = jnp.concatenate([sl(off_d + MIX_W, 2 * MIX_W), gates.reshape(a.shape[:-1] + (N_HEADS * HEAD_DIM,))],
                             axis=-1)
        return bf, fp

    w_bf, w_fp = regroup(prm["w_in"])
    b_bf, b_fp = regroup(prm["b_in"][:, None, :])
    gw = jnp.transpose(prm["rglru_w"], (0, 3, 4, 1, 2, 5)).reshape(depth, N_HEADS, HEAD_DIM, 4 * HEAD_DIM)
    gb = prm["rglru_b"].reshape(depth, 2, 2, N_HEADS, HEAD_DIM)
    gb = jnp.transpose(gb, (0, 3, 1, 2, 4)).reshape(depth, N_HEADS, 1, 4 * HEAD_DIM)
    row = lambda a: a[:, None, :]
    out = dict(
        norm_mix=row(prm["norm_mix"]), w_bf=w_bf.astype(BF16), b_bf=b_bf, w_fp=w_fp.astype(BF16), b_fp=b_fp,
        conv_a_w=prm["conv_a_w"], conv_a_b=row(prm["conv_a_b"]), rglru_w=gw.astype(BF16), rglru_b=gb,
        rglru_lam=prm["rglru_lam"], mlstm_norm=row(prm["mlstm_norm"]),
        hyena_conv_w=prm["hyena_conv_w"], hyena_conv_b=row(prm["hyena_conv_b"]),
        hyena_w1=prm["hyena_w1"], hyena_b1=prm["hyena_b1"], hyena_freq1=prm["hyena_freq1"],
        hyena_w2=prm["hyena_w2"], hyena_b2=prm["hyena_b2"], hyena_freq2=prm["hyena_freq2"],
        hyena_w3=prm["hyena_w3"], hyena_skip=prm["hyena_skip"], hgrn_norm=row(prm["hgrn_norm"]),
        w_branch=prm["w_branch"].astype(BF16), w_out=prm["w_out"].astype(BF16),
        norm_ffn=row(prm["norm_ffn"]), w_up=prm["w_up"].astype(BF16), ffn_conv_w=prm["ffn_conv_w"],
        ffn_conv_b=row(prm["ffn_conv_b"]), w_down=prm["w_down"].astype(BF16), norm_ple=row(prm["norm_ple"]),
        w_ple_gate=prm["w_ple_gate"].astype(BF16), w_ple=prm["w_ple"].astype(BF16),
    )
    return out


def _trunk(x, p, prm, lb_logits, final_norm):
    bsz, s, d = x.shape
    depth = p.shape[0]
    h = x.reshape(bsz * s, d)
    tab = _dft_tables(s)
    for i in range(depth):
        lp = {k: v[i] for k, v in prm.items()}
        pbf = _proj(h, lp["norm_mix"], lp["w_bf"], lp["b_bf"], BF16, 1280)
        pfp = _proj(h, lp["norm_mix"], lp["w_fp"], lp["b_fp"], F32, 512)
        gspec, hb0 = _hyena_spectrum(lp, s, tab)
        branches = (_rglru(pbf, lp, bsz, s), _mlstm(pbf, pfp, lp, bsz, s),
                    _hyena(pbf, lp, gspec, hb0, tab, bsz, s), _hgrn(pbf, pfp, lp, lb_logits, i, bsz, s))
        h = _merge(h, pbf, branches, lp)
        h = _ffn(h, p[i].reshape(bsz * s, -1), lp, final_norm, s, i == depth - 1)
    return h.reshape(bsz, s, d)


def kernel(x_prompt, x_sample, p_prompt, p_sample, norm_mix, w_in, b_in, conv_a_w, conv_a_b, rglru_w, rglru_b, rglru_lam, mlstm_norm, hyena_conv_w, hyena_conv_b, hyena_w1, hyena_b1, hyena_freq1, hyena_w2, hyena_b2, hyena_freq2, hyena_w3, hyena_skip, hgrn_lb_logits, hgrn_norm, w_branch, w_out, norm_ffn, w_up, ffn_conv_w, ffn_conv_b, w_down, norm_ple, w_ple_gate, w_ple, final_norm):
    prm = _prepare_params(dict(
        norm_mix=norm_mix, w_in=w_in, b_in=b_in, conv_a_w=conv_a_w, conv_a_b=conv_a_b, rglru_w=rglru_w,
        rglru_b=rglru_b, rglru_lam=rglru_lam, mlstm_norm=mlstm_norm, hyena_conv_w=hyena_conv_w,
        hyena_conv_b=hyena_conv_b, hyena_w1=hyena_w1, hyena_b1=hyena_b1, hyena_freq1=hyena_freq1,
        hyena_w2=hyena_w2, hyena_b2=hyena_b2, hyena_freq2=hyena_freq2, hyena_w3=hyena_w3,
        hyena_skip=hyena_skip, hgrn_norm=hgrn_norm, w_branch=w_branch, w_out=w_out, norm_ffn=norm_ffn,
        w_up=w_up, ffn_conv_w=ffn_conv_w, ffn_conv_b=ffn_conv_b, w_down=w_down, norm_ple=norm_ple,
        w_ple_gate=w_ple_gate, w_ple=w_ple))
    fn = final_norm[None, :]
    y_prompt = _trunk(x_prompt, p_prompt, prm, hgrn_lb_logits, fn)
    y_sample = _trunk(x_sample, p_sample, prm, hgrn_lb_logits, fn)
    return (y_prompt, y_sample)
```

```python
import functools
import math

import numpy as np
import jax
import jax.numpy as jnp
from jax import lax
from jax.experimental import pallas as pl
from jax.experimental.pallas import tpu as pltpu

F32 = jnp.float32
BF16 = jnp.bfloat16
HIGHEST = lax.Precision.HIGHEST

N_HEADS = 4
HEAD_DIM = 128
MIX_W = N_HEADS * HEAD_DIM
LRU_C = 8.0
EPS = 1e-6
STAB_INIT = -1e30
HYENA_BANDS = 16
HYENA_HID = 64
HYENA_FAST = 0.3
HYENA_SLOW = 1.5
HYENA_TARGET = 1e-2

MLSTM_CHUNK = 128
HGRN_CHUNK = 128
HGRN_BAND = 4
SCAN_CHUNK = 256
HALO = 16
SUBLANES = 8
PROJ_COLS = 1280
DFT_N2 = 128
V7X_VMEM_LIMIT = 56 * 1024 * 1024

BF_C, BF_D3, BF_G, BF_A, BF_B = 0, 1536, 3072, 7168, 8192
BF_COLS = 10240
FP_FF, FP_FB, FP_GATE = 0, 512, 1024
FP_COLS = 1536


def _cp(sem, vmem=V7X_VMEM_LIMIT):
    return pltpu.CompilerParams(dimension_semantics=sem, vmem_limit_bytes=vmem)


def _rms(x, g):
    return x * lax.rsqrt(jnp.mean(x * x, axis=-1, keepdims=True) + EPS) * g


def _log_sigmoid(x):
    return jnp.minimum(x, 0.0) - jnp.log(1.0 + jnp.exp(-jnp.abs(x)))


def _dot(a, b):
    return jnp.dot(a, b, preferred_element_type=F32)


def _dot_nt(a, b):
    return lax.dot_general(a, b, (((1,), (1,)), ((), ())), preferred_element_type=F32)


def _dot_tn(a, b):
    return lax.dot_general(a, b, (((0,), (0,)), ((), ())), preferred_element_type=F32)


def _cumsum_mm(tri, x):
    hi = x.astype(BF16)
    r = x - hi.astype(F32)
    mid = r.astype(BF16)
    lo = (r - mid.astype(F32)).astype(BF16)
    return _dot(tri, hi) + _dot(tri, mid) + _dot(tri, lo)


def _proj_body(h_ref, g_ref, wb_ref, bb_ref, wf_ref, bf_ref, ob_ref, of_ref):
    xn = _rms(h_ref[...], g_ref[...]).astype(BF16)
    for w_ref, b_ref, o_ref in ((wb_ref, bb_ref, ob_ref), (wf_ref, bf_ref, of_ref)):
        n = w_ref.shape[1]
        tn = math.gcd(PROJ_COLS, n)
        for lo in range(0, n, tn):
            o_ref[:, lo:lo + tn] = (_dot(xn, w_ref[:, lo:lo + tn]) + b_ref[:, lo:lo + tn]).astype(o_ref.dtype)


def _proj(h, g, w_bf, b_bf, w_fp, b_fp):
    t, d = h.shape
    tm = min(256, t)
    once = lambda a: pl.BlockSpec(a.shape, lambda i: (0, 0), pipeline_mode=pl.Buffered(1))
    row = lambda n: pl.BlockSpec((tm, n), lambda i: (i, 0))
    return pl.pallas_call(
        _proj_body,
        grid=(t // tm,),
        in_specs=[row(d), once(g), once(w_bf), once(b_bf), once(w_fp), once(b_fp)],
        out_specs=[row(w_bf.shape[1]), row(w_fp.shape[1])],
        out_shape=[jax.ShapeDtypeStruct((t, w_bf.shape[1]), BF16), jax.ShapeDtypeStruct((t, w_fp.shape[1]), F32)],
        compiler_params=_cp(("parallel",)),
        name="in_proj",
    )(h, g, w_bf, b_bf, w_fp, b_fp)


def _scan8(a, u, pos, reverse):
    n = a.shape[0]
    d = 1
    while d < SUBLANES:
        if reverse:
            m = pos < SUBLANES - d
            sh = n - d
        else:
            m = pos >= d
            sh = d
        a_s = jnp.where(m, pltpu.roll(a, sh, 0), 1.0)
        u_s = jnp.where(m, pltpu.roll(u, sh, 0), 0.0)
        u = a * u_s + u
        a = a * a_s
        d *= 2
    return a, u


def _chain8(a, u, carry, reverse):
    groups = a.shape[0] // SUBLANES
    outs = [None] * groups
    for gi in (range(groups - 1, -1, -1) if reverse else range(groups)):
        lo = gi * SUBLANES
        h = u[lo:lo + SUBLANES] + a[lo:lo + SUBLANES] * carry
        outs[gi] = h
        carry = h[0:1] if reverse else h[SUBLANES - 1:SUBLANES]
    return jnp.concatenate(outs, axis=0), carry


def _rglru_body(xa_ref, ya_ref, cw_ref, cb_ref, gw_ref, gb_ref, lam_ref, o_ref, xc_ref, hf_ref, hb_ref):
    s = xa_ref.shape[0]
    tc = min(SCAN_CHUNK, s)
    nc = s // tc
    n = tc + 2 * HALO
    pos = lax.broadcasted_iota(jnp.int32, (tc, HEAD_DIM), 0) & (SUBLANES - 1)
    cw = cw_ref[...]
    cb = cb_ref[...]
    sp = jax.nn.softplus(-lam_ref[...])

    def gates(xc, d):
        w = gw_ref[0, :, d * 256:(d + 1) * 256]
        g = jax.nn.sigmoid(_dot(xc.astype(BF16), w) + gb_ref[0, :, d * 256:(d + 1) * 256])
        log_a = (-LRU_C) * g[:, :HEAD_DIM] * sp[d:d + 1]
        a = jnp.exp(log_a)
        x2 = 2.0 * log_a
        one_m_a2 = jnp.where(x2 > -0.01, -x2 * (1.0 + 0.5 * x2 * (1.0 + x2 * (1.0 / 3.0))), 1.0 - a * a)
        u = jnp.sqrt(one_m_a2) * (g[:, HEAD_DIM:] * xc)
        return a, u

    def conv(c, carry):
        r0 = pl.multiple_of(c * tc, tc)
        x = xa_ref[pl.ds(r0, tc), :].astype(F32)
        rp = pl.multiple_of(jnp.maximum(r0 - HALO, 0), HALO)
        rn = pl.multiple_of(jnp.minimum(r0 + tc, s - HALO), HALO)
        prev = jnp.where(c > 0, xa_ref[pl.ds(rp, HALO), :].astype(F32), 0.0)
        nxt = jnp.where(c < nc - 1, xa_ref[pl.ds(rn, HALO), :].astype(F32), 0.0)
        cat = jnp.concatenate([prev, x, nxt], axis=0)
        xc_ref[pl.ds(r0, tc), :] = (
            cb + cw[0:1] * pltpu.roll(cat, 1, 0)[HALO:HALO + tc] + cw[1:2] * x
            + cw[2:3] * pltpu.roll(cat, n - 1, 0)[HALO:HALO + tc]
            + cw[3:4] * pltpu.roll(cat, n - 2, 0)[HALO:HALO + tc])
        return carry

    lax.fori_loop(0, nc, conv, 0)

    def scan(i, carry):
        c_f, c_b = carry
        rf = pl.multiple_of(i * tc, tc)
        rb = pl.multiple_of((nc - 1 - i) * tc, tc)
        a, u = gates(xc_ref[pl.ds(rf, tc), :], 0)
        a, u = _scan8(a, u, pos, False)
        h, c_f = _chain8(a, u, c_f, False)
        hf_ref[pl.ds(rf, tc), :] = h
        a, u = gates(xc_ref[pl.ds(rb, tc), :], 1)
        a, u = _scan8(a, u, pos, True)
        h, c_b = _chain8(a, u, c_b, True)
        hb_ref[pl.ds(rb, tc), :] = h
        return c_f, c_b

    zero = jnp.zeros((1, HEAD_DIM), F32)
    lax.fori_loop(0, nc, scan, (zero, zero))

    def fin(c, carry):
        r0 = pl.multiple_of(c * tc, tc)
        y = jax.nn.gelu(ya_ref[pl.ds(r0, tc), :].astype(F32))
        o_ref[pl.ds(r0, tc), :] = ((hf_ref[pl.ds(r0, tc), :] + hb_ref[pl.ds(r0, tc), :]) * y).astype(o_ref.dtype)
        return carry

    lax.fori_loop(0, nc, fin, 0)


def _rglru(pbf, lp, bsz, s):
    t = pbf.shape[0]
    ca, cy = BF_A // HEAD_DIM, (BF_A + MIX_W) // HEAD_DIM
    return pl.pallas_call(
        _rglru_body,
        grid=(bsz, N_HEADS),
        in_specs=[
            pl.BlockSpec((s, HEAD_DIM), lambda b, h: (b, ca + h)),
            pl.BlockSpec((s, HEAD_DIM), lambda b, h: (b, cy + h)),
            pl.BlockSpec((4, HEAD_DIM), lambda b, h: (0, h)),
            pl.BlockSpec((1, HEAD_DIM), lambda b, h: (0, h)),
            pl.BlockSpec((1, HEAD_DIM, 4 * HEAD_DIM), lambda b, h: (h, 0, 0)),
            pl.BlockSpec((1, 1, 4 * HEAD_DIM), lambda b, h: (h, 0, 0)),
            pl.BlockSpec((2, HEAD_DIM), lambda b, h: (0, h)),
        ],
        out_specs=pl.BlockSpec((s, HEAD_DIM), lambda b, h: (b, h)),
        out_shape=jax.ShapeDtypeStruct((t, MIX_W), BF16),
        scratch_shapes=[pltpu.VMEM((s, HEAD_DIM), F32)] * 3,
        compiler_params=_cp(("parallel", "parallel")),
        name="rglru",
    )(pbf, pbf, lp["conv_a_w"], lp["conv_a_b"], lp["rglru_w"], lp["rglru_b"], lp["rglru_lam"])


def _mlstm_body(q_ref, k_ref, v_ref, og_ref, g_ref, ng_ref, o_ref, hf_ref, hb_ref, stf_ref, stb_ref):
    s = q_ref.shape[0]
    cl = min(MLSTM_CHUNK, s)
    nc = s // cl
    ri = lax.broadcasted_iota(jnp.int32, (cl, cl), 0)
    ci = lax.broadcasted_iota(jnp.int32, (cl, cl), 1)
    masks = (ri >= ci, ri <= ci)
    tris = (masks[0].astype(BF16), masks[1].astype(BF16))
    ones_v = jnp.ones((cl, HEAD_DIM), BF16)
    kscale = HEAD_DIM ** -0.5
    twice = lambda a: jnp.concatenate([a, a], axis=1)

    def load(r0, st_ref):
        return (g_ref[pl.ds(r0, cl), :], q_ref[pl.ds(r0, cl), :], k_ref[pl.ds(r0, cl), :],
                v_ref[pl.ds(r0, cl), :], st_ref[...])

    def chunk(vals, d, m_st):
        g, q, k, v, st = vals
        ig = jnp.broadcast_to(g[:, 2 * d:2 * d + 1], (cl, HEAD_DIM))
        lf = _log_sigmoid(jnp.broadcast_to(g[:, 2 * d + 1:2 * d + 2], (cl, HEAD_DIM)))
        b = _cumsum_mm(tris[d], lf)
        tot = b[0:1] if d else b[cl - 1:cl]
        c = ig - b
        k = k.astype(F32) * kscale
        v1 = jnp.concatenate([v, ones_v], axis=1)
        lw = tot + c
        m_loc = jnp.max(lw, axis=0, keepdims=True)
        kw = (k * jnp.exp(lw - m_loc)).astype(BF16)
        cn_loc = _dot_tn(kw, v1)
        dm = jnp.where(masks[d], c.T, -jnp.inf)
        m_rel = jnp.max(dm, axis=1, keepdims=True)
        am = jnp.exp(dm - m_rel) * _dot_nt(q, k.astype(BF16))
        nd_intra = _dot(am.astype(BF16), v1)
        nd_inter = _dot(q, st.astype(BF16))
        mu = jnp.maximum(m_rel, m_st)
        f_i = jnp.exp(m_rel - mu)
        f_s = jnp.exp(m_st - mu)
        num = f_i * nd_intra[:, :HEAD_DIM] + f_s * nd_inter[:, :HEAD_DIM]
        den = f_i * nd_intra[:, HEAD_DIM:] + f_s * nd_inter[:, HEAD_DIM:]
        out = num / jnp.maximum(jnp.abs(den), jnp.exp(-(b + mu)))
        m_new = jnp.maximum(tot + m_st, m_loc)
        st_new = twice(jnp.exp(tot + m_st - m_new)) * st + twice(jnp.exp(m_loc - m_new)) * cn_loc
        return out, m_new, st_new

    m0 = jnp.full((1, HEAD_DIM), STAB_INIT, F32)
    stf_ref[...] = jnp.zeros_like(stf_ref)
    stb_ref[...] = jnp.zeros_like(stb_ref)

    def body(i, carry):
        m_f, m_b = carry
        rf = pl.multiple_of(i * cl, cl)
        rb = pl.multiple_of((nc - 1 - i) * cl, cl)
        vals_f = load(rf, stf_ref)
        vals_b = load(rb, stb_ref)
        out_f, m_f, st_f = chunk(vals_f, 0, m_f)
        out_b, m_b, st_b = chunk(vals_b, 1, m_b)
        hf_ref[pl.ds(rf, cl), :] = out_f
        hb_ref[pl.ds(rb, cl), :] = out_b
        stf_ref[...] = st_f
        stb_ref[...] = st_b
        return m_f, m_b

    lax.fori_loop(0, nc, body, (m0, m0), unroll=2 if nc % 2 == 0 else 1)
    ng = ng_ref[...]

    def fin(c, carry):
        r0 = pl.multiple_of(c * cl, cl)
        y = _rms(hf_ref[pl.ds(r0, cl), :] + hb_ref[pl.ds(r0, cl), :], ng)
        og = jax.nn.sigmoid(og_ref[pl.ds(r0, cl), :].astype(F32))
        o_ref[pl.ds(r0, cl), :] = (og * y).astype(o_ref.dtype)
        return carry

    lax.fori_loop(0, nc, fin, 0)


def _mlstm(pbf, pfp, lp, bsz, s):
    t = pbf.shape[0]
    cb = BF_B // HEAD_DIM
    cg = FP_GATE // HEAD_DIM
    blk = lambda off: pl.BlockSpec((s, HEAD_DIM), lambda b, h: (b, off + h))
    return pl.pallas_call(
        _mlstm_body,
        grid=(bsz, N_HEADS),
        in_specs=[blk(cb), blk(cb + 4), blk(cb + 8), blk(cb + 12), blk(cg),
                  pl.BlockSpec((1, HEAD_DIM), lambda b, h: (0, h))],
        out_specs=pl.BlockSpec((s, HEAD_DIM), lambda b, h: (b, h)),
        out_shape=jax.ShapeDtypeStruct((t, MIX_W), BF16),
        scratch_shapes=[pltpu.VMEM((s, HEAD_DIM), F32)] * 2 + [pltpu.VMEM((HEAD_DIM, 2 * HEAD_DIM), F32)] * 2,
        compiler_params=_cp(("parallel", "parallel")),
        name="mlstm",
    )(pbf, pbf, pbf, pbf, pfp, lp["mlstm_norm"])


def _block_ref(b, m, rev):
    c, w = b.shape
    parts = []
    for blk in range(c // (2 * m)):
        row = blk * 2 * m + (m if rev else m - 1)
        parts.append(jnp.broadcast_to(b[row:row + 1, :], (2 * m, w)))
    return parts[0] if len(parts) == 1 else jnp.concatenate(parts, axis=0)


def _hgrn_body(layer, q_ref, ff_ref, fb_ref, i_ref, g_ref, lbl_ref, ng_ref, o_ref, of_ref, ob_ref, stf_ref,
               stb_ref):
    s = q_ref.shape[0]
    c = min(HGRN_CHUNK, s)
    nc = s // c
    band = min(HGRN_BAND, c)
    if layer > 0:
        lg = lbl_ref[...]
        e = jnp.exp(lg - jnp.max(lg, axis=0, keepdims=True))
        p = e / jnp.sum(e, axis=0, keepdims=True)
        lb = p[1:2, :]
        for r in range(2, layer + 1):
            lb = lb + p[r:r + 1, :]
        log_lb = jnp.log(lb)
        log_1m = jnp.log(1.0 - lb)
        one_m = 1.0 - lb

    ri = lax.broadcasted_iota(jnp.int32, (c, c), 0)
    ci = lax.broadcasted_iota(jnp.int32, (c, c), 1)
    tris = ((ri >= ci).astype(BF16), (ri <= ci).astype(BF16))
    rw = lax.broadcasted_iota(jnp.int32, (c, HEAD_DIM), 0)
    pos = lax.broadcasted_iota(jnp.int32, (c, 1), 0) & (band - 1)
    levels = []
    m = c // 2
    while m >= band:
        levels.append(m)
        m //= 2
    upper = {m: (rw & (2 * m - 1)) >= m for m in levels}
    same_blk = {m: (ri >> int(math.log2(2 * m))) == (ci >> int(math.log2(2 * m))) for m in levels if 2 * m < c}
    diags = ([ri - ci == dl for dl in range(band)], [ci - ri == dl for dl in range(band)])

    def chunk(r0, rev, st_ref):
        f = (fb_ref if rev else ff_ref)[pl.ds(r0, c), :]
        e_f = jnp.exp(-jnp.abs(f))
        den = 1.0 + e_f
        lf = jnp.minimum(f, 0.0) - jnp.log(den)
        kd = jnp.where(f >= 0.0, e_f, 1.0) / den
        if layer > 0:
            x = log_1m + lf
            mx = jnp.maximum(log_lb, x)
            lf = mx + jnp.log(jnp.exp(log_lb - mx) + jnp.exp(x - mx))
            kd = one_m * kd
        b = _cumsum_mm(tris[rev], lf)
        tot = b[0:1] if rev else b[c - 1:c]
        qs = jax.nn.silu(q_ref[pl.ds(r0, c), :].astype(F32))
        v = i_ref[pl.ds(r0, c), :]
        st = st_ref[...]
        o = _dot_nt((qs * jnp.exp(b)).astype(BF16), st.astype(BF16))
        k_out = (kd * jnp.exp(tot - b)).astype(BF16)
        st_ref[...] = jnp.exp(tot) * st + _dot_tn(v, k_out)
        amat = None
        for m in levels:
            bref = _block_ref(b, m, rev)
            up = upper[m]
            qm = jnp.logical_not(up) if rev else up
            km = up if rev else jnp.logical_not(up)
            qt = (qs * jnp.exp(jnp.where(qm, b - bref, -jnp.inf))).astype(BF16)
            kt = (kd * jnp.exp(jnp.where(km, bref - b, -jnp.inf))).astype(BF16)
            am = _dot_nt(qt, kt)
            if m in same_blk:
                am = jnp.where(same_blk[m], am, 0.0)
            amat = am if amat is None else amat + am
        for dl in range(band):
            if dl == 0:
                a = jnp.sum(qs * kd, axis=1, keepdims=True)
            else:
                sh = (c - dl) if rev else dl
                valid = (pos + dl < band) if rev else (pos >= dl)
                ex = jnp.exp(b - pltpu.roll(b, sh, 0))
                a = jnp.where(valid, jnp.sum(qs * pltpu.roll(kd, sh, 0) * ex, axis=1, keepdims=True), 0.0)
            term = jnp.where(diags[rev][dl], a, 0.0)
            amat = term if amat is None else amat + term
        return o + _dot(amat.astype(BF16), v)

    stf_ref[...] = jnp.zeros_like(stf_ref)
    stb_ref[...] = jnp.zeros_like(stb_ref)

    def body(i, carry):
        rf = pl.multiple_of(i * c, c)
        rb = pl.multiple_of((nc - 1 - i) * c, c)
        of_ref[pl.ds(rf, c), :] = chunk(rf, 0, stf_ref)
        ob_ref[pl.ds(rb, c), :] = chunk(rb, 1, stb_ref)
        return carry

    lax.fori_loop(0, nc, body, 0, unroll=2 if nc % 2 == 0 else 1)
    ng = ng_ref[...]

    def fin(ci_, carry):
        r0 = pl.multiple_of(ci_ * c, c)
        hs = of_ref[pl.ds(r0, c), :] + ob_ref[pl.ds(r0, c), :]
        gate = jax.nn.sigmoid(g_ref[pl.ds(r0, c), :].astype(F32))
        o_ref[pl.ds(r0, c), :] = (gate * _rms(hs, ng)).astype(o_ref.dtype)
        return carry

    lax.fori_loop(0, nc, fin, 0)


def _hgrn(pbf, pfp, lp, lb_logits, layer, bsz, s):
    t = pbf.shape[0]
    cd = BF_D3 // HEAD_DIM
    blk = lambda off: pl.BlockSpec((s, HEAD_DIM), lambda b, h: (b, off + h))
    depth = lb_logits.shape[0]
    return pl.pallas_call(
        functools.partial(_hgrn_body, layer),
        grid=(bsz, N_HEADS),
        in_specs=[blk(cd), blk(FP_FF // HEAD_DIM), blk(FP_FB // HEAD_DIM), blk(cd + 4), blk(cd + 8),
                  pl.BlockSpec((depth, HEAD_DIM), lambda b, h: (0, h)),
                  pl.BlockSpec((1, HEAD_DIM), lambda b, h: (0, h))],
        out_specs=pl.BlockSpec((s, HEAD_DIM), lambda b, h: (b, h)),
        out_shape=jax.ShapeDtypeStruct((t, MIX_W), BF16),
        scratch_shapes=[pltpu.VMEM((s, HEAD_DIM), F32)] * 2 + [pltpu.VMEM((HEAD_DIM, HEAD_DIM), F32)] * 2,
        compiler_params=_cp(("parallel", "parallel")),
        name="hgrn2",
    )(pbf, pfp, pfp, pbf, pbf, lb_logits, lp["hgrn_norm"])


def _shift3(cat, w, bias, ts):
    n = ts + 2 * HALO
    return (bias + w[0:1] * pltpu.roll(cat, 1, 0)[HALO:HALO + ts] + w[1:2] * cat[HALO:HALO + ts]
            + w[2:3] * pltpu.roll(cat, n - 1, 0)[HALO:HALO + ts])


def _hyena_prep_body(x_ref, xp_ref, xn_ref, cw_ref, cb_ref, z_ref, x1_ref):
    i = pl.program_id(1)
    ts = x_ref.shape[0]
    prev = jnp.where(i > 0, xp_ref[...].astype(F32), 0.0)
    nxt = jnp.where(i < pl.num_programs(1) - 1, xn_ref[...].astype(F32), 0.0)
    cat = jnp.concatenate([prev, x_ref[...].astype(F32), nxt], axis=0)
    u = _shift3(cat, cw_ref[...], cb_ref[...], ts)
    z_ref[...] = (u[:, 2 * MIX_W:] * u[:, :MIX_W]).astype(z_ref.dtype)
    x1_ref[...] = u[:, MIX_W:2 * MIX_W].astype(x1_ref.dtype)


def _hyena_prep(pbf, lp, bsz, s):
    t = pbf.shape[0]
    ts = min(512, s)
    per = s // ts
    hb = ts // HALO
    nb = t // HALO
    w = 3 * MIX_W
    return pl.pallas_call(
        _hyena_prep_body,
        grid=(bsz, per),
        in_specs=[
            pl.BlockSpec((ts, w), lambda b, i: (b * per + i, 0)),
            pl.BlockSpec((HALO, w), lambda b, i: (jnp.maximum((b * per + i) * hb - 1, 0), 0)),
            pl.BlockSpec((HALO, w), lambda b, i: (jnp.minimum((b * per + i + 1) * hb, nb - 1), 0)),
            pl.BlockSpec((3, w), lambda b, i: (0, 0)),
            pl.BlockSpec((1, w), lambda b, i: (0, 0)),
        ],
        out_specs=[pl.BlockSpec((ts, MIX_W), lambda b, i: (b * per + i, 0)),
                   pl.BlockSpec((ts, MIX_W), lambda b, i: (b * per + i, 0))],
        out_shape=[jax.ShapeDtypeStruct((t, MIX_W), BF16), jax.ShapeDtypeStruct((t, MIX_W), BF16)],
        compiler_params=_cp(("parallel", "arbitrary")),
        name="hyena_prep",
    )(pbf, pbf, pbf, lp["hyena_conv_w"], lp["hyena_conv_b"])


def _dft1_body(prec, x_ref, f_ref, o_ref):
    n1h, r, c = x_ref.shape
    x = x_ref[...].reshape(n1h * r, c)
    o = jnp.dot(f_ref[...], x, preferred_element_type=F32, precision=prec)
    o_ref[...] = o.reshape(o_ref.shape).astype(o_ref.dtype)


def _dft1(x4, kron1, out_dtype, prec=None):
    bsz, n1h, n2, c = x4.shape
    r = kron1.shape[1] // n1h
    n1 = kron1.shape[0] // (2 * r)
    return pl.pallas_call(
        functools.partial(_dft1_body, prec),
        grid=(bsz, n2 // r),
        in_specs=[pl.BlockSpec((None, n1h, r, c), lambda b, j: (b, 0, j, 0)),
                  pl.BlockSpec(kron1.shape, lambda b, j: (0, 0))],
        out_specs=pl.BlockSpec((None, 2, n1, r, c), lambda b, j: (b, 0, 0, j, 0)),
        out_shape=jax.ShapeDtypeStruct((bsz, 2, n1, n2, c), out_dtype),
        compiler_params=_cp(("parallel", "parallel")),
        name="dft_stage1",
    )(x4, kron1)


def _dft2_conv_body(kb, a_ref, f_ref, fi_ref, g_ref, o_ref):
    n2 = a_ref.shape[2]
    for kk in range(kb):
        a = jnp.concatenate([a_ref[0, kk], a_ref[1, kk]], axis=0)
        x = _dot(f_ref[kk], a)
        xr, xi = x[:n2], x[n2:]
        gr, gi = g_ref[kk, 0], g_ref[kk, 1]
        y = jnp.concatenate([xr * gr - xi * gi, xr * gi + xi * gr], axis=0).astype(BF16)
        bq = _dot(fi_ref[kk], y)
        o_ref[0, kk] = bq[:n2].astype(o_ref.dtype)
        o_ref[1, kk] = bq[n2:].astype(o_ref.dtype)


def _dft2_conv(a5, f2, f2i, gspec, kb):
    bsz, _, n1, n2, c = a5.shape
    return pl.pallas_call(
        functools.partial(_dft2_conv_body, kb),
        grid=(n1 // kb, bsz),
        in_specs=[pl.BlockSpec((None, 2, kb, n2, c), lambda k, b: (b, 0, k, 0, 0)),
                  pl.BlockSpec((kb, 2 * n2, 2 * n2), lambda k, b: (k, 0, 0)),
                  pl.BlockSpec((kb, 2 * n2, 2 * n2), lambda k, b: (k, 0, 0)),
                  pl.BlockSpec((kb, 2, n2, c), lambda k, b: (k, 0, 0, 0))],
        out_specs=pl.BlockSpec((None, 2, kb, n2, c), lambda k, b: (b, 0, k, 0, 0)),
        out_shape=jax.ShapeDtypeStruct(a5.shape, BF16),
        compiler_params=_cp(("parallel", "arbitrary")),
        name="dft_stage2_conv",
    )(a5, f2, f2i, gspec)


def _dft2_filter_body(kb, a_ref, f_ref, o_ref):
    n2 = a_ref.shape[3]
    k0 = pl.program_id(0) * kb
    for kk in range(kb):
        sgn = (1 - 2 * ((k0 + kk) & 1)).astype(F32)
        xs = []
        for e in range(2):
            a = jnp.concatenate([a_ref[e, 0, kk], a_ref[e, 1, kk]], axis=0)
            xs.append(jnp.dot(f_ref[kk], a, preferred_element_type=F32, precision=HIGHEST))
        x = xs[0] + sgn * xs[1]
        o_ref[kk, 0] = x[:n2]
        o_ref[kk, 1] = x[n2:]


def _dft2_filter(a5, f2, kb):
    _, _, n1, n2, c = a5.shape
    return pl.pallas_call(
        functools.partial(_dft2_filter_body, kb),
        grid=(n1 // kb,),
        in_specs=[pl.BlockSpec((2, 2, kb, n2, c), lambda k: (0, 0, k, 0, 0)),
                  pl.BlockSpec((kb, 2 * n2, 2 * n2), lambda k: (k, 0, 0))],
        out_specs=pl.BlockSpec((kb, 2, n2, c), lambda k: (k, 0, 0, 0)),
        out_shape=jax.ShapeDtypeStruct((n1, 2, n2, c), F32),
        compiler_params=_cp(("parallel",)),
        name="dft_stage2_filter",
    )(a5, f2)


def _dft3_body(b_ref, f_ref, x1_ref, z_ref, sk_ref, o_ref):
    _, n1, r, c = b_ref.shape
    y = _dot(f_ref[...], b_ref[...].reshape(2 * n1 * r, c)).reshape(o_ref.shape)
    z = z_ref[...].astype(F32)
    o_ref[...] = (x1_ref[...].astype(F32) * (y + sk_ref[...] * z)).astype(o_ref.dtype)


def _dft3(b5, kron3, x1_4, z4, skip):
    bsz, _, n1, n2, c = b5.shape
    n1h = n1 // 2
    r = kron3.shape[0] // n1h
    row = pl.BlockSpec((None, n1h, r, c), lambda b, j: (b, 0, j, 0))
    return pl.pallas_call(
        _dft3_body,
        grid=(bsz, n2 // r),
        in_specs=[pl.BlockSpec((None, 2, n1, r, c), lambda b, j: (b, 0, 0, j, 0)),
                  pl.BlockSpec(kron3.shape, lambda b, j: (0, 0)),
                  row, row,
                  pl.BlockSpec((1, c), lambda b, j: (0, 0))],
        out_specs=row,
        out_shape=jax.ShapeDtypeStruct((bsz, n1h, n2, c), BF16),
        compiler_params=_cp(("parallel", "parallel")),
        name="dft_stage3",
    )(b5, kron3, x1_4, z4, skip)


def _hyena_filter_body(seq_len, pos_ref, sc_ref, bands_ref, w1t_ref, w1c_ref, w1s_ref, b1_ref, fr1_ref,
                       w2_ref, b2_ref, fr2_ref, w3_ref, rate_ref, o_ref):
    pos = pos_ref[...]
    t = pos * (1.0 / (seq_len - 1))
    arg = (pos * (2.0 * math.pi / seq_len)) * bands_ref[...]
    hdot = lambda a, b: jnp.dot(a, b, preferred_element_type=F32, precision=HIGHEST)
    pre = t * w1t_ref[...] + hdot(jnp.cos(arg), w1c_ref[...]) - hdot(jnp.sin(arg), w1s_ref[...]) + b1_ref[...]
    hid = jnp.sin(fr1_ref[...] * pre)
    hid = jnp.sin(fr2_ref[...] * (hdot(hid, w2_ref[...]) + b2_ref[...]))
    o_ref[...] = hdot(hid, w3_ref[...]) * jnp.exp(-t * rate_ref[...]) * sc_ref[...]


def _hyena_filter(lp, s):
    ts = min(256, s)
    r = 2 * s + ts
    pos = np.concatenate([np.arange(s), s - np.arange(s), np.zeros(ts)]).astype(np.float32)[:, None]
    sc = np.ones((r, 1), np.float32)
    sc[s] = 0.0
    pad = HEAD_DIM - HYENA_HID
    bands = np.zeros((1, HEAD_DIM), np.float32)
    bands[0, :HYENA_BANDS] = np.linspace(1e-4, HYENA_BANDS - 1, HYENA_BANDS)
    rate = np.abs(np.linspace(math.log(HYENA_TARGET) / HYENA_FAST, math.log(HYENA_TARGET) / HYENA_SLOW, MIX_W))
    rate = np.tile(rate, 2).astype(np.float32)[None, :]
    w1 = lp["hyena_w1"]
    padc = lambda a: jnp.pad(a, ((0, 0), (0, pad)))
    w1t = padc(w1[0:1])
    w1c = jnp.pad(w1[1:1 + HYENA_BANDS], ((0, HEAD_DIM - HYENA_BANDS), (0, pad)))
    w1s = jnp.pad(w1[1 + HYENA_BANDS:], ((0, HEAD_DIM - HYENA_BANDS), (0, pad)))
    w2 = jnp.pad(lp["hyena_w2"], ((0, pad), (0, pad)))
    w3 = jnp.pad(lp["hyena_w3"], ((0, pad), (0, 0)))
    vec = lambda a: padc(a[None, :])
    full = lambda a: pl.BlockSpec(a.shape, lambda i: (0, 0))
    args = [jnp.asarray(bands), w1t, w1c, w1s, vec(lp["hyena_b1"]), vec(lp["hyena_freq1"]), w2,
            vec(lp["hyena_b2"]), vec(lp["hyena_freq2"]), w3, jnp.asarray(rate)]
    out = pl.pallas_call(
        functools.partial(_hyena_filter_body, s),
        grid=(r // ts,),
        in_specs=[pl.BlockSpec((ts, 1), lambda i: (i, 0)), pl.BlockSpec((ts, 1), lambda i: (i, 0))]
                 + [full(a) for a in args],
        out_specs=pl.BlockSpec((ts, 2 * MIX_W), lambda i: (i, 0)),
        out_shape=jax.ShapeDtypeStruct((r, 2 * MIX_W), F32),
        compiler_params=_cp(("parallel",)),
        name="hyena_filter",
    )(jnp.asarray(pos), jnp.asarray(sc), *args)
    return out[:s, :MIX_W], out[s:2 * s, MIX_W:], out[2 * s:2 * s + 1, MIX_W:]


def _dft_tables(s):
    n = 2 * s
    n2 = min(DFT_N2, s // 8)
    n1 = n // n2
    n1h = n1 // 2
    two_pi = 2.0 * math.pi
    k1 = jnp.arange(n1, dtype=jnp.int32)
    a1 = (two_pi / n1) * ((k1[:, None] * k1[None, :n1h]) % n1).astype(F32)
    f1 = jnp.concatenate([jnp.cos(a1), -jnp.sin(a1)], axis=0)
    f1i = jnp.concatenate([jnp.cos(a1).T, -jnp.sin(a1).T], axis=1) * (1.0 / n)
    j = jnp.arange(n2, dtype=jnp.int32)
    ph = (j[None, :, None] * j[None, None, :] * n1 + j[None, None, :] * k1[:, None, None]) % n
    a2 = (two_pi / n) * ph.astype(F32)
    tr, ti = jnp.cos(a2), -jnp.sin(a2)
    f2 = jnp.concatenate([jnp.concatenate([tr, -ti], axis=2), jnp.concatenate([ti, tr], axis=2)], axis=1)
    kron = lambda m, r: jnp.kron(m, jnp.eye(r, dtype=F32))
    r32, r16 = min(SUBLANES, n2), min(2 * SUBLANES, n2)
    return dict(n1=n1, n2=n2, f2=f2, f2_bf=f2.astype(BF16), f2i_bf=jnp.swapaxes(f2, 1, 2).astype(BF16),
                kron1_f32=kron(f1, r32), kron1_bf=kron(f1, r16).astype(BF16), kron3_bf=kron(f1i, r16).astype(BF16))


def _hyena_spectrum(lp, s, tab):
    gpos, gneg, hb0 = _hyena_filter(lp, s)
    n1, n2 = tab["n1"], tab["n2"]
    a = _dft1(jnp.stack([gpos, gneg]).reshape(2, n1 // 2, n2, MIX_W), tab["kron1_f32"], F32, HIGHEST)
    return _dft2_filter(a, tab["f2"], min(2, n1)), hb0


def _hyena(pbf, lp, gspec, hb0, tab, bsz, s):
    n1, n2 = tab["n1"], tab["n2"]
    z, x1 = _hyena_prep(pbf, lp, bsz, s)
    z4 = z.reshape(bsz, n1 // 2, n2, MIX_W)
    a = _dft1(z4, tab["kron1_bf"], BF16)
    bq = _dft2_conv(a, tab["f2_bf"], tab["f2i_bf"], gspec, min(2, n1))
    out = _dft3(bq, tab["kron3_bf"], x1.reshape(z4.shape), z4, lp["hyena_skip"][None, :] + hb0)
    return out.reshape(bsz * s, MIX_W)


def _merge_body(h_ref, g0_ref, g1_ref, g2_ref, g3_ref, ba_ref, bb_ref, bc_ref, bd_ref, wb_ref, wo_ref, o_ref):
    merged = None
    for idx, (g_ref, br) in enumerate(zip((g0_ref, g1_ref, g2_ref, g3_ref), (ba_ref, bb_ref, bc_ref, bd_ref))):
        term = jax.nn.sigmoid(g_ref[...].astype(F32)) * _dot(br[...], wb_ref[idx])
        merged = term if merged is None else merged + term
    o_ref[...] = h_ref[...] + _dot(merged.astype(BF16), wo_ref[...])


def _merge(h, pbf, branches, lp):
    t, d = h.shape
    tm = min(512, t)
    gcol = BF_G // d
    row = lambda w: pl.BlockSpec((tm, w), lambda i: (i, 0))
    gate = lambda idx: pl.BlockSpec((tm, d), lambda i: (i, gcol + idx))
    return pl.pallas_call(
        _merge_body,
        grid=(t // tm,),
        in_specs=[row(d), gate(0), gate(1), gate(2), gate(3),
                  row(MIX_W), row(MIX_W), row(MIX_W), row(MIX_W),
                  pl.BlockSpec(lp["w_branch"].shape, lambda i: (0, 0, 0)),
                  pl.BlockSpec(lp["w_out"].shape, lambda i: (0, 0))],
        out_specs=row(d),
        out_shape=jax.ShapeDtypeStruct((t, d), F32),
        compiler_params=_cp(("parallel",)),
        name="merge",
    )(h, pbf, pbf, pbf, pbf, *branches, lp["w_branch"], lp["w_out"])


def _ffn_body(per, final, h_ref, hp_ref, hn_ref, p_ref, gf_ref, wu_ref, cw_ref, cb_ref, wd_ref, gp_ref, wg_ref,
              wp_ref, gl_ref, o_ref):
    i = pl.program_id(0)
    tm = h_ref.shape[0]
    dff = wd_ref.shape[0]
    gf = gf_ref[...]
    h = h_ref[...]
    first = (i % per) == 0
    last = (i % per) == per - 1
    xp = jnp.where(first, 0.0, _rms(hp_ref[...], gf))
    xn = jnp.where(last, 0.0, _rms(hn_ref[...], gf))
    cat = jnp.concatenate([xp, _rms(h, gf), xn], axis=0).astype(BF16)
    cb_w = min(512, dff)
    acc = jnp.zeros_like(h)
    for cb in range(dff // cb_w):
        lo, lo2 = cb * cb_w, dff + cb * cb_w
        u1 = _shift3(_dot(cat, wu_ref[:, lo:lo + cb_w]), cw_ref[:, lo:lo + cb_w], cb_ref[:, lo:lo + cb_w], tm)
        u2 = _shift3(_dot(cat, wu_ref[:, lo2:lo2 + cb_w]), cw_ref[:, lo2:lo2 + cb_w], cb_ref[:, lo2:lo2 + cb_w], tm)
        acc = acc + _dot((jax.nn.gelu(u1) * u2).astype(BF16), wd_ref[lo:lo + cb_w, :])
    h = h + acc
    gate = jax.nn.sigmoid(_dot(_rms(h, gp_ref[...]).astype(BF16), wg_ref[...]))
    h = h + gate * _dot(p_ref[...].astype(BF16), wp_ref[...])
    if final:
        h = _rms(h, gl_ref[...])
    o_ref[...] = h


def _ffn(h, p, lp, final_norm, s, final):
    t, d = h.shape
    tm = min(512, s)
    per = s // tm
    hb = tm // HALO
    nb = t // HALO
    full = lambda a: pl.BlockSpec(a.shape, lambda i: (0,) * a.ndim)
    row = lambda w: pl.BlockSpec((tm, w), lambda i: (i, 0))
    ws = [lp["norm_ffn"], lp["w_up"], lp["ffn_conv_w"], lp["ffn_conv_b"], lp["w_down"], lp["norm_ple"],
          lp["w_ple_gate"], lp["w_ple"], final_norm]
    return pl.pallas_call(
        functools.partial(_ffn_body, per, final),
        grid=(t // tm,),
        in_specs=[row(d),
                  pl.BlockSpec((HALO, d), lambda i: (jnp.maximum(i * hb - 1, 0), 0)),
                  pl.BlockSpec((HALO, d), lambda i: (jnp.minimum((i + 1) * hb, nb - 1), 0)),
                  row(p.shape[1])] + [full(a) for a in ws],
        out_specs=row(d),
        out_shape=jax.ShapeDtypeStruct((t, d), F32),
        compiler_params=_cp(("parallel",)),
        name="ffn_ple",
    )(h, h, h, p, *ws)


def _prepare_params(prm):
    d_model = prm["w_in"].shape[1]
    depth = prm["w_in"].shape[0]
    off_b = 2 * MIX_W
    off_gate = off_b + 4 * MIX_W
    off_c = off_gate + 4 * N_HEADS
    off_d = off_c + 3 * MIX_W
    off_g = off_d + 5 * MIX_W

    def regroup(a):
        sl = lambda lo, w: a[..., lo:lo + w]
        bf = jnp.concatenate([sl(off_c, 3 * MIX_W), sl(off_d, MIX_W), sl(off_d + 3 * MIX_W, 2 * MIX_W),
                              sl(off_g, 4 * d_model), sl(0, 2 * MIX_W), sl(off_b, 4 * MIX_W)], axis=-1)
        gates = sl(off_gate, 4 * N_HEADS).reshape(a.shape[:-1] + (4, N_HEADS))
        gates = jnp.swapaxes(gates, -1, -2)
        gates = jnp.pad(gates, [(0, 0)] * (gates.ndim - 1) + [(0, HEAD_DIM - 4)])
        fp = jnp.concatenate([sl(off_d + MIX_W, 2 * MIX_W), gates.reshape(a.shape[:-1] + (N_HEADS * HEAD_DIM,))],
                             axis=-1)
        return bf, fp

    w_bf, w_fp = regroup(prm["w_in"])
    b_bf, b_fp = regroup(prm["b_in"][:, None, :])
    gw = jnp.transpose(prm["rglru_w"], (0, 3, 4, 1, 2, 5)).reshape(depth, N_HEADS, HEAD_DIM, 4 * HEAD_DIM)
    gb = prm["rglru_b"].reshape(depth, 2, 2, N_HEADS, HEAD_DIM)
    gb = jnp.transpose(gb, (0, 3, 1, 2, 4)).reshape(depth, N_HEADS, 1, 4 * HEAD_DIM)
    row = lambda a: a[:, None, :]
    out = dict(
        norm_mix=row(prm["norm_mix"]), w_bf=w_bf.astype(BF16), b_bf=b_bf, w_fp=w_fp.astype(BF16), b_fp=b_fp,
        conv_a_w=prm["conv_a_w"], conv_a_b=row(prm["conv_a_b"]), rglru_w=gw.astype(BF16), rglru_b=gb,
        rglru_lam=prm["rglru_lam"], mlstm_norm=row(prm["mlstm_norm"]),
        hyena_conv_w=prm["hyena_conv_w"], hyena_conv_b=row(prm["hyena_conv_b"]),
        hyena_w1=prm["hyena_w1"], hyena_b1=prm["hyena_b1"], hyena_freq1=prm["hyena_freq1"],
        hyena_w2=prm["hyena_w2"], hyena_b2=prm["hyena_b2"], hyena_freq2=prm["hyena_freq2"],
        hyena_w3=prm["hyena_w3"], hyena_skip=prm["hyena_skip"], hgrn_norm=row(prm["hgrn_norm"]),
        w_branch=prm["w_branch"].astype(BF16), w_out=prm["w_out"].astype(BF16),
        norm_ffn=row(prm["norm_ffn"]), w_up=prm["w_up"].astype(BF16), ffn_conv_w=prm["ffn_conv_w"],
        ffn_conv_b=row(prm["ffn_conv_b"]), w_down=prm["w_down"].astype(BF16), norm_ple=row(prm["norm_ple"]),
        w_ple_gate=prm["w_ple_gate"].astype(BF16), w_ple=prm["w_ple"].astype(BF16),
    )
    return out


def _trunk(x, p, prm, lb_logits, final_norm):
    bsz, s, d = x.shape
    depth = p.shape[0]
    h = x.reshape(bsz * s, d)
    tab = _dft_tables(s)
    for i in range(depth):
        lp = {k: v[i] for k, v in prm.items()}
        pbf, pfp = _proj(h, lp["norm_mix"], lp["w_bf"], lp["b_bf"], lp["w_fp"], lp["b_fp"])
        gspec, hb0 = _hyena_spectrum(lp, s, tab)
        branches = (_rglru(pbf, lp, bsz, s), _mlstm(pbf, pfp, lp, bsz, s),
                    _hyena(pbf, lp, gspec, hb0, tab, bsz, s), _hgrn(pbf, pfp, lp, lb_logits, i, bsz, s))
        h = _merge(h, pbf, branches, lp)
        h = _ffn(h, p[i].reshape(bsz * s, -1), lp, final_norm, s, i == depth - 1)
    return h.reshape(bsz, s, d)


def kernel(x_prompt, x_sample, p_prompt, p_sample, norm_mix, w_in, b_in, conv_a_w, conv_a_b, rglru_w, rglru_b, rglru_lam, mlstm_norm, hyena_conv_w, hyena_conv_b, hyena_w1, hyena_b1, hyena_freq1, hyena_w2, hyena_b2, hyena_freq2, hyena_w3, hyena_skip, hgrn_lb_logits, hgrn_norm, w_branch, w_out, norm_ffn, w_up, ffn_conv_w, ffn_conv_b, w_down, norm_ple, w_ple_gate, w_ple, final_norm):
    prm = _prepare_params(dict(
        norm_mix=norm_mix, w_in=w_in, b_in=b_in, conv_a_w=conv_a_w, conv_a_b=conv_a_b, rglru_w=rglru_w,
        rglru_b=rglru_b, rglru_lam=rglru_lam, mlstm_norm=mlstm_norm, hyena_conv_w=hyena_conv_w,
        hyena_conv_b=hyena_conv_b, hyena_w1=hyena_w1, hyena_b1=hyena_b1, hyena_freq1=hyena_freq1,
        hyena_w2=hyena_w2, hyena_b2=hyena_b2, hyena_freq2=hyena_freq2, hyena_w3=hyena_w3,
        hyena_skip=hyena_skip, hgrn_norm=hgrn_norm, w_branch=w_branch, w_out=w_out, norm_ffn=norm_ffn,
        w_up=w_up, ffn_conv_w=ffn_conv_w, ffn_conv_b=ffn_conv_b, w_down=w_down, norm_ple=norm_ple,
        w_ple_gate=w_ple_gate, w_ple=w_ple))
    fn = final_norm[None, :]
    y_prompt = _trunk(x_prompt, p_prompt, prm, hgrn_lb_logits, fn)
    y_sample = _trunk(x_sample, p_sample, prm, hgrn_lb_logits, fn)
    return (y_prompt, y_sample)
```

```python
import collections
import functools
import math

import numpy as np
import jax
import jax.numpy as jnp
from jax import lax
from jax.experimental import pallas as pl
from jax.experimental.pallas import tpu as pltpu

F32 = jnp.float32
BF16 = jnp.bfloat16
HIGHEST = lax.Precision.HIGHEST

N_HEADS = 4
HEAD_DIM = 128
MIX_W = N_HEADS * HEAD_DIM
LRU_C = 8.0
LOG2_E = math.log2(math.e)
EPS = 1e-6
STAB_INIT = -1e30
HYENA_BANDS = 16
HYENA_HID = 64
HYENA_FAST = 0.3
HYENA_SLOW = 1.5
HYENA_TARGET = 1e-2

MLSTM_CHUNK = 128
HGRN_CHUNK = 128
MIXER_GROUP = 4
HGRN_BAND = 4
SCAN_CHUNK = 256
HALO = 16
SUBLANES = 8
PROJ_COLS = 1280
DFT_N2 = 128
V7X_VMEM_LIMIT = 56 * 1024 * 1024

BF_C, BF_D3, BF_G, BF_A, BF_B = 0, 1536, 3072, 7168, 8192
BF_COLS = 10240
FP_FF, FP_FB, FP_GATE = 0, 512, 1024
FP_COLS = 1536


def _cp(sem, vmem=V7X_VMEM_LIMIT):
    return pltpu.CompilerParams(dimension_semantics=sem, vmem_limit_bytes=vmem)


def _rms(x, g):
    return x * lax.rsqrt(jnp.mean(x * x, axis=-1, keepdims=True) + EPS) * g


def _log_sigmoid(x):
    return jnp.minimum(x, 0.0) - jnp.log(1.0 + jnp.exp(-jnp.abs(x)))


def _dot(a, b):
    return jnp.dot(a, b, preferred_element_type=F32)


def _dot_nt(a, b):
    return lax.dot_general(a, b, (((1,), (1,)), ((), ())), preferred_element_type=F32)


def _dot_tn(a, b):
    return lax.dot_general(a, b, (((0,), (0,)), ((), ())), preferred_element_type=F32)


def _cumsum_mm(tri, x):
    hi = x.astype(BF16)
    lo = (x - hi.astype(F32)).astype(BF16)
    return _dot(tri, hi) + _dot(tri, lo)


def _proj_body(h_ref, g_ref, wb_ref, bb_ref, wf_ref, bf_ref, ob_ref, of_ref):
    xn = _rms(h_ref[...], g_ref[...]).astype(BF16)
    for w_ref, b_ref, o_ref in ((wb_ref, bb_ref, ob_ref), (wf_ref, bf_ref, of_ref)):
        n = w_ref.shape[1]
        tn = math.gcd(PROJ_COLS, n)
        for lo in range(0, n, tn):
            o_ref[:, lo:lo + tn] = (_dot(xn, w_ref[:, lo:lo + tn]) + b_ref[:, lo:lo + tn]).astype(o_ref.dtype)


def _proj(h, g, w_bf, b_bf, w_fp, b_fp):
    t, d = h.shape
    tm = min(256, t)
    once = lambda a: pl.BlockSpec(a.shape, lambda i: (0, 0), pipeline_mode=pl.Buffered(1))
    row = lambda n: pl.BlockSpec((tm, n), lambda i: (i, 0))
    return pl.pallas_call(
        _proj_body,
        grid=(t // tm,),
        in_specs=[row(d), once(g), once(w_bf), once(b_bf), once(w_fp), once(b_fp)],
        out_specs=[row(w_bf.shape[1]), row(w_fp.shape[1])],
        out_shape=[jax.ShapeDtypeStruct((t, w_bf.shape[1]), BF16), jax.ShapeDtypeStruct((t, w_fp.shape[1]), F32)],
        compiler_params=_cp(("parallel",)),
        name="in_proj",
    )(h, g, w_bf, b_bf, w_fp, b_fp)


def _scan8(a, u, pos, reverse):
    n = a.shape[0]
    d = 1
    while d < SUBLANES:
        if reverse:
            m = pos < SUBLANES - d
            sh = n - d
        else:
            m = pos >= d
            sh = d
        a_s = jnp.where(m, pltpu.roll(a, sh, 0), 1.0)
        u_s = jnp.where(m, pltpu.roll(u, sh, 0), 0.0)
        u = a * u_s + u
        a = a * a_s
        d *= 2
    return a, u


def _chain8(a, u, carry, reverse):
    groups = a.shape[0] // SUBLANES
    outs = [None] * groups
    for gi in (range(groups - 1, -1, -1) if reverse else range(groups)):
        lo = gi * SUBLANES
        h = u[lo:lo + SUBLANES] + a[lo:lo + SUBLANES] * carry
        outs[gi] = h
        carry = h[0:1] if reverse else h[SUBLANES - 1:SUBLANES]
    return jnp.concatenate(outs, axis=0), carry


def _rglru_body(xa_ref, ya_ref, cw_ref, cb_ref, gw_ref, gb_ref, lam_ref, o_ref, xc_ref, hf_ref, hb_ref):
    s = xa_ref.shape[0]
    tc = min(SCAN_CHUNK, s)
    nc = s // tc
    n = tc + 2 * HALO
    pos = lax.broadcasted_iota(jnp.int32, (tc, HEAD_DIM), 0) & (SUBLANES - 1)
    cw = cw_ref[...]
    cb = cb_ref[...]
    sp = jax.nn.softplus(-lam_ref[...])

    def gates(xc, d):
        w = gw_ref[0, :, d * 256:(d + 1) * 256]
        g = jax.nn.sigmoid(_dot(xc.astype(BF16), w) + gb_ref[0, :, d * 256:(d + 1) * 256])
        log_a = (-LRU_C) * g[:, :HEAD_DIM] * sp[d:d + 1]
        a = jnp.exp(log_a)
        x2 = 2.0 * log_a
        one_m_a2 = jnp.where(x2 > -0.01, -x2 * (1.0 + 0.5 * x2 * (1.0 + x2 * (1.0 / 3.0))), 1.0 - a * a)
        u = jnp.sqrt(one_m_a2) * (g[:, HEAD_DIM:] * xc)
        return a, u

    def conv(c, carry):
        r0 = pl.multiple_of(c * tc, tc)
        x = xa_ref[pl.ds(r0, tc), :].astype(F32)
        rp = pl.multiple_of(jnp.maximum(r0 - HALO, 0), HALO)
        rn = pl.multiple_of(jnp.minimum(r0 + tc, s - HALO), HALO)
        prev = jnp.where(c > 0, xa_ref[pl.ds(rp, HALO), :].astype(F32), 0.0)
        nxt = jnp.where(c < nc - 1, xa_ref[pl.ds(rn, HALO), :].astype(F32), 0.0)
        cat = jnp.concatenate([prev, x, nxt], axis=0)
        xc_ref[pl.ds(r0, tc), :] = (
            cb + cw[0:1] * pltpu.roll(cat, 1, 0)[HALO:HALO + tc] + cw[1:2] * x
            + cw[2:3] * pltpu.roll(cat, n - 1, 0)[HALO:HALO + tc]
            + cw[3:4] * pltpu.roll(cat, n - 2, 0)[HALO:HALO + tc])
        return carry

    lax.fori_loop(0, nc, conv, 0)

    def scan(i, carry):
        c_f, c_b = carry
        rf = pl.multiple_of(i * tc, tc)
        rb = pl.multiple_of((nc - 1 - i) * tc, tc)
        a, u = gates(xc_ref[pl.ds(rf, tc), :], 0)
        a, u = _scan8(a, u, pos, False)
        h, c_f = _chain8(a, u, c_f, False)
        hf_ref[pl.ds(rf, tc), :] = h
        a, u = gates(xc_ref[pl.ds(rb, tc), :], 1)
        a, u = _scan8(a, u, pos, True)
        h, c_b = _chain8(a, u, c_b, True)
        hb_ref[pl.ds(rb, tc), :] = h
        return c_f, c_b

    zero = jnp.zeros((1, HEAD_DIM), F32)
    lax.fori_loop(0, nc, scan, (zero, zero))

    def fin(c, carry):
        r0 = pl.multiple_of(c * tc, tc)
        y = jax.nn.gelu(ya_ref[pl.ds(r0, tc), :].astype(F32))
        o_ref[pl.ds(r0, tc), :] = ((hf_ref[pl.ds(r0, tc), :] + hb_ref[pl.ds(r0, tc), :]) * y).astype(o_ref.dtype)
        return carry

    lax.fori_loop(0, nc, fin, 0)


def _rglru(pbf, lp, bsz, s):
    t = pbf.shape[0]
    ca, cy = BF_A // HEAD_DIM, (BF_A + MIX_W) // HEAD_DIM
    return pl.pallas_call(
        _rglru_body,
        grid=(bsz, N_HEADS),
        in_specs=[
            pl.BlockSpec((s, HEAD_DIM), lambda b, h: (b, ca + h)),
            pl.BlockSpec((s, HEAD_DIM), lambda b, h: (b, cy + h)),
            pl.BlockSpec((4, HEAD_DIM), lambda b, h: (0, h)),
            pl.BlockSpec((1, HEAD_DIM), lambda b, h: (0, h)),
            pl.BlockSpec((1, HEAD_DIM, 4 * HEAD_DIM), lambda b, h: (h, 0, 0)),
            pl.BlockSpec((1, 1, 4 * HEAD_DIM), lambda b, h: (h, 0, 0)),
            pl.BlockSpec((2, HEAD_DIM), lambda b, h: (0, h)),
        ],
        out_specs=pl.BlockSpec((s, HEAD_DIM), lambda b, h: (b, h)),
        out_shape=jax.ShapeDtypeStruct((t, MIX_W), BF16),
        scratch_shapes=[pltpu.VMEM((s, HEAD_DIM), F32)] * 3,
        compiler_params=_cp(("parallel", "parallel")),
        name="rglru",
    )(pbf, pbf, lp["conv_a_w"], lp["conv_a_b"], lp["rglru_w"], lp["rglru_b"], lp["rglru_lam"])


_Steps = collections.namedtuple("_Steps", "init step fin carry0")


def _run_steps(n, part, unroll):
    part.init()

    def body(i, carry):
        carry, commit = part.step(i, carry)
        commit()
        return carry

    lax.fori_loop(0, n, body, part.carry0, unroll=unroll)

    def fin(c, carry):
        part.fin(c)
        return carry

    lax.fori_loop(0, n, fin, 0)


def _mlstm_steps(q_ref, k_ref, v_ref, og_ref, g_ref, ng_ref, o_ref, hf_ref, hb_ref, stf_ref, stb_ref):
    s = q_ref.shape[0]
    cl = min(MLSTM_CHUNK, s)
    nc = s // cl
    ri = lax.broadcasted_iota(jnp.int32, (cl, cl), 0)
    ci = lax.broadcasted_iota(jnp.int32, (cl, cl), 1)
    masks = (ri >= ci, ri <= ci)
    tris = (masks[0].astype(BF16), masks[1].astype(BF16))
    grp = math.gcd(nc, MIXER_GROUP)
    rows = grp * cl
    mask_rows = tuple(jnp.concatenate([m] * grp, axis=0) for m in masks)
    ones_v = jnp.ones((rows, HEAD_DIM), BF16)
    kscale = HEAD_DIM ** -0.5
    twice = lambda a: jnp.concatenate([a, a], axis=1)
    per_chunk = lambda x: [x[g * cl:(g + 1) * cl] for g in range(grp)]
    cat = lambda xs: jnp.concatenate(xs, axis=0)
    chunks3 = lambda x: x.reshape(grp, cl, x.shape[-1])

    def load(r0, st_ref):
        return (g_ref[pl.ds(r0, rows), :], q_ref[pl.ds(r0, rows), :], k_ref[pl.ds(r0, rows), :],
                v_ref[pl.ds(r0, rows), :], st_ref[...])

    def group(vals, d, m_st):
        g, q, k, v, st = vals
        ig = jnp.broadcast_to(g[:, 2 * d:2 * d + 1], (rows, HEAD_DIM))
        lf = _log_sigmoid(jnp.broadcast_to(g[:, 2 * d + 1:2 * d + 2], (rows, HEAD_DIM)))
        b = cat([_cumsum_mm(tris[d], x) for x in per_chunk(lf)])
        tot3 = chunks3(b)[:, 0:1] if d else chunks3(b)[:, cl - 1:cl]
        c = ig - b
        lw3 = tot3 + chunks3(c)
        m_loc3 = jnp.max(lw3, axis=1, keepdims=True)
        kf = k.astype(F32) * kscale
        kw = (chunks3(kf) * jnp.exp(lw3 - m_loc3)).astype(BF16).reshape(rows, HEAD_DIM)
        kb = kf.astype(BF16)
        v1 = jnp.concatenate([v, ones_v], axis=1)
        cn_loc = [_dot_tn(a, w) for a, w in zip(per_chunk(kw), per_chunk(v1))]
        dm = jnp.where(mask_rows[d], cat([x.T for x in per_chunk(c)]), -jnp.inf)
        m_rel = jnp.max(dm, axis=1, keepdims=True)
        qk = cat([_dot_nt(a, w) for a, w in zip(per_chunk(q), per_chunk(kb))])
        am = (jnp.exp(dm - m_rel) * qk).astype(BF16)
        nd_intra = cat([_dot(a, w) for a, w in zip(per_chunk(am), per_chunk(v1))])
        outs = [None] * grp
        for gi in (range(grp - 1, -1, -1) if d else range(grp)):
            sl = slice(gi * cl, (gi + 1) * cl)
            nd_inter = _dot(q[sl], st.astype(BF16))
            mu = jnp.maximum(m_rel[sl], m_st)
            f_i = jnp.exp(m_rel[sl] - mu)
            f_s = jnp.exp(m_st - mu)
            num = f_i * nd_intra[sl, :HEAD_DIM] + f_s * nd_inter[:, :HEAD_DIM]
            den = f_i * nd_intra[sl, HEAD_DIM:] + f_s * nd_inter[:, HEAD_DIM:]
            outs[gi] = num / jnp.maximum(jnp.abs(den), jnp.exp(-(b[sl] + mu)))
            tot, m_loc = tot3[gi], m_loc3[gi]
            m_new = jnp.maximum(tot + m_st, m_loc)
            st = twice(jnp.exp(tot + m_st - m_new)) * st + twice(jnp.exp(m_loc - m_new)) * cn_loc[gi]
            m_st = m_new
        return cat(outs), m_st, st

    m0 = jnp.full((1, HEAD_DIM), STAB_INIT, F32)

    def init():
        stf_ref[...] = jnp.zeros_like(stf_ref)
        stb_ref[...] = jnp.zeros_like(stb_ref)

    def step(i, carry):
        m_f, m_b = carry
        rf = pl.multiple_of(i * rows, rows)
        rb = pl.multiple_of((nc // grp - 1 - i) * rows, rows)
        vals_f = load(rf, stf_ref)
        vals_b = load(rb, stb_ref)
        out_f, m_f, st_f = group(vals_f, 0, m_f)
        out_b, m_b, st_b = group(vals_b, 1, m_b)

        def commit():
            hf_ref[pl.ds(rf, rows), :] = out_f
            hb_ref[pl.ds(rb, rows), :] = out_b
            stf_ref[...] = st_f
            stb_ref[...] = st_b

        return (m_f, m_b), commit

    def fin(c):
        r0 = pl.multiple_of(c * rows, rows)
        y = _rms(hf_ref[pl.ds(r0, rows), :] + hb_ref[pl.ds(r0, rows), :], ng_ref[...])
        og = jax.nn.sigmoid(og_ref[pl.ds(r0, rows), :].astype(F32))
        o_ref[pl.ds(r0, rows), :] = (og * y).astype(o_ref.dtype)

    return nc // grp, _Steps(init, step, fin, (m0, m0))


def _mlstm_body(*refs):
    n, part = _mlstm_steps(*refs)
    _run_steps(n, part, 1)


def _block_ref(b, m, rev):
    c, w = b.shape
    parts = []
    for blk in range(c // (2 * m)):
        row = blk * 2 * m + (m if rev else m - 1)
        parts.append(jnp.broadcast_to(b[row:row + 1, :], (2 * m, w)))
    return parts[0] if len(parts) == 1 else jnp.concatenate(parts, axis=0)


def _hgrn_steps(layer, q_ref, ff_ref, fb_ref, i_ref, g_ref, lbl_ref, ng_ref, o_ref, of_ref, ob_ref, stf_ref,
                stb_ref):
    s = q_ref.shape[0]
    c = min(HGRN_CHUNK, s)
    nc = s // c
    band = min(HGRN_BAND, c)
    if layer > 0:
        lg = lbl_ref[...]
        e = jnp.exp(lg - jnp.max(lg, axis=0, keepdims=True))
        p = e / jnp.sum(e, axis=0, keepdims=True)
        lb = p[1:2, :]
        for r in range(2, layer + 1):
            lb = lb + p[r:r + 1, :]
        log_lb = jnp.log(lb)
        log_1m = jnp.log(1.0 - lb)
        one_m = 1.0 - lb

    grp = math.gcd(nc, MIXER_GROUP)
    rows = grp * c
    ri = lax.broadcasted_iota(jnp.int32, (c, c), 0)
    ci = lax.broadcasted_iota(jnp.int32, (c, c), 1)
    tris = ((ri >= ci).astype(BF16), (ri <= ci).astype(BF16))
    ri = lax.broadcasted_iota(jnp.int32, (rows, c), 0) & (c - 1)
    ci = lax.broadcasted_iota(jnp.int32, (rows, c), 1)
    rw = lax.broadcasted_iota(jnp.int32, (rows, HEAD_DIM), 0)
    pos = lax.broadcasted_iota(jnp.int32, (rows, 1), 0) & (band - 1)
    levels = []
    m = c // 2
    while m >= band:
        levels.append(m)
        m //= 2
    upper = {m: (rw & (2 * m - 1)) >= m for m in levels}
    same_blk = {m: (ri >> int(math.log2(2 * m))) == (ci >> int(math.log2(2 * m))) for m in levels if 2 * m < c}
    diags = ([ri - ci == dl for dl in range(band)], [ci - ri == dl for dl in range(band)])
    per_chunk = lambda x: [x[g * c:(g + 1) * c] for g in range(grp)]
    cat = lambda xs: jnp.concatenate(xs, axis=0)
    chunks3 = lambda x: x.reshape(grp, c, x.shape[-1])

    def group(r0, rev, st_ref):
        f = (fb_ref if rev else ff_ref)[pl.ds(r0, rows), :]
        e_f = jnp.exp(-jnp.abs(f))
        den = 1.0 + e_f
        lf = jnp.minimum(f, 0.0) - jnp.log(den)
        kd = jnp.where(f >= 0.0, e_f, 1.0) / den
        if layer > 0:
            x = log_1m + lf
            mx = jnp.maximum(log_lb, x)
            lf = mx + jnp.log(jnp.exp(log_lb - mx) + jnp.exp(x - mx))
            kd = one_m * kd
        b = cat([_cumsum_mm(tris[rev], x) for x in per_chunk(lf * LOG2_E)])
        tot3 = chunks3(b)[:, 0:1] if rev else chunks3(b)[:, c - 1:c]
        qs = jax.nn.silu(q_ref[pl.ds(r0, rows), :].astype(F32))
        v = i_ref[pl.ds(r0, rows), :]
        q_in = per_chunk((qs * jnp.exp2(b)).astype(BF16))
        k_out = per_chunk((chunks3(kd) * jnp.exp2(tot3 - chunks3(b))).astype(BF16).reshape(rows, HEAD_DIM))
        kv = [_dot_tn(x, y) for x, y in zip(per_chunk(v), k_out)]
        amat = None
        for m in levels:
            bref = _block_ref(b, m, rev)
            up = upper[m]
            qm = jnp.logical_not(up) if rev else up
            km = up if rev else jnp.logical_not(up)
            qt = (qs * jnp.exp2(jnp.where(qm, b - bref, -jnp.inf))).astype(BF16)
            kt = (kd * jnp.exp2(jnp.where(km, bref - b, -jnp.inf))).astype(BF16)
            am = cat([_dot_nt(x, y) for x, y in zip(per_chunk(qt), per_chunk(kt))])
            if m in same_blk:
                am = jnp.where(same_blk[m], am, 0.0)
            amat = am if amat is None else amat + am
        for dl in range(band):
            if dl == 0:
                a = jnp.sum(qs * kd, axis=1, keepdims=True)
            else:
                sh = (rows - dl) if rev else dl
                valid = (pos + dl < band) if rev else (pos >= dl)
                ex = jnp.exp2(b - pltpu.roll(b, sh, 0))
                a = jnp.where(valid, jnp.sum(qs * pltpu.roll(kd, sh, 0) * ex, axis=1, keepdims=True), 0.0)
            term = jnp.where(diags[rev][dl], a, 0.0)
            amat = term if amat is None else amat + term
        o_intra = [_dot(x, y) for x, y in zip(per_chunk(amat.astype(BF16)), per_chunk(v))]
        st = st_ref[...]
        outs = [None] * grp
        for gi in (range(grp - 1, -1, -1) if rev else range(grp)):
            outs[gi] = o_intra[gi] + _dot_nt(q_in[gi], st.astype(BF16))
            st = jnp.exp2(tot3[gi]) * st + kv[gi]
        return cat(outs), st

    def init():
        stf_ref[...] = jnp.zeros_like(stf_ref)
        stb_ref[...] = jnp.zeros_like(stb_ref)

    def step(i, carry):
        rf = pl.multiple_of(i * rows, rows)
        rb = pl.multiple_of((nc // grp - 1 - i) * rows, rows)
        out_f, st_f = group(rf, 0, stf_ref)
        out_b, st_b = group(rb, 1, stb_ref)

        def commit():
            of_ref[pl.ds(rf, rows), :] = out_f
            ob_ref[pl.ds(rb, rows), :] = out_b
            stf_ref[...] = st_f
            stb_ref[...] = st_b

        return carry, commit

    def fin(ci_):
        r0 = pl.multiple_of(ci_ * rows, rows)
        hs = of_ref[pl.ds(r0, rows), :] + ob_ref[pl.ds(r0, rows), :]
        gate = jax.nn.sigmoid(g_ref[pl.ds(r0, rows), :].astype(F32))
        o_ref[pl.ds(r0, rows), :] = (gate * _rms(hs, ng_ref[...])).astype(o_ref.dtype)

    return nc // grp, _Steps(init, step, fin, jnp.int32(0))


def _hgrn_body(layer, *refs):
    n, part = _hgrn_steps(layer, *refs)
    _run_steps(n, part, 1)


def _head_blk(s, off):
    return pl.BlockSpec((s, HEAD_DIM), lambda b, h: (b, off + h))


def _head_row(rows):
    return pl.BlockSpec((rows, HEAD_DIM), lambda b, h: (0, h))


def _mlstm_call_parts(pbf, pfp, lp, s):
    cb, cg = BF_B // HEAD_DIM, FP_GATE // HEAD_DIM
    specs = [_head_blk(s, cb), _head_blk(s, cb + 4), _head_blk(s, cb + 8), _head_blk(s, cb + 12),
             _head_blk(s, cg), _head_row(1)]
    scratch = [pltpu.VMEM((s, HEAD_DIM), F32)] * 2 + [pltpu.VMEM((HEAD_DIM, 2 * HEAD_DIM), F32)] * 2
    return specs, (pbf, pbf, pbf, pbf, pfp, lp["mlstm_norm"]), scratch


def _hgrn_call_parts(pbf, pfp, lp, lb_logits, s):
    cd = BF_D3 // HEAD_DIM
    specs = [_head_blk(s, cd), _head_blk(s, FP_FF // HEAD_DIM), _head_blk(s, FP_FB // HEAD_DIM),
             _head_blk(s, cd + 4), _head_blk(s, cd + 8), _head_row(lb_logits.shape[0]), _head_row(1)]
    scratch = [pltpu.VMEM((s, HEAD_DIM), F32)] * 2 + [pltpu.VMEM((HEAD_DIM, HEAD_DIM), F32)] * 2
    return specs, (pbf, pfp, pfp, pbf, pbf, lb_logits, lp["hgrn_norm"]), scratch


def _mlstm_hgrn(pbf, pfp, lp, lb_logits, layer, bsz, s):
    t = pbf.shape[0]
    m_specs, m_args, m_scr = _mlstm_call_parts(pbf, pfp, lp, s)
    h_specs, h_args, h_scr = _hgrn_call_parts(pbf, pfp, lp, lb_logits, s)
    out_spec = _head_blk(s, 0)
    out_shape = jax.ShapeDtypeStruct((t, MIX_W), BF16)
    m_out = pl.pallas_call(
        _mlstm_body, grid=(bsz, N_HEADS), in_specs=m_specs, out_specs=out_spec, out_shape=out_shape,
        scratch_shapes=m_scr, compiler_params=_cp(("parallel", "parallel")), name="mlstm")(*m_args)
    h_out = pl.pallas_call(
        functools.partial(_hgrn_body, layer), grid=(bsz, N_HEADS), in_specs=h_specs, out_specs=out_spec,
        out_shape=out_shape, scratch_shapes=h_scr, compiler_params=_cp(("parallel", "parallel")),
        name="hgrn2")(*h_args)
    return m_out, h_out


def _shift3(cat, w, bias, ts):
    n = ts + 2 * HALO
    return (bias + w[0:1] * pltpu.roll(cat, 1, 0)[HALO:HALO + ts] + w[1:2] * cat[HALO:HALO + ts]
            + w[2:3] * pltpu.roll(cat, n - 1, 0)[HALO:HALO + ts])


def _hyena_prep_body(x_ref, xp_ref, xn_ref, cw_ref, cb_ref, z_ref, x1_ref):
    i = pl.program_id(1)
    ts = x_ref.shape[0]
    prev = jnp.where(i > 0, xp_ref[...].astype(F32), 0.0)
    nxt = jnp.where(i < pl.num_programs(1) - 1, xn_ref[...].astype(F32), 0.0)
    cat = jnp.concatenate([prev, x_ref[...].astype(F32), nxt], axis=0)
    u = _shift3(cat, cw_ref[...], cb_ref[...], ts)
    z_ref[...] = (u[:, 2 * MIX_W:] * u[:, :MIX_W]).astype(z_ref.dtype)
    x1_ref[...] = u[:, MIX_W:2 * MIX_W].astype(x1_ref.dtype)


def _hyena_prep(pbf, lp, bsz, s):
    t = pbf.shape[0]
    ts = min(512, s)
    per = s // ts
    hb = ts // HALO
    nb = t // HALO
    w = 3 * MIX_W
    return pl.pallas_call(
        _hyena_prep_body,
        grid=(bsz, per),
        in_specs=[
            pl.BlockSpec((ts, w), lambda b, i: (b * per + i, 0)),
            pl.BlockSpec((HALO, w), lambda b, i: (jnp.maximum((b * per + i) * hb - 1, 0), 0)),
            pl.BlockSpec((HALO, w), lambda b, i: (jnp.minimum((b * per + i + 1) * hb, nb - 1), 0)),
            pl.BlockSpec((3, w), lambda b, i: (0, 0)),
            pl.BlockSpec((1, w), lambda b, i: (0, 0)),
        ],
        out_specs=[pl.BlockSpec((ts, MIX_W), lambda b, i: (b * per + i, 0)),
                   pl.BlockSpec((ts, MIX_W), lambda b, i: (b * per + i, 0))],
        out_shape=[jax.ShapeDtypeStruct((t, MIX_W), BF16), jax.ShapeDtypeStruct((t, MIX_W), BF16)],
        compiler_params=_cp(("parallel", "arbitrary")),
        name="hyena_prep",
    )(pbf, pbf, pbf, lp["hyena_conv_w"], lp["hyena_conv_b"])


def _dft1_body(prec, x_ref, f_ref, o_ref):
    n1h, r, c = x_ref.shape
    x = x_ref[...].reshape(n1h * r, c)
    o = jnp.dot(f_ref[...], x, preferred_element_type=F32, precision=prec)
    o_ref[...] = o.reshape(o_ref.shape).astype(o_ref.dtype)


def _dft1(x4, kron1, out_dtype, prec=None):
    bsz, n1h, n2, c = x4.shape
    r = kron1.shape[1] // n1h
    n1 = kron1.shape[0] // (2 * r)
    return pl.pallas_call(
        functools.partial(_dft1_body, prec),
        grid=(bsz, n2 // r),
        in_specs=[pl.BlockSpec((None, n1h, r, c), lambda b, j: (b, 0, j, 0)),
                  pl.BlockSpec(kron1.shape, lambda b, j: (0, 0))],
        out_specs=pl.BlockSpec((None, 2, n1, r, c), lambda b, j: (b, 0, 0, j, 0)),
        out_shape=jax.ShapeDtypeStruct((bsz, 2, n1, n2, c), out_dtype),
        compiler_params=_cp(("parallel", "parallel")),
        name="dft_stage1",
    )(x4, kron1)


def _dft2_conv_body(kb, a_ref, f_ref, fi_ref, g_ref, o_ref):
    n2 = a_ref.shape[2]
    for kk in range(kb):
        a = jnp.concatenate([a_ref[0, kk], a_ref[1, kk]], axis=0)
        x = _dot(f_ref[kk], a)
        xr, xi = x[:n2], x[n2:]
        gr, gi = g_ref[kk, 0], g_ref[kk, 1]
        y = jnp.concatenate([xr * gr - xi * gi, xr * gi + xi * gr], axis=0).astype(BF16)
        bq = _dot(fi_ref[kk], y)
        o_ref[0, kk] = bq[:n2].astype(o_ref.dtype)
        o_ref[1, kk] = bq[n2:].astype(o_ref.dtype)


def _dft2_conv(a5, f2, f2i, gspec, kb):
    bsz, _, n1, n2, c = a5.shape
    return pl.pallas_call(
        functools.partial(_dft2_conv_body, kb),
        grid=(n1 // kb, bsz),
        in_specs=[pl.BlockSpec((None, 2, kb, n2, c), lambda k, b: (b, 0, k, 0, 0)),
                  pl.BlockSpec((kb, 2 * n2, 2 * n2), lambda k, b: (k, 0, 0)),
                  pl.BlockSpec((kb, 2 * n2, 2 * n2), lambda k, b: (k, 0, 0)),
                  pl.BlockSpec((kb, 2, n2, c), lambda k, b: (k, 0, 0, 0))],
        out_specs=pl.BlockSpec((None, 2, kb, n2, c), lambda k, b: (b, 0, k, 0, 0)),
        out_shape=jax.ShapeDtypeStruct(a5.shape, BF16),
        compiler_params=_cp(("parallel", "arbitrary")),
        name="dft_stage2_conv",
    )(a5, f2, f2i, gspec)


def _dft2_filter_body(kb, a_ref, f_ref, o_ref):
    n2 = a_ref.shape[3]
    k0 = pl.program_id(0) * kb
    for kk in range(kb):
        sgn = (1 - 2 * ((k0 + kk) & 1)).astype(F32)
        xs = []
        for e in range(2):
            a = jnp.concatenate([a_ref[e, 0, kk], a_ref[e, 1, kk]], axis=0)
            xs.append(jnp.dot(f_ref[kk], a, preferred_element_type=F32, precision=HIGHEST))
        x = xs[0] + sgn * xs[1]
        o_ref[kk, 0] = x[:n2]
        o_ref[kk, 1] = x[n2:]


def _dft2_filter(a5, f2, kb):
    _, _, n1, n2, c = a5.shape
    return pl.pallas_call(
        functools.partial(_dft2_filter_body, kb),
        grid=(n1 // kb,),
        in_specs=[pl.BlockSpec((2, 2, kb, n2, c), lambda k: (0, 0, k, 0, 0)),
                  pl.BlockSpec((kb, 2 * n2, 2 * n2), lambda k: (k, 0, 0))],
        out_specs=pl.BlockSpec((kb, 2, n2, c), lambda k: (k, 0, 0, 0)),
        out_shape=jax.ShapeDtypeStruct((n1, 2, n2, c), F32),
        compiler_params=_cp(("parallel",)),
        name="dft_stage2_filter",
    )(a5, f2)


def _dft3_body(b_ref, f_ref, x1_ref, z_ref, sk_ref, o_ref):
    _, n1, r, c = b_ref.shape
    y = _dot(f_ref[...], b_ref[...].reshape(2 * n1 * r, c)).reshape(o_ref.shape)
    z = z_ref[...].astype(F32)
    o_ref[...] = (x1_ref[...].astype(F32) * (y + sk_ref[...] * z)).astype(o_ref.dtype)


def _dft3(b5, kron3, x1_4, z4, skip):
    bsz, _, n1, n2, c = b5.shape
    n1h = n1 // 2
    r = kron3.shape[0] // n1h
    row = pl.BlockSpec((None, n1h, r, c), lambda b, j: (b, 0, j, 0))
    return pl.pallas_call(
        _dft3_body,
        grid=(bsz, n2 // r),
        in_specs=[pl.BlockSpec((None, 2, n1, r, c), lambda b, j: (b, 0, 0, j, 0)),
                  pl.BlockSpec(kron3.shape, lambda b, j: (0, 0)),
                  row, row,
                  pl.BlockSpec((1, c), lambda b, j: (0, 0))],
        out_specs=row,
        out_shape=jax.ShapeDtypeStruct((bsz, n1h, n2, c), BF16),
        compiler_params=_cp(("parallel", "parallel")),
        name="dft_stage3",
    )(b5, kron3, x1_4, z4, skip)


def _hyena_filter_body(seq_len, pos_ref, sc_ref, bands_ref, w1t_ref, w1c_ref, w1s_ref, b1_ref, fr1_ref,
                       w2_ref, b2_ref, fr2_ref, w3_ref, rate_ref, o_ref):
    pos = pos_ref[...]
    t = pos * (1.0 / (seq_len - 1))
    arg = (pos * (2.0 * math.pi / seq_len)) * bands_ref[...]
    hdot = lambda a, b: jnp.dot(a, b, preferred_element_type=F32, precision=HIGHEST)
    pre = t * w1t_ref[...] + hdot(jnp.cos(arg), w1c_ref[...]) - hdot(jnp.sin(arg), w1s_ref[...]) + b1_ref[...]
    hid = jnp.sin(fr1_ref[...] * pre)
    hid = jnp.sin(fr2_ref[...] * (hdot(hid, w2_ref[...]) + b2_ref[...]))
    o_ref[...] = hdot(hid, w3_ref[...]) * jnp.exp(-t * rate_ref[...]) * sc_ref[...]


def _hyena_filter(lp, s):
    ts = min(256, s)
    r = 2 * s + ts
    pos = np.concatenate([np.arange(s), s - np.arange(s), np.zeros(ts)]).astype(np.float32)[:, None]
    sc = np.ones((r, 1), np.float32)
    sc[s] = 0.0
    pad = HEAD_DIM - HYENA_HID
    bands = np.zeros((1, HEAD_DIM), np.float32)
    bands[0, :HYENA_BANDS] = np.linspace(1e-4, HYENA_BANDS - 1, HYENA_BANDS)
    rate = np.abs(np.linspace(math.log(HYENA_TARGET) / HYENA_FAST, math.log(HYENA_TARGET) / HYENA_SLOW, MIX_W))
    rate = np.tile(rate, 2).astype(np.float32)[None, :]
    w1 = lp["hyena_w1"]
    padc = lambda a: jnp.pad(a, ((0, 0), (0, pad)))
    w1t = padc(w1[0:1])
    w1c = jnp.pad(w1[1:1 + HYENA_BANDS], ((0, HEAD_DIM - HYENA_BANDS), (0, pad)))
    w1s = jnp.pad(w1[1 + HYENA_BANDS:], ((0, HEAD_DIM - HYENA_BANDS), (0, pad)))
    w2 = jnp.pad(lp["hyena_w2"], ((0, pad), (0, pad)))
    w3 = jnp.pad(lp["hyena_w3"], ((0, pad), (0, 0)))
    vec = lambda a: padc(a[None, :])
    full = lambda a: pl.BlockSpec(a.shape, lambda i: (0, 0))
    args = [jnp.asarray(bands), w1t, w1c, w1s, vec(lp["hyena_b1"]), vec(lp["hyena_freq1"]), w2,
            vec(lp["hyena_b2"]), vec(lp["hyena_freq2"]), w3, jnp.asarray(rate)]
    out = pl.pallas_call(
        functools.partial(_hyena_filter_body, s),
        grid=(r // ts,),
        in_specs=[pl.BlockSpec((ts, 1), lambda i: (i, 0)), pl.BlockSpec((ts, 1), lambda i: (i, 0))]
                 + [full(a) for a in args],
        out_specs=pl.BlockSpec((ts, 2 * MIX_W), lambda i: (i, 0)),
        out_shape=jax.ShapeDtypeStruct((r, 2 * MIX_W), F32),
        compiler_params=_cp(("parallel",)),
        name="hyena_filter",
    )(jnp.asarray(pos), jnp.asarray(sc), *args)
    return out[:s, :MIX_W], out[s:2 * s, MIX_W:], out[2 * s:2 * s + 1, MIX_W:]


def _dft_tables(s):
    n = 2 * s
    n2 = min(DFT_N2, s // 8)
    n1 = n // n2
    n1h = n1 // 2
    two_pi = 2.0 * math.pi
    k1 = jnp.arange(n1, dtype=jnp.int32)
    a1 = (two_pi / n1) * ((k1[:, None] * k1[None, :n1h]) % n1).astype(F32)
    f1 = jnp.concatenate([jnp.cos(a1), -jnp.sin(a1)], axis=0)
    f1i = jnp.concatenate([jnp.cos(a1).T, -jnp.sin(a1).T], axis=1) * (1.0 / n)
    j = jnp.arange(n2, dtype=jnp.int32)
    ph = (j[None, :, None] * j[None, None, :] * n1 + j[None, None, :] * k1[:, None, None]) % n
    a2 = (two_pi / n) * ph.astype(F32)
    tr, ti = jnp.cos(a2), -jnp.sin(a2)
    f2 = jnp.concatenate([jnp.concatenate([tr, -ti], axis=2), jnp.concatenate([ti, tr], axis=2)], axis=1)
    kron = lambda m, r: jnp.kron(m, jnp.eye(r, dtype=F32))
    r32, r16 = min(SUBLANES, n2), min(2 * SUBLANES, n2)
    return dict(n1=n1, n2=n2, f2=f2, f2_bf=f2.astype(BF16), f2i_bf=jnp.swapaxes(f2, 1, 2).astype(BF16),
                kron1_f32=kron(f1, r32), kron1_bf=kron(f1, r16).astype(BF16), kron3_bf=kron(f1i, r16).astype(BF16))


def _hyena_spectrum(lp, s, tab):
    gpos, gneg, hb0 = _hyena_filter(lp, s)
    n1, n2 = tab["n1"], tab["n2"]
    a = _dft1(jnp.stack([gpos, gneg]).reshape(2, n1 // 2, n2, MIX_W), tab["kron1_f32"], F32, HIGHEST)
    return _dft2_filter(a, tab["f2"], min(2, n1)), hb0


def _hyena(pbf, lp, gspec, hb0, tab, bsz, s):
    n1, n2 = tab["n1"], tab["n2"]
    z, x1 = _hyena_prep(pbf, lp, bsz, s)
    z4 = z.reshape(bsz, n1 // 2, n2, MIX_W)
    a = _dft1(z4, tab["kron1_bf"], BF16)
    bq = _dft2_conv(a, tab["f2_bf"], tab["f2i_bf"], gspec, min(2, n1))
    out = _dft3(bq, tab["kron3_bf"], x1.reshape(z4.shape), z4, lp["hyena_skip"][None, :] + hb0)
    return out.reshape(bsz * s, MIX_W)


def _merge_body(h_ref, g0_ref, g1_ref, g2_ref, g3_ref, ba_ref, bb_ref, bc_ref, bd_ref, wb_ref, wo_ref, o_ref):
    merged = None
    for idx, (g_ref, br) in enumerate(zip((g0_ref, g1_ref, g2_ref, g3_ref), (ba_ref, bb_ref, bc_ref, bd_ref))):
        term = jax.nn.sigmoid(g_ref[...].astype(F32)) * _dot(br[...], wb_ref[idx])
        merged = term if merged is None else merged + term
    o_ref[...] = h_ref[...] + _dot(merged.astype(BF16), wo_ref[...])


def _merge(h, pbf, branches, lp):
    t, d = h.shape
    tm = min(512, t)
    gcol = BF_G // d
    row = lambda w: pl.BlockSpec((tm, w), lambda i: (i, 0))
    gate = lambda idx: pl.BlockSpec((tm, d), lambda i: (i, gcol + idx))
    return pl.pallas_call(
        _merge_body,
        grid=(t // tm,),
        in_specs=[row(d), gate(0), gate(1), gate(2), gate(3),
                  row(MIX_W), row(MIX_W), row(MIX_W), row(MIX_W),
                  pl.BlockSpec(lp["w_branch"].shape, lambda i: (0, 0, 0)),
                  pl.BlockSpec(lp["w_out"].shape, lambda i: (0, 0))],
        out_specs=row(d),
        out_shape=jax.ShapeDtypeStruct((t, d), F32),
        compiler_params=_cp(("parallel",)),
        name="merge",
    )(h, pbf, pbf, pbf, pbf, *branches, lp["w_branch"], lp["w_out"])


def _ffn_body(per, final, h_ref, hp_ref, hn_ref, p_ref, gf_ref, wu_ref, cw_ref, cb_ref, wd_ref, gp_ref, wg_ref,
              wp_ref, gl_ref, o_ref):
    i = pl.program_id(0)
    tm = h_ref.shape[0]
    dff = wd_ref.shape[0]
    gf = gf_ref[...]
    h = h_ref[...]
    first = (i % per) == 0
    last = (i % per) == per - 1
    xp = jnp.where(first, 0.0, _rms(hp_ref[...], gf))
    xn = jnp.where(last, 0.0, _rms(hn_ref[...], gf))
    cat = jnp.concatenate([xp, _rms(h, gf), xn], axis=0).astype(BF16)
    cb_w = min(512, dff)
    acc = jnp.zeros_like(h)
    for cb in range(dff // cb_w):
        lo, lo2 = cb * cb_w, dff + cb * cb_w
        u1 = _shift3(_dot(cat, wu_ref[:, lo:lo + cb_w]), cw_ref[:, lo:lo + cb_w], cb_ref[:, lo:lo + cb_w], tm)
        u2 = _shift3(_dot(cat, wu_ref[:, lo2:lo2 + cb_w]), cw_ref[:, lo2:lo2 + cb_w], cb_ref[:, lo2:lo2 + cb_w], tm)
        acc = acc + _dot((jax.nn.gelu(u1) * u2).astype(BF16), wd_ref[lo:lo + cb_w, :])
    h = h + acc
    gate = jax.nn.sigmoid(_dot(_rms(h, gp_ref[...]).astype(BF16), wg_ref[...]))
    h = h + gate * _dot(p_ref[...].astype(BF16), wp_ref[...])
    if final:
        h = _rms(h, gl_ref[...])
    o_ref[...] = h


def _ffn(h, p, lp, final_norm, s, final):
    t, d = h.shape
    tm = min(512, s)
    per = s // tm
    hb = tm // HALO
    nb = t // HALO
    full = lambda a: pl.BlockSpec(a.shape, lambda i: (0,) * a.ndim)
    row = lambda w: pl.BlockSpec((tm, w), lambda i: (i, 0))
    ws = [lp["norm_ffn"], lp["w_up"], lp["ffn_conv_w"], lp["ffn_conv_b"], lp["w_down"], lp["norm_ple"],
          lp["w_ple_gate"], lp["w_ple"], final_norm]
    return pl.pallas_call(
        functools.partial(_ffn_body, per, final),
        grid=(t // tm,),
        in_specs=[row(d),
                  pl.BlockSpec((HALO, d), lambda i: (jnp.maximum(i * hb - 1, 0), 0)),
                  pl.BlockSpec((HALO, d), lambda i: (jnp.minimum((i + 1) * hb, nb - 1), 0)),
                  row(p.shape[1])] + [full(a) for a in ws],
        out_specs=row(d),
        out_shape=jax.ShapeDtypeStruct((t, d), F32),
        compiler_params=_cp(("parallel",)),
        name="ffn_ple",
    )(h, h, h, p, *ws)


def _prepare_params(prm):
    d_model = prm["w_in"].shape[1]
    depth = prm["w_in"].shape[0]
    off_b = 2 * MIX_W
    off_gate = off_b + 4 * MIX_W
    off_c = off_gate + 4 * N_HEADS
    off_d = off_c + 3 * MIX_W
    off_g = off_d + 5 * MIX_W

    def regroup(a):
        sl = lambda lo, w: a[..., lo:lo + w]
        bf = jnp.concatenate([sl(off_c, 3 * MIX_W), sl(off_d, MIX_W), sl(off_d + 3 * MIX_W, 2 * MIX_W),
                              sl(off_g, 4 * d_model), sl(0, 2 * MIX_W), sl(off_b, 4 * MIX_W)], axis=-1)
        gates = sl(off_gate, 4 * N_HEADS).reshape(a.shape[:-1] + (4, N_HEADS))
        gates = jnp.swapaxes(gates, -1, -2)
        gates = jnp.pad(gates, [(0, 0)] * (gates.ndim - 1) + [(0, HEAD_DIM - 4)])
        fp = jnp.concatenate([sl(off_d + MIX_W, 2 * MIX_W), gates.reshape(a.shape[:-1] + (N_HEADS * HEAD_DIM,))],
                             axis=-1)
        return bf, fp

    w_bf, w_fp = regroup(prm["w_in"])
    b_bf, b_fp = regroup(prm["b_in"][:, None, :])
    gw = jnp.transpose(prm["rglru_w"], (0, 3, 4, 1, 2, 5)).reshape(depth, N_HEADS, HEAD_DIM, 4 * HEAD_DIM)
    gb = prm["rglru_b"].reshape(depth, 2, 2, N_HEADS, HEAD_DIM)
    gb = jnp.transpose(gb, (0, 3, 1, 2, 4)).reshape(depth, N_HEADS, 1, 4 * HEAD_DIM)
    row = lambda a: a[:, None, :]
    out = dict(
        norm_mix=row(prm["norm_mix"]), w_bf=w_bf.astype(BF16), b_bf=b_bf, w_fp=w_fp.astype(BF16), b_fp=b_fp,
        conv_a_w=prm["conv_a_w"], conv_a_b=row(prm["conv_a_b"]), rglru_w=gw.astype(BF16), rglru_b=gb,
        rglru_lam=prm["rglru_lam"], mlstm_norm=row(prm["mlstm_norm"]),
        hyena_conv_w=prm["hyena_conv_w"], hyena_conv_b=row(prm["hyena_conv_b"]),
        hyena_w1=prm["hyena_w1"], hyena_b1=prm["hyena_b1"], hyena_freq1=prm["hyena_freq1"],
        hyena_w2=prm["hyena_w2"], hyena_b2=prm["hyena_b2"], hyena_freq2=prm["hyena_freq2"],
        hyena_w3=prm["hyena_w3"], hyena_skip=prm["hyena_skip"], hgrn_norm=row(prm["hgrn_norm"]),
        w_branch=prm["w_branch"].astype(BF16), w_out=prm["w_out"].astype(BF16),
        norm_ffn=row(prm["norm_ffn"]), w_up=prm["w_up"].astype(BF16), ffn_conv_w=prm["ffn_conv_w"],
        ffn_conv_b=row(prm["ffn_conv_b"]), w_down=prm["w_down"].astype(BF16), norm_ple=row(prm["norm_ple"]),
        w_ple_gate=prm["w_ple_gate"].astype(BF16), w_ple=prm["w_ple"].astype(BF16),
    )
    return out


def _trunk(x, p, prm, lb_logits, final_norm):
    bsz, s, d = x.shape
    depth = p.shape[0]
    h = x.reshape(bsz * s, d)
    tab = _dft_tables(s)
    for i in range(depth):
        lp = {k: v[i] for k, v in prm.items()}
        pbf, pfp = _proj(h, lp["norm_mix"], lp["w_bf"], lp["b_bf"], lp["w_fp"], lp["b_fp"])
        gspec, hb0 = _hyena_spectrum(lp, s, tab)
        br_b, br_d = _mlstm_hgrn(pbf, pfp, lp, lb_logits, i, bsz, s)
        branches = (_rglru(pbf, lp, bsz, s), br_b, _hyena(pbf, lp, gspec, hb0, tab, bsz, s), br_d)
        h = _merge(h, pbf, branches, lp)
        h = _ffn(h, p[i].reshape(bsz * s, -1), lp, final_norm, s, i == depth - 1)
    return h.reshape(bsz, s, d)


def kernel(x_prompt, x_sample, p_prompt, p_sample, norm_mix, w_in, b_in, conv_a_w, conv_a_b, rglru_w, rglru_b, rglru_lam, mlstm_norm, hyena_conv_w, hyena_conv_b, hyena_w1, hyena_b1, hyena_freq1, hyena_w2, hyena_b2, hyena_freq2, hyena_w3, hyena_skip, hgrn_lb_logits, hgrn_norm, w_branch, w_out, norm_ffn, w_up, ffn_conv_w, ffn_conv_b, w_down, norm_ple, w_ple_gate, w_ple, final_norm):
    prm = _prepare_params(dict(
        norm_mix=norm_mix, w_in=w_in, b_in=b_in, conv_a_w=conv_a_w, conv_a_b=conv_a_b, rglru_w=rglru_w,
        rglru_b=rglru_b, rglru_lam=rglru_lam, mlstm_norm=mlstm_norm, hyena_conv_w=hyena_conv_w,
        hyena_conv_b=hyena_conv_b, hyena_w1=hyena_w1, hyena_b1=hyena_b1, hyena_freq1=hyena_freq1,
        hyena_w2=hyena_w2, hyena_b2=hyena_b2, hyena_freq2=hyena_freq2, hyena_w3=hyena_w3,
        hyena_skip=hyena_skip, hgrn_norm=hgrn_norm, w_branch=w_branch, w_out=w_out, norm_ffn=norm_ffn,
        w_up=w_up, ffn_conv_w=ffn_conv_w, ffn_conv_b=ffn_conv_b, w_down=w_down, norm_ple=norm_ple,
        w_ple_gate=w_ple_gate, w_ple=w_ple))
    fn = final_norm[None, :]
    y_prompt = _trunk(x_prompt, p_prompt, prm, hgrn_lb_logits, fn)
    y_sample = _trunk(x_sample, p_sample, prm, hgrn_lb_logits, fn)
    return (y_prompt, y_sample)
```

```python
import collections
import functools
import math

import numpy as np
import jax
import jax.numpy as jnp
from jax import lax
from jax.experimental import pallas as pl
from jax.experimental.pallas import tpu as pltpu

F32 = jnp.float32
BF16 = jnp.bfloat16
HIGHEST = lax.Precision.HIGHEST

N_HEADS = 4
HEAD_DIM = 128
MIX_W = N_HEADS * HEAD_DIM
LRU_C = 8.0
LOG2_E = math.log2(math.e)
EPS = 1e-6
STAB_INIT = -1e30
HYENA_BANDS = 16
HYENA_HID = 64
HYENA_FAST = 0.3
HYENA_SLOW = 1.5
HYENA_TARGET = 1e-2

MLSTM_CHUNK = 128
HGRN_CHUNK = 128
MIXER_GROUP = 4
HGRN_BAND = 4
SCAN_CHUNK = 512
HALO = 16
SUBLANES = 8
PROJ_COLS = 1280
DFT_N2 = 128
V7X_VMEM_LIMIT = 56 * 1024 * 1024

BF_C, BF_D3, BF_G, BF_A, BF_B = 0, 1536, 3072, 7168, 8192
BF_COLS = 10240
FP_FF, FP_FB, FP_GATE = 0, 512, 1024
FP_COLS = 1536


def _cp(sem, vmem=V7X_VMEM_LIMIT):
    return pltpu.CompilerParams(dimension_semantics=sem, vmem_limit_bytes=vmem)


def _rms(x, g):
    return x * lax.rsqrt(jnp.mean(x * x, axis=-1, keepdims=True) + EPS) * g


def _log_sigmoid(x):
    return jnp.minimum(x, 0.0) - jnp.log(1.0 + jnp.exp(-jnp.abs(x)))


def _dot(a, b):
    return jnp.dot(a, b, preferred_element_type=F32)


def _dot_nt(a, b):
    return lax.dot_general(a, b, (((1,), (1,)), ((), ())), preferred_element_type=F32)


def _dot_tn(a, b):
    return lax.dot_general(a, b, (((0,), (0,)), ((), ())), preferred_element_type=F32)


def _cumsum_mm(tri, x):
    hi = x.astype(BF16)
    lo = (x - hi.astype(F32)).astype(BF16)
    return _dot(tri, hi) + _dot(tri, lo)


def _proj_body(h_ref, g_ref, wb_ref, bb_ref, wf_ref, bf_ref, ob_ref, of_ref):
    xn = _rms(h_ref[...], g_ref[...]).astype(BF16)
    for w_ref, b_ref, o_ref in ((wb_ref, bb_ref, ob_ref), (wf_ref, bf_ref, of_ref)):
        n = w_ref.shape[1]
        tn = math.gcd(PROJ_COLS, n)
        for lo in range(0, n, tn):
            o_ref[:, lo:lo + tn] = (_dot(xn, w_ref[:, lo:lo + tn]) + b_ref[:, lo:lo + tn]).astype(o_ref.dtype)


def _proj(h, g, w_bf, b_bf, w_fp, b_fp):
    t, d = h.shape
    tm = min(256, t)
    once = lambda a: pl.BlockSpec(a.shape, lambda i: (0, 0), pipeline_mode=pl.Buffered(1))
    row = lambda n: pl.BlockSpec((tm, n), lambda i: (i, 0))
    return pl.pallas_call(
        _proj_body,
        grid=(t // tm,),
        in_specs=[row(d), once(g), once(w_bf), once(b_bf), once(w_fp), once(b_fp)],
        out_specs=[row(w_bf.shape[1]), row(w_fp.shape[1])],
        out_shape=[jax.ShapeDtypeStruct((t, w_bf.shape[1]), BF16), jax.ShapeDtypeStruct((t, w_fp.shape[1]), F32)],
        compiler_params=_cp(("parallel",)),
        name="in_proj",
    )(h, g, w_bf, b_bf, w_fp, b_fp)


def _scan8(a, u, pos, reverse):
    n = a.shape[0]
    d = 1
    while d < SUBLANES:
        if reverse:
            m = pos < SUBLANES - d
            sh = n - d
        else:
            m = pos >= d
            sh = d
        a_s = jnp.where(m, pltpu.roll(a, sh, 0), 1.0)
        u_s = jnp.where(m, pltpu.roll(u, sh, 0), 0.0)
        u = a * u_s + u
        a = a * a_s
        d *= 2
    return a, u


def _chain8(a, u, carry, reverse):
    groups = a.shape[0] // SUBLANES
    outs = [None] * groups
    for gi in (range(groups - 1, -1, -1) if reverse else range(groups)):
        lo = gi * SUBLANES
        h = u[lo:lo + SUBLANES] + a[lo:lo + SUBLANES] * carry
        outs[gi] = h
        carry = h[0:1] if reverse else h[SUBLANES - 1:SUBLANES]
    return jnp.concatenate(outs, axis=0), carry


def _rglru_body(xa_ref, ya_ref, cw_ref, cb_ref, gw_ref, gb_ref, lam_ref, o_ref, xc_ref, hf_ref, hb_ref):
    s = xa_ref.shape[0]
    tc = min(SCAN_CHUNK, s)
    nc = s // tc
    n = tc + 2 * HALO
    pos = lax.broadcasted_iota(jnp.int32, (tc, HEAD_DIM), 0) & (SUBLANES - 1)
    cw = cw_ref[...]
    cb = cb_ref[...]
    sp = jax.nn.softplus(-lam_ref[...])

    def gates(xc, d):
        w = gw_ref[0, :, d * 256:(d + 1) * 256]
        g = jax.nn.sigmoid(_dot(xc.astype(BF16), w) + gb_ref[0, :, d * 256:(d + 1) * 256])
        log_a = (-LRU_C) * g[:, :HEAD_DIM] * sp[d:d + 1]
        a = jnp.exp(log_a)
        x2 = 2.0 * log_a
        one_m_a2 = jnp.where(x2 > -0.01, -x2 * (1.0 + 0.5 * x2 * (1.0 + x2 * (1.0 / 3.0))), 1.0 - a * a)
        u = jnp.sqrt(one_m_a2) * (g[:, HEAD_DIM:] * xc)
        return a, u

    def conv(c, carry):
        r0 = pl.multiple_of(c * tc, tc)
        x = xa_ref[pl.ds(r0, tc), :].astype(F32)
        rp = pl.multiple_of(jnp.maximum(r0 - HALO, 0), HALO)
        rn = pl.multiple_of(jnp.minimum(r0 + tc, s - HALO), HALO)
        prev = jnp.where(c > 0, xa_ref[pl.ds(rp, HALO), :].astype(F32), 0.0)
        nxt = jnp.where(c < nc - 1, xa_ref[pl.ds(rn, HALO), :].astype(F32), 0.0)
        cat = jnp.concatenate([prev, x, nxt], axis=0)
        xc_ref[pl.ds(r0, tc), :] = (
            cb + cw[0:1] * pltpu.roll(cat, 1, 0)[HALO:HALO + tc] + cw[1:2] * x
            + cw[2:3] * pltpu.roll(cat, n - 1, 0)[HALO:HALO + tc]
            + cw[3:4] * pltpu.roll(cat, n - 2, 0)[HALO:HALO + tc])
        return carry

    lax.fori_loop(0, nc, conv, 0)

    def scan(i, carry):
        c_f, c_b = carry
        rf = pl.multiple_of(i * tc, tc)
        rb = pl.multiple_of((nc - 1 - i) * tc, tc)
        a, u = gates(xc_ref[pl.ds(rf, tc), :], 0)
        a, u = _scan8(a, u, pos, False)
        h, c_f = _chain8(a, u, c_f, False)
        hf_ref[pl.ds(rf, tc), :] = h
        a, u = gates(xc_ref[pl.ds(rb, tc), :], 1)
        a, u = _scan8(a, u, pos, True)
        h, c_b = _chain8(a, u, c_b, True)
        hb_ref[pl.ds(rb, tc), :] = h
        return c_f, c_b

    zero = jnp.zeros((1, HEAD_DIM), F32)
    lax.fori_loop(0, nc, scan, (zero, zero))

    def fin(c, carry):
        r0 = pl.multiple_of(c * tc, tc)
        y = jax.nn.gelu(ya_ref[pl.ds(r0, tc), :].astype(F32))
        o_ref[pl.ds(r0, tc), :] = ((hf_ref[pl.ds(r0, tc), :] + hb_ref[pl.ds(r0, tc), :]) * y).astype(o_ref.dtype)
        return carry

    lax.fori_loop(0, nc, fin, 0)


def _rglru(pbf, lp, bsz, s):
    t = pbf.shape[0]
    ca, cy = BF_A // HEAD_DIM, (BF_A + MIX_W) // HEAD_DIM
    return pl.pallas_call(
        _rglru_body,
        grid=(bsz, N_HEADS),
        in_specs=[
            pl.BlockSpec((s, HEAD_DIM), lambda b, h: (b, ca + h)),
            pl.BlockSpec((s, HEAD_DIM), lambda b, h: (b, cy + h)),
            pl.BlockSpec((4, HEAD_DIM), lambda b, h: (0, h)),
            pl.BlockSpec((1, HEAD_DIM), lambda b, h: (0, h)),
            pl.BlockSpec((1, HEAD_DIM, 4 * HEAD_DIM), lambda b, h: (h, 0, 0)),
            pl.BlockSpec((1, 1, 4 * HEAD_DIM), lambda b, h: (h, 0, 0)),
            pl.BlockSpec((2, HEAD_DIM), lambda b, h: (0, h)),
        ],
        out_specs=pl.BlockSpec((s, HEAD_DIM), lambda b, h: (b, h)),
        out_shape=jax.ShapeDtypeStruct((t, MIX_W), BF16),
        scratch_shapes=[pltpu.VMEM((s, HEAD_DIM), F32)] * 3,
        compiler_params=_cp(("parallel", "parallel")),
        name="rglru",
    )(pbf, pbf, lp["conv_a_w"], lp["conv_a_b"], lp["rglru_w"], lp["rglru_b"], lp["rglru_lam"])


_Steps = collections.namedtuple("_Steps", "init step fin carry0")


def _run_steps(n, part, unroll):
    part.init()

    def body(i, carry):
        carry, commit = part.step(i, carry)
        commit()
        return carry

    lax.fori_loop(0, n, body, part.carry0, unroll=unroll)

    def fin(c, carry):
        part.fin(c)
        return carry

    lax.fori_loop(0, n, fin, 0)


def _mlstm_steps(q_ref, k_ref, v_ref, og_ref, g_ref, ng_ref, o_ref, hf_ref, hb_ref, stf_ref, stb_ref):
    s = q_ref.shape[0]
    cl = min(MLSTM_CHUNK, s)
    nc = s // cl
    ri = lax.broadcasted_iota(jnp.int32, (cl, cl), 0)
    ci = lax.broadcasted_iota(jnp.int32, (cl, cl), 1)
    masks = (ri >= ci, ri <= ci)
    tris = (masks[0].astype(BF16), masks[1].astype(BF16))
    grp = math.gcd(nc, MIXER_GROUP)
    rows = grp * cl
    mask_rows = tuple(jnp.concatenate([m] * grp, axis=0) for m in masks)
    ones_v = jnp.ones((rows, HEAD_DIM), BF16)
    kscale = HEAD_DIM ** -0.5
    twice = lambda a: jnp.concatenate([a, a], axis=1)
    per_chunk = lambda x: [x[g * cl:(g + 1) * cl] for g in range(grp)]
    cat = lambda xs: jnp.concatenate(xs, axis=0)
    chunks3 = lambda x: x.reshape(grp, cl, x.shape[-1])

    def load(r0, st_ref):
        return (g_ref[pl.ds(r0, rows), :], q_ref[pl.ds(r0, rows), :], k_ref[pl.ds(r0, rows), :],
                v_ref[pl.ds(r0, rows), :], st_ref[...])

    def group(vals, d, m_st):
        g, q, k, v, st = vals
        ig = jnp.broadcast_to(g[:, 2 * d:2 * d + 1], (rows, HEAD_DIM))
        lf = _log_sigmoid(jnp.broadcast_to(g[:, 2 * d + 1:2 * d + 2], (rows, HEAD_DIM)))
        b = cat([_cumsum_mm(tris[d], x) for x in per_chunk(lf)])
        tot3 = chunks3(b)[:, 0:1] if d else chunks3(b)[:, cl - 1:cl]
        c = ig - b
        lw3 = tot3 + chunks3(c)
        m_loc3 = jnp.max(lw3, axis=1, keepdims=True)
        kf = k.astype(F32) * kscale
        kw = (chunks3(kf) * jnp.exp(lw3 - m_loc3)).astype(BF16).reshape(rows, HEAD_DIM)
        kb = kf.astype(BF16)
        v1 = jnp.concatenate([v, ones_v], axis=1)
        cn_loc = [_dot_tn(a, w) for a, w in zip(per_chunk(kw), per_chunk(v1))]
        dm = jnp.where(mask_rows[d], cat([x.T for x in per_chunk(c)]), -jnp.inf)
        m_rel = jnp.max(dm, axis=1, keepdims=True)
        qk = cat([_dot_nt(a, w) for a, w in zip(per_chunk(q), per_chunk(kb))])
        am = (jnp.exp(dm - m_rel) * qk).astype(BF16)
        nd_intra = cat([_dot(a, w) for a, w in zip(per_chunk(am), per_chunk(v1))])
        outs = [None] * grp
        for gi in (range(grp - 1, -1, -1) if d else range(grp)):
            sl = slice(gi * cl, (gi + 1) * cl)
            nd_inter = _dot(q[sl], st.astype(BF16))
            mu = jnp.maximum(m_rel[sl], m_st)
            f_i = jnp.exp(m_rel[sl] - mu)
            f_s = jnp.exp(m_st - mu)
            num = f_i * nd_intra[sl, :HEAD_DIM] + f_s * nd_inter[:, :HEAD_DIM]
            den = f_i * nd_intra[sl, HEAD_DIM:] + f_s * nd_inter[:, HEAD_DIM:]
            outs[gi] = num / jnp.maximum(jnp.abs(den), jnp.exp(-(b[sl] + mu)))
            tot, m_loc = tot3[gi], m_loc3[gi]
            m_new = jnp.maximum(tot + m_st, m_loc)
            st = twice(jnp.exp(tot + m_st - m_new)) * st + twice(jnp.exp(m_loc - m_new)) * cn_loc[gi]
            m_st = m_new
        return cat(outs), m_st, st

    m0 = jnp.full((1, HEAD_DIM), STAB_INIT, F32)

    def init():
        stf_ref[...] = jnp.zeros_like(stf_ref)
        stb_ref[...] = jnp.zeros_like(stb_ref)

    def step(i, carry):
        m_f, m_b = carry
        rf = pl.multiple_of(i * rows, rows)
        rb = pl.multiple_of((nc // grp - 1 - i) * rows, rows)
        vals_f = load(rf, stf_ref)
        vals_b = load(rb, stb_ref)
        out_f, m_f, st_f = group(vals_f, 0, m_f)
        out_b, m_b, st_b = group(vals_b, 1, m_b)

        def commit():
            hf_ref[pl.ds(rf, rows), :] = out_f
            hb_ref[pl.ds(rb, rows), :] = out_b
            stf_ref[...] = st_f
            stb_ref[...] = st_b

        return (m_f, m_b), commit

    def fin(c):
        r0 = pl.multiple_of(c * rows, rows)
        y = _rms(hf_ref[pl.ds(r0, rows), :] + hb_ref[pl.ds(r0, rows), :], ng_ref[...])
        og = jax.nn.sigmoid(og_ref[pl.ds(r0, rows), :].astype(F32))
        o_ref[pl.ds(r0, rows), :] = (og * y).astype(o_ref.dtype)

    return nc // grp, _Steps(init, step, fin, (m0, m0))


def _mlstm_body(*refs):
    n, part = _mlstm_steps(*refs)
    _run_steps(n, part, 1)


def _block_ref(b, m, rev):
    c, w = b.shape
    parts = []
    for blk in range(c // (2 * m)):
        row = blk * 2 * m + (m if rev else m - 1)
        parts.append(jnp.broadcast_to(b[row:row + 1, :], (2 * m, w)))
    return parts[0] if len(parts) == 1 else jnp.concatenate(parts, axis=0)


def _hgrn_steps(layer, q_ref, ff_ref, fb_ref, i_ref, g_ref, lbl_ref, ng_ref, o_ref, of_ref, ob_ref, stf_ref,
                stb_ref):
    s = q_ref.shape[0]
    c = min(HGRN_CHUNK, s)
    nc = s // c
    band = min(HGRN_BAND, c)
    if layer > 0:
        lg = lbl_ref[...]
        e = jnp.exp(lg - jnp.max(lg, axis=0, keepdims=True))
        p = e / jnp.sum(e, axis=0, keepdims=True)
        lb = p[1:2, :]
        for r in range(2, layer + 1):
            lb = lb + p[r:r + 1, :]
        log_lb = jnp.log(lb)
        log_1m = jnp.log(1.0 - lb)
        one_m = 1.0 - lb

    grp = math.gcd(nc, MIXER_GROUP)
    rows = grp * c
    ri = lax.broadcasted_iota(jnp.int32, (c, c), 0)
    ci = lax.broadcasted_iota(jnp.int32, (c, c), 1)
    tris = ((ri >= ci).astype(BF16), (ri <= ci).astype(BF16))
    ri = lax.broadcasted_iota(jnp.int32, (rows, c), 0) & (c - 1)
    ci = lax.broadcasted_iota(jnp.int32, (rows, c), 1)
    rw = lax.broadcasted_iota(jnp.int32, (rows, HEAD_DIM), 0)
    pos = lax.broadcasted_iota(jnp.int32, (rows, 1), 0) & (band - 1)
    levels = []
    m = c // 2
    while m >= band:
        levels.append(m)
        m //= 2
    upper = {m: (rw & (2 * m - 1)) >= m for m in levels}
    same_blk = {m: (ri >> int(math.log2(2 * m))) == (ci >> int(math.log2(2 * m))) for m in levels if 2 * m < c}
    diags = ([ri - ci == dl for dl in range(band)], [ci - ri == dl for dl in range(band)])
    per_chunk = lambda x: [x[g * c:(g + 1) * c] for g in range(grp)]
    cat = lambda xs: jnp.concatenate(xs, axis=0)
    chunks3 = lambda x: x.reshape(grp, c, x.shape[-1])

    def group(r0, rev, st_ref):
        f = (fb_ref if rev else ff_ref)[pl.ds(r0, rows), :]
        e_f = jnp.exp(-jnp.abs(f))
        den = 1.0 + e_f
        lf = jnp.minimum(f, 0.0) - jnp.log(den)
        kd = jnp.where(f >= 0.0, e_f, 1.0) / den
        if layer > 0:
            x = log_1m + lf
            mx = jnp.maximum(log_lb, x)
            lf = mx + jnp.log(jnp.exp(log_lb - mx) + jnp.exp(x - mx))
            kd = one_m * kd
        b = cat([_cumsum_mm(tris[rev], x) for x in per_chunk(lf * LOG2_E)])
        tot3 = chunks3(b)[:, 0:1] if rev else chunks3(b)[:, c - 1:c]
        qs = jax.nn.silu(q_ref[pl.ds(r0, rows), :].astype(F32))
        v = i_ref[pl.ds(r0, rows), :]
        q_in = per_chunk((qs * jnp.exp2(b)).astype(BF16))
        k_out = per_chunk((chunks3(kd) * jnp.exp2(tot3 - chunks3(b))).astype(BF16).reshape(rows, HEAD_DIM))
        kv = [_dot_tn(x, y) for x, y in zip(per_chunk(v), k_out)]
        amat = None
        for m in levels:
            bref = _block_ref(b, m, rev)
            up = upper[m]
            qm = jnp.logical_not(up) if rev else up
            km = up if rev else jnp.logical_not(up)
            qt = (qs * jnp.exp2(jnp.where(qm, b - bref, -jnp.inf))).astype(BF16)
            kt = (kd * jnp.exp2(jnp.where(km, bref - b, -jnp.inf))).astype(BF16)
            am = cat([_dot_nt(x, y) for x, y in zip(per_chunk(qt), per_chunk(kt))])
            if m in same_blk:
                am = jnp.where(same_blk[m], am, 0.0)
            amat = am if amat is None else amat + am
        for dl in range(band):
            if dl == 0:
                a = jnp.sum(qs * kd, axis=1, keepdims=True)
            else:
                sh = (rows - dl) if rev else dl
                valid = (pos + dl < band) if rev else (pos >= dl)
                ex = jnp.exp2(b - pltpu.roll(b, sh, 0))
                a = jnp.where(valid, jnp.sum(qs * pltpu.roll(kd, sh, 0) * ex, axis=1, keepdims=True), 0.0)
            term = jnp.where(diags[rev][dl], a, 0.0)
            amat = term if amat is None else amat + term
        o_intra = [_dot(x, y) for x, y in zip(per_chunk(amat.astype(BF16)), per_chunk(v))]
        st = st_ref[...]
        outs = [None] * grp
        for gi in (range(grp - 1, -1, -1) if rev else range(grp)):
            outs[gi] = o_intra[gi] + _dot_nt(q_in[gi], st.astype(BF16))
            st = jnp.exp2(tot3[gi]) * st + kv[gi]
        return cat(outs), st

    def init():
        stf_ref[...] = jnp.zeros_like(stf_ref)
        stb_ref[...] = jnp.zeros_like(stb_ref)

    def step(i, carry):
        rf = pl.multiple_of(i * rows, rows)
        rb = pl.multiple_of((nc // grp - 1 - i) * rows, rows)
        out_f, st_f = group(rf, 0, stf_ref)
        out_b, st_b = group(rb, 1, stb_ref)

        def commit():
            of_ref[pl.ds(rf, rows), :] = out_f
            ob_ref[pl.ds(rb, rows), :] = out_b
            stf_ref[...] = st_f
            stb_ref[...] = st_b

        return carry, commit

    def fin(ci_):
        r0 = pl.multiple_of(ci_ * rows, rows)
        hs = of_ref[pl.ds(r0, rows), :] + ob_ref[pl.ds(r0, rows), :]
        gate = jax.nn.sigmoid(g_ref[pl.ds(r0, rows), :].astype(F32))
        o_ref[pl.ds(r0, rows), :] = (gate * _rms(hs, ng_ref[...])).astype(o_ref.dtype)

    return nc // grp, _Steps(init, step, fin, jnp.int32(0))


def _hgrn_body(layer, *refs):
    n, part = _hgrn_steps(layer, *refs)
    _run_steps(n, part, 1)


def _head_blk(s, off):
    return pl.BlockSpec((s, HEAD_DIM), lambda b, h: (b, off + h))


def _head_row(rows):
    return pl.BlockSpec((rows, HEAD_DIM), lambda b, h: (0, h))


def _mlstm_call_parts(pbf, pfp, lp, s):
    cb, cg = BF_B // HEAD_DIM, FP_GATE // HEAD_DIM
    specs = [_head_blk(s, cb), _head_blk(s, cb + 4), _head_blk(s, cb + 8), _head_blk(s, cb + 12),
             _head_blk(s, cg), _head_row(1)]
    scratch = [pltpu.VMEM((s, HEAD_DIM), F32)] * 2 + [pltpu.VMEM((HEAD_DIM, 2 * HEAD_DIM), F32)] * 2
    return specs, (pbf, pbf, pbf, pbf, pfp, lp["mlstm_norm"]), scratch


def _hgrn_call_parts(pbf, pfp, lp, lb_logits, s):
    cd = BF_D3 // HEAD_DIM
    specs = [_head_blk(s, cd), _head_blk(s, FP_FF // HEAD_DIM), _head_blk(s, FP_FB // HEAD_DIM),
             _head_blk(s, cd + 4), _head_blk(s, cd + 8), _head_row(lb_logits.shape[0]), _head_row(1)]
    scratch = [pltpu.VMEM((s, HEAD_DIM), F32)] * 2 + [pltpu.VMEM((HEAD_DIM, HEAD_DIM), F32)] * 2
    return specs, (pbf, pfp, pfp, pbf, pbf, lb_logits, lp["hgrn_norm"]), scratch


def _mlstm_hgrn(pbf, pfp, lp, lb_logits, layer, bsz, s):
    t = pbf.shape[0]
    m_specs, m_args, m_scr = _mlstm_call_parts(pbf, pfp, lp, s)
    h_specs, h_args, h_scr = _hgrn_call_parts(pbf, pfp, lp, lb_logits, s)
    out_spec = _head_blk(s, 0)
    out_shape = jax.ShapeDtypeStruct((t, MIX_W), BF16)
    m_out = pl.pallas_call(
        _mlstm_body, grid=(bsz, N_HEADS), in_specs=m_specs, out_specs=out_spec, out_shape=out_shape,
        scratch_shapes=m_scr, compiler_params=_cp(("parallel", "parallel")), name="mlstm")(*m_args)
    h_out = pl.pallas_call(
        functools.partial(_hgrn_body, layer), grid=(bsz, N_HEADS), in_specs=h_specs, out_specs=out_spec,
        out_shape=out_shape, scratch_shapes=h_scr, compiler_params=_cp(("parallel", "parallel")),
        name="hgrn2")(*h_args)
    return m_out, h_out


def _shift3(cat, w, bias, ts):
    n = ts + 2 * HALO
    return (bias + w[0:1] * pltpu.roll(cat, 1, 0)[HALO:HALO + ts] + w[1:2] * cat[HALO:HALO + ts]
            + w[2:3] * pltpu.roll(cat, n - 1, 0)[HALO:HALO + ts])


def _hyena_prep_body(x_ref, xp_ref, xn_ref, cw_ref, cb_ref, z_ref, x1_ref):
    i = pl.program_id(1)
    ts = x_ref.shape[0]
    prev = jnp.where(i > 0, xp_ref[...].astype(F32), 0.0)
    nxt = jnp.where(i < pl.num_programs(1) - 1, xn_ref[...].astype(F32), 0.0)
    cat = jnp.concatenate([prev, x_ref[...].astype(F32), nxt], axis=0)
    u = _shift3(cat, cw_ref[...], cb_ref[...], ts)
    z_ref[...] = (u[:, 2 * MIX_W:] * u[:, :MIX_W]).astype(z_ref.dtype)
    x1_ref[...] = u[:, MIX_W:2 * MIX_W].astype(x1_ref.dtype)


def _hyena_prep(pbf, lp, bsz, s):
    t = pbf.shape[0]
    ts = min(512, s)
    per = s // ts
    hb = ts // HALO
    nb = t // HALO
    w = 3 * MIX_W
    return pl.pallas_call(
        _hyena_prep_body,
        grid=(bsz, per),
        in_specs=[
            pl.BlockSpec((ts, w), lambda b, i: (b * per + i, 0)),
            pl.BlockSpec((HALO, w), lambda b, i: (jnp.maximum((b * per + i) * hb - 1, 0), 0)),
            pl.BlockSpec((HALO, w), lambda b, i: (jnp.minimum((b * per + i + 1) * hb, nb - 1), 0)),
            pl.BlockSpec((3, w), lambda b, i: (0, 0)),
            pl.BlockSpec((1, w), lambda b, i: (0, 0)),
        ],
        out_specs=[pl.BlockSpec((ts, MIX_W), lambda b, i: (b * per + i, 0)),
                   pl.BlockSpec((ts, MIX_W), lambda b, i: (b * per + i, 0))],
        out_shape=[jax.ShapeDtypeStruct((t, MIX_W), BF16), jax.ShapeDtypeStruct((t, MIX_W), BF16)],
        compiler_params=_cp(("parallel", "arbitrary")),
        name="hyena_prep",
    )(pbf, pbf, pbf, lp["hyena_conv_w"], lp["hyena_conv_b"])


def _split_bf16(a):
    hi = a.astype(BF16)
    return jnp.stack([hi, (a - hi.astype(F32)).astype(BF16)])


def _dot3(f_hi, f_lo, x):
    x_hi = x.astype(BF16)
    x_lo = (x - x_hi.astype(F32)).astype(BF16)
    return _dot(f_hi, x_hi) + _dot(f_hi, x_lo) + _dot(f_lo, x_hi)


def _dft1_body(x_ref, f_ref, o_ref):
    n1h, r, c = x_ref.shape
    x = x_ref[...].reshape(n1h * r, c)
    o = _dot3(f_ref[0], f_ref[1], x) if f_ref.ndim == 3 else _dot(f_ref[...], x)
    o_ref[...] = o.reshape(o_ref.shape).astype(o_ref.dtype)


def _dft1(x4, kron1, out_dtype):
    bsz, n1h, n2, c = x4.shape
    r = kron1.shape[-1] // n1h
    n1 = kron1.shape[-2] // (2 * r)
    return pl.pallas_call(
        _dft1_body,
        grid=(bsz, n2 // r),
        in_specs=[pl.BlockSpec((None, n1h, r, c), lambda b, j: (b, 0, j, 0)),
                  pl.BlockSpec(kron1.shape, lambda b, j: (0,) * kron1.ndim)],
        out_specs=pl.BlockSpec((None, None, 2, n1, r, c), lambda b, j: (b, j, 0, 0, 0, 0)),
        out_shape=jax.ShapeDtypeStruct((bsz, n2 // r, 2, n1, r, c), out_dtype),
        compiler_params=_cp(("parallel", "parallel")),
        name="dft_stage1",
    )(x4, kron1)


def _spectrum_rows(a_ref, lead, kk):
    nj = a_ref.shape[len(lead)]
    return jnp.concatenate([a_ref[lead + (j, e, kk)] for e in range(2) for j in range(nj)], axis=0)


def _dft2_conv_body(kb, a_ref, f_ref, fi_ref, g_ref, o_ref):
    nj, _, _, r, _ = a_ref.shape
    n2 = nj * r
    for kk in range(kb):
        x = _dot(f_ref[kk], _spectrum_rows(a_ref, (), kk))
        xr, xi = x[:n2], x[n2:]
        gr, gi = g_ref[kk, 0], g_ref[kk, 1]
        y = jnp.concatenate([xr * gr - xi * gi, xr * gi + xi * gr], axis=0).astype(BF16)
        bq = _dot(fi_ref[kk], y).astype(o_ref.dtype)
        for e in range(2):
            for j in range(nj):
                o_ref[j, e, kk] = bq[e * n2 + j * r:e * n2 + (j + 1) * r]


def _dft2_conv(a6, f2, f2i, gspec, kb):
    bsz, nj, _, n1, r, c = a6.shape
    n2 = nj * r
    blk = pl.BlockSpec((None, nj, 2, kb, r, c), lambda k, b: (b, 0, 0, k, 0, 0))
    return pl.pallas_call(
        functools.partial(_dft2_conv_body, kb),
        grid=(n1 // kb, bsz),
        in_specs=[blk,
                  pl.BlockSpec((kb, 2 * n2, 2 * n2), lambda k, b: (k, 0, 0)),
                  pl.BlockSpec((kb, 2 * n2, 2 * n2), lambda k, b: (k, 0, 0)),
                  pl.BlockSpec((kb, 2, n2, c), lambda k, b: (k, 0, 0, 0))],
        out_specs=blk,
        out_shape=jax.ShapeDtypeStruct(a6.shape, BF16),
        compiler_params=_cp(("parallel", "arbitrary")),
        name="dft_stage2_conv",
    )(a6, f2, f2i, gspec)


def _dft2_filter_body(kb, a_ref, f_ref, o_ref):
    n2 = a_ref.shape[1] * a_ref.shape[4]
    k0 = pl.program_id(0) * kb
    for kk in range(kb):
        sgn = (1 - 2 * ((k0 + kk) & 1)).astype(F32)
        xs = [_dot3(f_ref[0, kk], f_ref[1, kk], _spectrum_rows(a_ref, (part,), kk)) for part in range(2)]
        x = xs[0] + sgn * xs[1]
        o_ref[kk, 0] = x[:n2]
        o_ref[kk, 1] = x[n2:]


def _dft2_filter(a6, f2, kb):
    _, nj, _, n1, r, c = a6.shape
    n2 = nj * r
    return pl.pallas_call(
        functools.partial(_dft2_filter_body, kb),
        grid=(n1 // kb,),
        in_specs=[pl.BlockSpec((2, nj, 2, kb, r, c), lambda k: (0, 0, 0, k, 0, 0)),
                  pl.BlockSpec((2, kb, 2 * n2, 2 * n2), lambda k: (0, k, 0, 0))],
        out_specs=pl.BlockSpec((kb, 2, n2, c), lambda k: (k, 0, 0, 0)),
        out_shape=jax.ShapeDtypeStruct((n1, 2, n2, c), F32),
        compiler_params=_cp(("parallel",)),
        name="dft_stage2_filter",
    )(a6, f2)


def _dft3_body(b_ref, f_ref, x1_ref, z_ref, sk_ref, o_ref):
    _, n1, r, c = b_ref.shape
    y = _dot(f_ref[...], b_ref[...].reshape(2 * n1 * r, c)).reshape(o_ref.shape)
    z = z_ref[...].astype(F32)
    o_ref[...] = (x1_ref[...].astype(F32) * (y + sk_ref[...] * z)).astype(o_ref.dtype)


def _dft3(b6, kron3, x1_4, z4, skip):
    bsz, nj, _, n1, r, c = b6.shape
    n1h, n2 = n1 // 2, nj * r
    assert r == kron3.shape[0] // n1h
    row = pl.BlockSpec((None, n1h, r, c), lambda b, j: (b, 0, j, 0))
    return pl.pallas_call(
        _dft3_body,
        grid=(bsz, nj),
        in_specs=[pl.BlockSpec((None, None, 2, n1, r, c), lambda b, j: (b, j, 0, 0, 0, 0)),
                  pl.BlockSpec(kron3.shape, lambda b, j: (0, 0)),
                  row, row,
                  pl.BlockSpec((1, c), lambda b, j: (0, 0))],
        out_specs=row,
        out_shape=jax.ShapeDtypeStruct((bsz, n1h, n2, c), BF16),
        compiler_params=_cp(("parallel", "parallel")),
        name="dft_stage3",
    )(b6, kron3, x1_4, z4, skip)


def _hyena_filter_body(seq_len, pos_ref, sc_ref, bands_ref, w1t_ref, w1c_ref, w1s_ref, b1_ref, fr1_ref,
                       w2_ref, b2_ref, fr2_ref, w3_ref, rate_ref, o_ref):
    pos = pos_ref[...]
    t = pos * (1.0 / (seq_len - 1))
    arg = (pos * (2.0 * math.pi / seq_len)) * bands_ref[...]
    hdot = lambda a, b: jnp.dot(a, b, preferred_element_type=F32, precision=HIGHEST)
    pre = t * w1t_ref[...] + hdot(jnp.cos(arg), w1c_ref[...]) - hdot(jnp.sin(arg), w1s_ref[...]) + b1_ref[...]
    hid = jnp.sin(fr1_ref[...] * pre)
    hid = jnp.sin(fr2_ref[...] * (hdot(hid, w2_ref[...]) + b2_ref[...]))
    o_ref[...] = hdot(hid, w3_ref[...]) * jnp.exp(-t * rate_ref[...]) * sc_ref[...]


def _hyena_filter(lp, s):
    ts = min(256, s)
    r = 2 * s + ts
    pos = np.concatenate([np.arange(s), s - np.arange(s), np.zeros(ts)]).astype(np.float32)[:, None]
    sc = np.ones((r, 1), np.float32)
    sc[s] = 0.0
    pad = HEAD_DIM - HYENA_HID
    bands = np.zeros((1, HEAD_DIM), np.float32)
    bands[0, :HYENA_BANDS] = np.linspace(1e-4, HYENA_BANDS - 1, HYENA_BANDS)
    rate = np.abs(np.linspace(math.log(HYENA_TARGET) / HYENA_FAST, math.log(HYENA_TARGET) / HYENA_SLOW, MIX_W))
    rate = np.tile(rate, 2).astype(np.float32)[None, :]
    half = lambda i: (0, jnp.where(i < s // ts, 0, 1))
    w1 = lp["hyena_w1"]
    padc = lambda a: jnp.pad(a, ((0, 0), (0, pad)))
    w1t = padc(w1[0:1])
    w1c = jnp.pad(w1[1:1 + HYENA_BANDS], ((0, HEAD_DIM - HYENA_BANDS), (0, pad)))
    w1s = jnp.pad(w1[1 + HYENA_BANDS:], ((0, HEAD_DIM - HYENA_BANDS), (0, pad)))
    w2 = jnp.pad(lp["hyena_w2"], ((0, pad), (0, pad)))
    w3 = jnp.pad(lp["hyena_w3"], ((0, pad), (0, 0)))
    vec = lambda a: padc(a[None, :])
    full = lambda a: pl.BlockSpec(a.shape, lambda i: (0, 0))
    args = [jnp.asarray(bands), w1t, w1c, w1s, vec(lp["hyena_b1"]), vec(lp["hyena_freq1"]), w2,
            vec(lp["hyena_b2"]), vec(lp["hyena_freq2"])]
    out = pl.pallas_call(
        functools.partial(_hyena_filter_body, s),
        grid=(r // ts,),
        in_specs=[pl.BlockSpec((ts, 1), lambda i: (i, 0)), pl.BlockSpec((ts, 1), lambda i: (i, 0))]
                 + [full(a) for a in args]
                 + [pl.BlockSpec((HEAD_DIM, MIX_W), half), pl.BlockSpec((1, MIX_W), half)],
        out_specs=pl.BlockSpec((ts, MIX_W), lambda i: (i, 0)),
        out_shape=jax.ShapeDtypeStruct((r, MIX_W), F32),
        compiler_params=_cp(("parallel",)),
        name="hyena_filter",
    )(jnp.asarray(pos), jnp.asarray(sc), *args, w3, jnp.asarray(rate))
    return out[:s], out[s:2 * s], out[2 * s:2 * s + 1]


def _dft_tables(s):
    n = 2 * s
    n2 = min(DFT_N2, s // 8)
    n1 = n // n2
    n1h = n1 // 2
    two_pi = 2.0 * math.pi
    k1 = jnp.arange(n1, dtype=jnp.int32)
    a1 = (two_pi / n1) * ((k1[:, None] * k1[None, :n1h]) % n1).astype(F32)
    f1 = jnp.concatenate([jnp.cos(a1), -jnp.sin(a1)], axis=0)
    f1i = jnp.concatenate([jnp.cos(a1).T, -jnp.sin(a1).T], axis=1) * (1.0 / n)
    j = jnp.arange(n2, dtype=jnp.int32)
    ph = (j[None, :, None] * j[None, None, :] * n1 + j[None, None, :] * k1[:, None, None]) % n
    a2 = (two_pi / n) * ph.astype(F32)
    tr, ti = jnp.cos(a2), -jnp.sin(a2)
    f2 = jnp.concatenate([jnp.concatenate([tr, -ti], axis=2), jnp.concatenate([ti, tr], axis=2)], axis=1)
    kron = lambda m, r: jnp.kron(m, jnp.eye(r, dtype=F32))
    r32, r16 = min(SUBLANES, n2), min(2 * SUBLANES, n2)
    return dict(n1=n1, n2=n2, f2_split=_split_bf16(f2), f2_bf=f2.astype(BF16),
                f2i_bf=jnp.swapaxes(f2, 1, 2).astype(BF16), kron1_split=_split_bf16(kron(f1, r32)),
                kron1_bf=kron(f1, r16).astype(BF16), kron3_bf=kron(f1i, r16).astype(BF16))


def _hyena_spectrum(lp, s, tab):
    gpos, gneg, hb0 = _hyena_filter(lp, s)
    n1, n2 = tab["n1"], tab["n2"]
    a = _dft1(jnp.stack([gpos, gneg]).reshape(2, n1 // 2, n2, MIX_W), tab["kron1_split"], F32)
    return _dft2_filter(a, tab["f2_split"], min(2, n1)), hb0


def _hyena(pbf, lp, gspec, hb0, tab, bsz, s):
    n1, n2 = tab["n1"], tab["n2"]
    z, x1 = _hyena_prep(pbf, lp, bsz, s)
    z4 = z.reshape(bsz, n1 // 2, n2, MIX_W)
    a = _dft1(z4, tab["kron1_bf"], BF16)
    bq = _dft2_conv(a, tab["f2_bf"], tab["f2i_bf"], gspec, min(8, n1))
    out = _dft3(bq, tab["kron3_bf"], x1.reshape(z4.shape), z4, lp["hyena_skip"][None, :] + hb0)
    return out.reshape(bsz * s, MIX_W)


def _merge_body(h_ref, g0_ref, g1_ref, g2_ref, g3_ref, ba_ref, bb_ref, bc_ref, bd_ref, wb_ref, wo_ref, o_ref):
    merged = None
    for idx, (g_ref, br) in enumerate(zip((g0_ref, g1_ref, g2_ref, g3_ref), (ba_ref, bb_ref, bc_ref, bd_ref))):
        term = jax.nn.sigmoid(g_ref[...].astype(F32)) * _dot(br[...], wb_ref[idx])
        merged = term if merged is None else merged + term
    o_ref[...] = h_ref[...] + _dot(merged.astype(BF16), wo_ref[...])


def _merge(h, pbf, branches, lp):
    t, d = h.shape
    tm = min(512, t)
    gcol = BF_G // d
    row = lambda w: pl.BlockSpec((tm, w), lambda i: (i, 0))
    gate = lambda idx: pl.BlockSpec((tm, d), lambda i: (i, gcol + idx))
    return pl.pallas_call(
        _merge_body,
        grid=(t // tm,),
        in_specs=[row(d), gate(0), gate(1), gate(2), gate(3),
                  row(MIX_W), row(MIX_W), row(MIX_W), row(MIX_W),
                  pl.BlockSpec(lp["w_branch"].shape, lambda i: (0, 0, 0)),
                  pl.BlockSpec(lp["w_out"].shape, lambda i: (0, 0))],
        out_specs=row(d),
        out_shape=jax.ShapeDtypeStruct((t, d), F32),
        compiler_params=_cp(("parallel",)),
        name="merge",
    )(h, pbf, pbf, pbf, pbf, *branches, lp["w_branch"], lp["w_out"])


def _ffn_body(per, final, h_ref, hp_ref, hn_ref, p_ref, gf_ref, wu_ref, cw_ref, cb_ref, wd_ref, gp_ref, wg_ref,
              wp_ref, gl_ref, o_ref):
    i = pl.program_id(0)
    tm = h_ref.shape[0]
    dff = wd_ref.shape[0]
    gf = gf_ref[...]
    h = h_ref[...]
    first = (i % per) == 0
    last = (i % per) == per - 1
    xp = jnp.where(first, 0.0, _rms(hp_ref[...], gf))
    xn = jnp.where(last, 0.0, _rms(hn_ref[...], gf))
    cat = jnp.concatenate([xp, _rms(h, gf), xn], axis=0).astype(BF16)
    cb_w = min(512, dff)
    acc = jnp.zeros_like(h)
    for cb in range(dff // cb_w):
        lo, lo2 = cb * cb_w, dff + cb * cb_w
        u1 = _shift3(_dot(cat, wu_ref[:, lo:lo + cb_w]), cw_ref[:, lo:lo + cb_w], cb_ref[:, lo:lo + cb_w], tm)
        u2 = _shift3(_dot(cat, wu_ref[:, lo2:lo2 + cb_w]), cw_ref[:, lo2:lo2 + cb_w], cb_ref[:, lo2:lo2 + cb_w], tm)
        acc = acc + _dot((jax.nn.gelu(u1) * u2).astype(BF16), wd_ref[lo:lo + cb_w, :])
    h = h + acc
    gate = jax.nn.sigmoid(_dot(_rms(h, gp_ref[...]).astype(BF16), wg_ref[...]))
    h = h + gate * _dot(p_ref[...].astype(BF16), wp_ref[...])
    if final:
        h = _rms(h, gl_ref[...])
    o_ref[...] = h


def _ffn(h, p, lp, final_norm, s, final):
    t, d = h.shape
    tm = min(512, s)
    per = s // tm
    hb = tm // HALO
    nb = t // HALO
    full = lambda a: pl.BlockSpec(a.shape, lambda i: (0,) * a.ndim)
    row = lambda w: pl.BlockSpec((tm, w), lambda i: (i, 0))
    ws = [lp["norm_ffn"], lp["w_up"], lp["ffn_conv_w"], lp["ffn_conv_b"], lp["w_down"], lp["norm_ple"],
          lp["w_ple_gate"], lp["w_ple"], final_norm]
    return pl.pallas_call(
        functools.partial(_ffn_body, per, final),
        grid=(t // tm,),
        in_specs=[row(d),
                  pl.BlockSpec((HALO, d), lambda i: (jnp.maximum(i * hb - 1, 0), 0)),
                  pl.BlockSpec((HALO, d), lambda i: (jnp.minimum((i + 1) * hb, nb - 1), 0)),
                  row(p.shape[1])] + [full(a) for a in ws],
        out_specs=row(d),
        out_shape=jax.ShapeDtypeStruct((t, d), F32),
        compiler_params=_cp(("parallel",)),
        name="ffn_ple",
    )(h, h, h, p, *ws)


def _prepare_params(prm):
    d_model = prm["w_in"].shape[1]
    depth = prm["w_in"].shape[0]
    off_b = 2 * MIX_W
    off_gate = off_b + 4 * MIX_W
    off_c = off_gate + 4 * N_HEADS
    off_d = off_c + 3 * MIX_W
    off_g = off_d + 5 * MIX_W

    def regroup(a):
        sl = lambda lo, w: a[..., lo:lo + w]
        bf = jnp.concatenate([sl(off_c, 3 * MIX_W), sl(off_d, MIX_W), sl(off_d + 3 * MIX_W, 2 * MIX_W),
                              sl(off_g, 4 * d_model), sl(0, 2 * MIX_W), sl(off_b, 4 * MIX_W)], axis=-1)
        gates = sl(off_gate, 4 * N_HEADS).reshape(a.shape[:-1] + (4, N_HEADS))
        gates = jnp.swapaxes(gates, -1, -2)
        gates = jnp.pad(gates, [(0, 0)] * (gates.ndim - 1) + [(0, HEAD_DIM - 4)])
        fp = jnp.concatenate([sl(off_d + MIX_W, 2 * MIX_W), gates.reshape(a.shape[:-1] + (N_HEADS * HEAD_DIM,))],
                             axis=-1)
        return bf, fp

    w_bf, w_fp = regroup(prm["w_in"])
    b_bf, b_fp = regroup(prm["b_in"][:, None, :])
    gw = jnp.transpose(prm["rglru_w"], (0, 3, 4, 1, 2, 5)).reshape(depth, N_HEADS, HEAD_DIM, 4 * HEAD_DIM)
    gb = prm["rglru_b"].reshape(depth, 2, 2, N_HEADS, HEAD_DIM)
    gb = jnp.transpose(gb, (0, 3, 1, 2, 4)).reshape(depth, N_HEADS, 1, 4 * HEAD_DIM)
    row = lambda a: a[:, None, :]
    out = dict(
        norm_mix=row(prm["norm_mix"]), w_bf=w_bf.astype(BF16), b_bf=b_bf, w_fp=w_fp.astype(BF16), b_fp=b_fp,
        conv_a_w=prm["conv_a_w"], conv_a_b=row(prm["conv_a_b"]), rglru_w=gw.astype(BF16), rglru_b=gb,
        rglru_lam=prm["rglru_lam"], mlstm_norm=row(prm["mlstm_norm"]),
        hyena_conv_w=prm["hyena_conv_w"], hyena_conv_b=row(prm["hyena_conv_b"]),
        hyena_w1=prm["hyena_w1"], hyena_b1=prm["hyena_b1"], hyena_freq1=prm["hyena_freq1"],
        hyena_w2=prm["hyena_w2"], hyena_b2=prm["hyena_b2"], hyena_freq2=prm["hyena_freq2"],
        hyena_w3=prm["hyena_w3"], hyena_skip=prm["hyena_skip"], hgrn_norm=row(prm["hgrn_norm"]),
        w_branch=prm["w_branch"].astype(BF16), w_out=prm["w_out"].astype(BF16),
        norm_ffn=row(prm["norm_ffn"]), w_up=prm["w_up"].astype(BF16), ffn_conv_w=prm["ffn_conv_w"],
        ffn_conv_b=row(prm["ffn_conv_b"]), w_down=prm["w_down"].astype(BF16), norm_ple=row(prm["norm_ple"]),
        w_ple_gate=prm["w_ple_gate"].astype(BF16), w_ple=prm["w_ple"].astype(BF16),
    )
    return out


def _trunk(x, p, prm, lb_logits, final_norm):
    bsz, s, d = x.shape
    depth = p.shape[0]
    h = x.reshape(bsz * s, d)
    tab = _dft_tables(s)
    for i in range(depth):
        lp = {k: v[i] for k, v in prm.items()}
        pbf, pfp = _proj(h, lp["norm_mix"], lp["w_bf"], lp["b_bf"], lp["w_fp"], lp["b_fp"])
        gspec, hb0 = _hyena_spectrum(lp, s, tab)
        br_b, br_d = _mlstm_hgrn(pbf, pfp, lp, lb_logits, i, bsz, s)
        branches = (_rglru(pbf, lp, bsz, s), br_b, _hyena(pbf, lp, gspec, hb0, tab, bsz, s), br_d)
        h = _merge(h, pbf, branches, lp)
        h = _ffn(h, p[i].reshape(bsz * s, -1), lp, final_norm, s, i == depth - 1)
    return h.reshape(bsz, s, d)


def kernel(x_prompt, x_sample, p_prompt, p_sample, norm_mix, w_in, b_in, conv_a_w, conv_a_b, rglru_w, rglru_b, rglru_lam, mlstm_norm, hyena_conv_w, hyena_conv_b, hyena_w1, hyena_b1, hyena_freq1, hyena_w2, hyena_b2, hyena_freq2, hyena_w3, hyena_skip, hgrn_lb_logits, hgrn_norm, w_branch, w_out, norm_ffn, w_up, ffn_conv_w, ffn_conv_b, w_down, norm_ple, w_ple_gate, w_ple, final_norm):
    prm = _prepare_params(dict(
        norm_mix=norm_mix, w_in=w_in, b_in=b_in, conv_a_w=conv_a_w, conv_a_b=conv_a_b, rglru_w=rglru_w,
        rglru_b=rglru_b, rglru_lam=rglru_lam, mlstm_norm=mlstm_norm, hyena_conv_w=hyena_conv_w,
        hyena_conv_b=hyena_conv_b, hyena_w1=hyena_w1, hyena_b1=hyena_b1, hyena_freq1=hyena_freq1,
        hyena_w2=hyena_w2, hyena_b2=hyena_b2, hyena_freq2=hyena_freq2, hyena_w3=hyena_w3,
        hyena_skip=hyena_skip, hgrn_norm=hgrn_norm, w_branch=w_branch, w_out=w_out, norm_ffn=norm_ffn,
        w_up=w_up, ffn_conv_w=ffn_conv_w, ffn_conv_b=ffn_conv_b, w_down=w_down, norm_ple=norm_ple,
        w_ple_gate=w_ple_gate, w_ple=w_ple))
    fn = final_norm[None, :]
    y_prompt = _trunk(x_prompt, p_prompt, prm, hgrn_lb_logits, fn)
    y_sample = _trunk(x_sample, p_sample, prm, hgrn_lb_logits, fn)
    return (y_prompt, y_sample)
```

```python
import collections
import functools
import math

import numpy as np
import jax
import jax.numpy as jnp
from jax import lax
from jax.experimental import pallas as pl
from jax.experimental.pallas import tpu as pltpu

F32 = jnp.float32
BF16 = jnp.bfloat16
HIGHEST = lax.Precision.HIGHEST

N_HEADS = 4
HEAD_DIM = 128
MIX_W = N_HEADS * HEAD_DIM
LRU_C = 8.0
LOG2_E = math.log2(math.e)
EPS = 1e-6
STAB_INIT = -1e30
HYENA_BANDS = 16
HYENA_HID = 64
HYENA_FAST = 0.3
HYENA_SLOW = 1.5
HYENA_TARGET = 1e-2

MLSTM_CHUNK = 128
HGRN_CHUNK = 128
MIXER_GROUP = 4
HGRN_BAND = 4
SCAN_CHUNK = 512
HALO = 16
SUBLANES = 8
PROJ_COLS = 1280
DFT_N2 = 128
DFT3_ROWS = 512
V7X_VMEM_LIMIT = 56 * 1024 * 1024

BF_C, BF_D3, BF_G, BF_A, BF_B = 0, 1536, 3072, 7168, 8192
BF_COLS = 10240
FP_FF, FP_FB, FP_GATE = 0, 512, 1024
FP_COLS = 1536


def _cp(sem, vmem=V7X_VMEM_LIMIT):
    return pltpu.CompilerParams(dimension_semantics=sem, vmem_limit_bytes=vmem)


def _rms(x, g):
    return x * lax.rsqrt(jnp.mean(x * x, axis=-1, keepdims=True) + EPS) * g


def _log_sigmoid(x):
    return jnp.minimum(x, 0.0) - jnp.log(1.0 + jnp.exp(-jnp.abs(x)))


def _dot(a, b):
    return jnp.dot(a, b, preferred_element_type=F32)


def _dot_nt(a, b):
    return lax.dot_general(a, b, (((1,), (1,)), ((), ())), preferred_element_type=F32)


def _dot_tn(a, b):
    return lax.dot_general(a, b, (((0,), (0,)), ((), ())), preferred_element_type=F32)


def _cumsum_mm(tri, x):
    hi = x.astype(BF16)
    lo = (x - hi.astype(F32)).astype(BF16)
    return _dot(tri, hi) + _dot(tri, lo)


def _proj_body(h_ref, g_ref, wb_ref, bb_ref, wf_ref, bf_ref, ob_ref, of_ref):
    xn = _rms(h_ref[...], g_ref[...]).astype(BF16)
    for w_ref, b_ref, o_ref in ((wb_ref, bb_ref, ob_ref), (wf_ref, bf_ref, of_ref)):
        n = w_ref.shape[1]
        tn = math.gcd(PROJ_COLS, n)
        for lo in range(0, n, tn):
            o_ref[:, lo:lo + tn] = (_dot(xn, w_ref[:, lo:lo + tn]) + b_ref[:, lo:lo + tn]).astype(o_ref.dtype)


def _proj(h, g, w_bf, b_bf, w_fp, b_fp):
    t, d = h.shape
    tm = min(256, t)
    once = lambda a: pl.BlockSpec(a.shape, lambda i: (0, 0), pipeline_mode=pl.Buffered(1))
    row = lambda n: pl.BlockSpec((tm, n), lambda i: (i, 0))
    return pl.pallas_call(
        _proj_body,
        grid=(t // tm,),
        in_specs=[row(d), once(g), once(w_bf), once(b_bf), once(w_fp), once(b_fp)],
        out_specs=[row(w_bf.shape[1]), row(w_fp.shape[1])],
        out_shape=[jax.ShapeDtypeStruct((t, w_bf.shape[1]), BF16), jax.ShapeDtypeStruct((t, w_fp.shape[1]), F32)],
        compiler_params=_cp(("parallel",)),
        name="in_proj",
    )(h, g, w_bf, b_bf, w_fp, b_fp)


def _scan8(a, u, pos, reverse):
    n = a.shape[0]
    d = 1
    while d < SUBLANES:
        if reverse:
            m = pos < SUBLANES - d
            sh = n - d
        else:
            m = pos >= d
            sh = d
        a_s = jnp.where(m, pltpu.roll(a, sh, 0), 1.0)
        u_s = jnp.where(m, pltpu.roll(u, sh, 0), 0.0)
        u = a * u_s + u
        a = a * a_s
        d *= 2
    return a, u


def _chain8(a, u, carry, reverse):
    groups = a.shape[0] // SUBLANES
    outs = [None] * groups
    for gi in (range(groups - 1, -1, -1) if reverse else range(groups)):
        lo = gi * SUBLANES
        h = u[lo:lo + SUBLANES] + a[lo:lo + SUBLANES] * carry
        outs[gi] = h
        carry = h[0:1] if reverse else h[SUBLANES - 1:SUBLANES]
    return jnp.concatenate(outs, axis=0), carry


def _rglru_body(xa_ref, ya_ref, cw_ref, cb_ref, gw_ref, gb_ref, lam_ref, o_ref, xc_ref, hf_ref, hb_ref):
    s = xa_ref.shape[0]
    tc = min(SCAN_CHUNK, s)
    nc = s // tc
    n = tc + 2 * HALO
    pos = lax.broadcasted_iota(jnp.int32, (tc, HEAD_DIM), 0) & (SUBLANES - 1)
    cw = cw_ref[...]
    cb = cb_ref[...]
    sp = jax.nn.softplus(-lam_ref[...])

    def gates(xc, d):
        w = gw_ref[0, :, d * 256:(d + 1) * 256]
        g = jax.nn.sigmoid(_dot(xc.astype(BF16), w) + gb_ref[0, :, d * 256:(d + 1) * 256])
        log_a = (-LRU_C) * g[:, :HEAD_DIM] * sp[d:d + 1]
        a = jnp.exp(log_a)
        x2 = 2.0 * log_a
        one_m_a2 = jnp.where(x2 > -0.01, -x2 * (1.0 + 0.5 * x2 * (1.0 + x2 * (1.0 / 3.0))), 1.0 - a * a)
        u = jnp.sqrt(one_m_a2) * (g[:, HEAD_DIM:] * xc)
        return a, u

    def conv(c, carry):
        r0 = pl.multiple_of(c * tc, tc)
        x = xa_ref[pl.ds(r0, tc), :].astype(F32)
        rp = pl.multiple_of(jnp.maximum(r0 - HALO, 0), HALO)
        rn = pl.multiple_of(jnp.minimum(r0 + tc, s - HALO), HALO)
        prev = jnp.where(c > 0, xa_ref[pl.ds(rp, HALO), :].astype(F32), 0.0)
        nxt = jnp.where(c < nc - 1, xa_ref[pl.ds(rn, HALO), :].astype(F32), 0.0)
        cat = jnp.concatenate([prev, x, nxt], axis=0)
        xc_ref[pl.ds(r0, tc), :] = (
            cb + cw[0:1] * pltpu.roll(cat, 1, 0)[HALO:HALO + tc] + cw[1:2] * x
            + cw[2:3] * pltpu.roll(cat, n - 1, 0)[HALO:HALO + tc]
            + cw[3:4] * pltpu.roll(cat, n - 2, 0)[HALO:HALO + tc])
        return carry

    lax.fori_loop(0, nc, conv, 0)

    def scan(i, carry):
        c_f, c_b = carry
        rf = pl.multiple_of(i * tc, tc)
        rb = pl.multiple_of((nc - 1 - i) * tc, tc)
        a, u = gates(xc_ref[pl.ds(rf, tc), :], 0)
        a, u = _scan8(a, u, pos, False)
        h, c_f = _chain8(a, u, c_f, False)
        hf_ref[pl.ds(rf, tc), :] = h
        a, u = gates(xc_ref[pl.ds(rb, tc), :], 1)
        a, u = _scan8(a, u, pos, True)
        h, c_b = _chain8(a, u, c_b, True)
        hb_ref[pl.ds(rb, tc), :] = h
        return c_f, c_b

    zero = jnp.zeros((1, HEAD_DIM), F32)
    lax.fori_loop(0, nc, scan, (zero, zero))

    def fin(c, carry):
        r0 = pl.multiple_of(c * tc, tc)
        y = jax.nn.gelu(ya_ref[pl.ds(r0, tc), :].astype(F32))
        o_ref[pl.ds(r0, tc), :] = ((hf_ref[pl.ds(r0, tc), :] + hb_ref[pl.ds(r0, tc), :]) * y).astype(o_ref.dtype)
        return carry

    lax.fori_loop(0, nc, fin, 0)


def _rglru(pbf, lp, bsz, s):
    t = pbf.shape[0]
    ca, cy = BF_A // HEAD_DIM, (BF_A + MIX_W) // HEAD_DIM
    return pl.pallas_call(
        _rglru_body,
        grid=(bsz, N_HEADS),
        in_specs=[
            pl.BlockSpec((s, HEAD_DIM), lambda b, h: (b, ca + h)),
            pl.BlockSpec((s, HEAD_DIM), lambda b, h: (b, cy + h)),
            pl.BlockSpec((4, HEAD_DIM), lambda b, h: (0, h)),
            pl.BlockSpec((1, HEAD_DIM), lambda b, h: (0, h)),
            pl.BlockSpec((1, HEAD_DIM, 4 * HEAD_DIM), lambda b, h: (h, 0, 0)),
            pl.BlockSpec((1, 1, 4 * HEAD_DIM), lambda b, h: (h, 0, 0)),
            pl.BlockSpec((2, HEAD_DIM), lambda b, h: (0, h)),
        ],
        out_specs=pl.BlockSpec((s, HEAD_DIM), lambda b, h: (b, h)),
        out_shape=jax.ShapeDtypeStruct((t, MIX_W), BF16),
        scratch_shapes=[pltpu.VMEM((s, HEAD_DIM), F32)] * 3,
        compiler_params=_cp(("parallel", "parallel")),
        name="rglru",
    )(pbf, pbf, lp["conv_a_w"], lp["conv_a_b"], lp["rglru_w"], lp["rglru_b"], lp["rglru_lam"])


_Steps = collections.namedtuple("_Steps", "init step fin carry0")


def _run_steps(n, part, unroll):
    part.init()

    def body(i, carry):
        carry, commit = part.step(i, carry)
        commit()
        return carry

    lax.fori_loop(0, n, body, part.carry0, unroll=unroll)

    def fin(c, carry):
        part.fin(c)
        return carry

    lax.fori_loop(0, n, fin, 0)


def _mlstm_steps(q_ref, k_ref, v_ref, og_ref, g_ref, ng_ref, o_ref, hf_ref, hb_ref, stf_ref, stb_ref):
    s = q_ref.shape[0]
    cl = min(MLSTM_CHUNK, s)
    nc = s // cl
    ri = lax.broadcasted_iota(jnp.int32, (cl, cl), 0)
    ci = lax.broadcasted_iota(jnp.int32, (cl, cl), 1)
    masks = (ri >= ci, ri <= ci)
    tris = (masks[0].astype(BF16), masks[1].astype(BF16))
    grp = math.gcd(nc, MIXER_GROUP)
    rows = grp * cl
    mask_rows = tuple(jnp.concatenate([m] * grp, axis=0) for m in masks)
    ones_v = jnp.ones((rows, HEAD_DIM), BF16)
    kscale = HEAD_DIM ** -0.5
    twice = lambda a: jnp.concatenate([a, a], axis=1)
    per_chunk = lambda x: [x[g * cl:(g + 1) * cl] for g in range(grp)]
    cat = lambda xs: jnp.concatenate(xs, axis=0)
    chunks3 = lambda x: x.reshape(grp, cl, x.shape[-1])

    def load(r0, st_ref):
        return (g_ref[pl.ds(r0, rows), :], q_ref[pl.ds(r0, rows), :], k_ref[pl.ds(r0, rows), :],
                v_ref[pl.ds(r0, rows), :], st_ref[...])

    def group(vals, d, m_st):
        g, q, k, v, st = vals
        ig = jnp.broadcast_to(g[:, 2 * d:2 * d + 1], (rows, HEAD_DIM))
        lf = _log_sigmoid(jnp.broadcast_to(g[:, 2 * d + 1:2 * d + 2], (rows, HEAD_DIM)))
        b = cat([_cumsum_mm(tris[d], x) for x in per_chunk(lf)])
        tot3 = chunks3(b)[:, 0:1] if d else chunks3(b)[:, cl - 1:cl]
        c = ig - b
        lw3 = tot3 + chunks3(c)
        m_loc3 = jnp.max(lw3, axis=1, keepdims=True)
        kf = k.astype(F32) * kscale
        kw = (chunks3(kf) * jnp.exp(lw3 - m_loc3)).astype(BF16).reshape(rows, HEAD_DIM)
        kb = kf.astype(BF16)
        v1 = jnp.concatenate([v, ones_v], axis=1)
        cn_loc = [_dot_tn(a, w) for a, w in zip(per_chunk(kw), per_chunk(v1))]
        dm = jnp.where(mask_rows[d], cat([x.T for x in per_chunk(c)]), -jnp.inf)
        m_rel = jnp.max(dm, axis=1, keepdims=True)
        qk = cat([_dot_nt(a, w) for a, w in zip(per_chunk(q), per_chunk(kb))])
        am = (jnp.exp(dm - m_rel) * qk).astype(BF16)
        nd_intra = cat([_dot(a, w) for a, w in zip(per_chunk(am), per_chunk(v1))])
        outs = [None] * grp
        for gi in (range(grp - 1, -1, -1) if d else range(grp)):
            sl = slice(gi * cl, (gi + 1) * cl)
            nd_inter = _dot(q[sl], st.astype(BF16))
            mu = jnp.maximum(m_rel[sl], m_st)
            f_i = jnp.exp(m_rel[sl] - mu)
            f_s = jnp.exp(m_st - mu)
            num = f_i * nd_intra[sl, :HEAD_DIM] + f_s * nd_inter[:, :HEAD_DIM]
            den = f_i * nd_intra[sl, HEAD_DIM:] + f_s * nd_inter[:, HEAD_DIM:]
            outs[gi] = num / jnp.maximum(jnp.abs(den), jnp.exp(-(b[sl] + mu)))
            tot, m_loc = tot3[gi], m_loc3[gi]
            m_new = jnp.maximum(tot + m_st, m_loc)
            st = twice(jnp.exp(tot + m_st - m_new)) * st + twice(jnp.exp(m_loc - m_new)) * cn_loc[gi]
            m_st = m_new
        return cat(outs), m_st, st

    m0 = jnp.full((1, HEAD_DIM), STAB_INIT, F32)

    def init():
        stf_ref[...] = jnp.zeros_like(stf_ref)
        stb_ref[...] = jnp.zeros_like(stb_ref)

    def step(i, carry):
        m_f, m_b = carry
        rf = pl.multiple_of(i * rows, rows)
        rb = pl.multiple_of((nc // grp - 1 - i) * rows, rows)
        vals_f = load(rf, stf_ref)
        vals_b = load(rb, stb_ref)
        out_f, m_f, st_f = group(vals_f, 0, m_f)
        out_b, m_b, st_b = group(vals_b, 1, m_b)

        def commit():
            hf_ref[pl.ds(rf, rows), :] = out_f
            hb_ref[pl.ds(rb, rows), :] = out_b
            stf_ref[...] = st_f
            stb_ref[...] = st_b

        return (m_f, m_b), commit

    def fin(c):
        r0 = pl.multiple_of(c * rows, rows)
        y = _rms(hf_ref[pl.ds(r0, rows), :] + hb_ref[pl.ds(r0, rows), :], ng_ref[...])
        og = jax.nn.sigmoid(og_ref[pl.ds(r0, rows), :].astype(F32))
        o_ref[pl.ds(r0, rows), :] = (og * y).astype(o_ref.dtype)

    return nc // grp, _Steps(init, step, fin, (m0, m0))


def _mlstm_body(*refs):
    n, part = _mlstm_steps(*refs)
    _run_steps(n, part, 1)


def _block_ref(b, m, rev):
    c, w = b.shape
    parts = []
    for blk in range(c // (2 * m)):
        row = blk * 2 * m + (m if rev else m - 1)
        parts.append(jnp.broadcast_to(b[row:row + 1, :], (2 * m, w)))
    return parts[0] if len(parts) == 1 else jnp.concatenate(parts, axis=0)


def _hgrn_steps(layer, q_ref, ff_ref, fb_ref, i_ref, g_ref, lbl_ref, ng_ref, o_ref, of_ref, ob_ref, stf_ref,
                stb_ref):
    s = q_ref.shape[0]
    c = min(HGRN_CHUNK, s)
    nc = s // c
    band = min(HGRN_BAND, c)
    if layer > 0:
        lg = lbl_ref[...]
        e = jnp.exp(lg - jnp.max(lg, axis=0, keepdims=True))
        p = e / jnp.sum(e, axis=0, keepdims=True)
        lb = p[1:2, :]
        for r in range(2, layer + 1):
            lb = lb + p[r:r + 1, :]
        log_lb = jnp.log(lb)
        log_1m = jnp.log(1.0 - lb)
        one_m = 1.0 - lb

    grp = math.gcd(nc, MIXER_GROUP)
    rows = grp * c
    ri = lax.broadcasted_iota(jnp.int32, (c, c), 0)
    ci = lax.broadcasted_iota(jnp.int32, (c, c), 1)
    tris = ((ri >= ci).astype(BF16), (ri <= ci).astype(BF16))
    ri = lax.broadcasted_iota(jnp.int32, (rows, c), 0) & (c - 1)
    ci = lax.broadcasted_iota(jnp.int32, (rows, c), 1)
    rw = lax.broadcasted_iota(jnp.int32, (rows, HEAD_DIM), 0)
    pos = lax.broadcasted_iota(jnp.int32, (rows, 1), 0) & (band - 1)
    levels = []
    m = c // 2
    while m >= band:
        levels.append(m)
        m //= 2
    upper = {m: (rw & (2 * m - 1)) >= m for m in levels}
    same_blk = {m: (ri >> int(math.log2(2 * m))) == (ci >> int(math.log2(2 * m))) for m in levels if 2 * m < c}
    diags = ([ri - ci == dl for dl in range(band)], [ci - ri == dl for dl in range(band)])
    per_chunk = lambda x: [x[g * c:(g + 1) * c] for g in range(grp)]
    cat = lambda xs: jnp.concatenate(xs, axis=0)
    chunks3 = lambda x: x.reshape(grp, c, x.shape[-1])

    def group(r0, rev, st_ref):
        f = (fb_ref if rev else ff_ref)[pl.ds(r0, rows), :]
        e_f = jnp.exp(-jnp.abs(f))
        den = 1.0 + e_f
        lf = jnp.minimum(f, 0.0) - jnp.log(den)
        kd = jnp.where(f >= 0.0, e_f, 1.0) / den
        if layer > 0:
            x = log_1m + lf
            mx = jnp.maximum(log_lb, x)
            lf = mx + jnp.log(jnp.exp(log_lb - mx) + jnp.exp(x - mx))
            kd = one_m * kd
        b = cat([_cumsum_mm(tris[rev], x) for x in per_chunk(lf * LOG2_E)])
        tot3 = chunks3(b)[:, 0:1] if rev else chunks3(b)[:, c - 1:c]
        qs = jax.nn.silu(q_ref[pl.ds(r0, rows), :].astype(F32))
        v = i_ref[pl.ds(r0, rows), :]
        q_in = per_chunk((qs * jnp.exp2(b)).astype(BF16))
        k_out = per_chunk((chunks3(kd) * jnp.exp2(tot3 - chunks3(b))).astype(BF16).reshape(rows, HEAD_DIM))
        kv = [_dot_tn(x, y) for x, y in zip(per_chunk(v), k_out)]
        amat = None
        for m in levels:
            bref = _block_ref(b, m, rev)
            up = upper[m]
            qm = jnp.logical_not(up) if rev else up
            km = up if rev else jnp.logical_not(up)
            qt = (qs * jnp.exp2(jnp.where(qm, b - bref, -jnp.inf))).astype(BF16)
            kt = (kd * jnp.exp2(jnp.where(km, bref - b, -jnp.inf))).astype(BF16)
            am = cat([_dot_nt(x, y) for x, y in zip(per_chunk(qt), per_chunk(kt))])
            if m in same_blk:
                am = jnp.where(same_blk[m], am, 0.0)
            amat = am if amat is None else amat + am
        for dl in range(band):
            if dl == 0:
                a = jnp.sum(qs * kd, axis=1, keepdims=True)
            else:
                sh = (rows - dl) if rev else dl
                valid = (pos + dl < band) if rev else (pos >= dl)
                ex = jnp.exp2(b - pltpu.roll(b, sh, 0))
                a = jnp.where(valid, jnp.sum(qs * pltpu.roll(kd, sh, 0) * ex, axis=1, keepdims=True), 0.0)
            term = jnp.where(diags[rev][dl], a, 0.0)
            amat = term if amat is None else amat + term
        o_intra = [_dot(x, y) for x, y in zip(per_chunk(amat.astype(BF16)), per_chunk(v))]
        st = st_ref[...]
        outs = [None] * grp
        for gi in (range(grp - 1, -1, -1) if rev else range(grp)):
            outs[gi] = o_intra[gi] + _dot_nt(q_in[gi], st.astype(BF16))
            st = jnp.exp2(tot3[gi]) * st + kv[gi]
        return cat(outs), st

    def init():
        stf_ref[...] = jnp.zeros_like(stf_ref)
        stb_ref[...] = jnp.zeros_like(stb_ref)

    def step(i, carry):
        rf = pl.multiple_of(i * rows, rows)
        rb = pl.multiple_of((nc // grp - 1 - i) * rows, rows)
        out_f, st_f = group(rf, 0, stf_ref)
        out_b, st_b = group(rb, 1, stb_ref)

        def commit():
            of_ref[pl.ds(rf, rows), :] = out_f
            ob_ref[pl.ds(rb, rows), :] = out_b
            stf_ref[...] = st_f
            stb_ref[...] = st_b

        return carry, commit

    def fin(ci_):
        r0 = pl.multiple_of(ci_ * rows, rows)
        hs = of_ref[pl.ds(r0, rows), :] + ob_ref[pl.ds(r0, rows), :]
        gate = jax.nn.sigmoid(g_ref[pl.ds(r0, rows), :].astype(F32))
        o_ref[pl.ds(r0, rows), :] = (gate * _rms(hs, ng_ref[...])).astype(o_ref.dtype)

    return nc // grp, _Steps(init, step, fin, jnp.int32(0))


def _hgrn_body(layer, *refs):
    n, part = _hgrn_steps(layer, *refs)
    _run_steps(n, part, 1)


def _head_blk(s, off):
    return pl.BlockSpec((s, HEAD_DIM), lambda b, h: (b, off + h))


def _head_row(rows):
    return pl.BlockSpec((rows, HEAD_DIM), lambda b, h: (0, h))


def _mlstm_call_parts(pbf, pfp, lp, s):
    cb, cg = BF_B // HEAD_DIM, FP_GATE // HEAD_DIM
    specs = [_head_blk(s, cb), _head_blk(s, cb + 4), _head_blk(s, cb + 8), _head_blk(s, cb + 12),
             _head_blk(s, cg), _head_row(1)]
    scratch = [pltpu.VMEM((s, HEAD_DIM), F32)] * 2 + [pltpu.VMEM((HEAD_DIM, 2 * HEAD_DIM), F32)] * 2
    return specs, (pbf, pbf, pbf, pbf, pfp, lp["mlstm_norm"]), scratch


def _hgrn_call_parts(pbf, pfp, lp, lb_logits, s):
    cd = BF_D3 // HEAD_DIM
    specs = [_head_blk(s, cd), _head_blk(s, FP_FF // HEAD_DIM), _head_blk(s, FP_FB // HEAD_DIM),
             _head_blk(s, cd + 4), _head_blk(s, cd + 8), _head_row(lb_logits.shape[0]), _head_row(1)]
    scratch = [pltpu.VMEM((s, HEAD_DIM), F32)] * 2 + [pltpu.VMEM((HEAD_DIM, HEAD_DIM), F32)] * 2
    return specs, (pbf, pfp, pfp, pbf, pbf, lb_logits, lp["hgrn_norm"]), scratch


def _mlstm_hgrn(pbf, pfp, lp, lb_logits, layer, bsz, s):
    t = pbf.shape[0]
    m_specs, m_args, m_scr = _mlstm_call_parts(pbf, pfp, lp, s)
    h_specs, h_args, h_scr = _hgrn_call_parts(pbf, pfp, lp, lb_logits, s)
    out_spec = _head_blk(s, 0)
    out_shape = jax.ShapeDtypeStruct((t, MIX_W), BF16)
    m_out = pl.pallas_call(
        _mlstm_body, grid=(bsz, N_HEADS), in_specs=m_specs, out_specs=out_spec, out_shape=out_shape,
        scratch_shapes=m_scr, compiler_params=_cp(("parallel", "parallel")), name="mlstm")(*m_args)
    h_out = pl.pallas_call(
        functools.partial(_hgrn_body, layer), grid=(bsz, N_HEADS), in_specs=h_specs, out_specs=out_spec,
        out_shape=out_shape, scratch_shapes=h_scr, compiler_params=_cp(("parallel", "parallel")),
        name="hgrn2")(*h_args)
    return m_out, h_out


def _shift3(cat, w, bias, ts):
    n = ts + 2 * HALO
    return (bias + w[0:1] * pltpu.roll(cat, 1, 0)[HALO:HALO + ts] + w[1:2] * cat[HALO:HALO + ts]
            + w[2:3] * pltpu.roll(cat, n - 1, 0)[HALO:HALO + ts])


def _hyena_prep_body(x_ref, xp_ref, xn_ref, cw_ref, cb_ref, z_ref, x1_ref):
    i = pl.program_id(1)
    ts = x_ref.shape[0]
    prev = jnp.where(i > 0, xp_ref[...].astype(F32), 0.0)
    nxt = jnp.where(i < pl.num_programs(1) - 1, xn_ref[...].astype(F32), 0.0)
    cat = jnp.concatenate([prev, x_ref[...].astype(F32), nxt], axis=0)
    u = _shift3(cat, cw_ref[...], cb_ref[...], ts)
    z_ref[...] = (u[:, 2 * MIX_W:] * u[:, :MIX_W]).astype(z_ref.dtype)
    x1_ref[...] = u[:, MIX_W:2 * MIX_W].astype(x1_ref.dtype)


def _hyena_prep(pbf, lp, bsz, s):
    t = pbf.shape[0]
    ts = min(512, s)
    per = s // ts
    hb = ts // HALO
    nb = t // HALO
    w = 3 * MIX_W
    return pl.pallas_call(
        _hyena_prep_body,
        grid=(bsz, per),
        in_specs=[
            pl.BlockSpec((ts, w), lambda b, i: (b * per + i, 0)),
            pl.BlockSpec((HALO, w), lambda b, i: (jnp.maximum((b * per + i) * hb - 1, 0), 0)),
            pl.BlockSpec((HALO, w), lambda b, i: (jnp.minimum((b * per + i + 1) * hb, nb - 1), 0)),
            pl.BlockSpec((3, w), lambda b, i: (0, 0)),
            pl.BlockSpec((1, w), lambda b, i: (0, 0)),
        ],
        out_specs=[pl.BlockSpec((ts, MIX_W), lambda b, i: (b * per + i, 0)),
                   pl.BlockSpec((ts, MIX_W), lambda b, i: (b * per + i, 0))],
        out_shape=[jax.ShapeDtypeStruct((t, MIX_W), BF16), jax.ShapeDtypeStruct((t, MIX_W), BF16)],
        compiler_params=_cp(("parallel", "arbitrary")),
        name="hyena_prep",
    )(pbf, pbf, pbf, lp["hyena_conv_w"], lp["hyena_conv_b"])


def _split_bf16(a):
    hi = a.astype(BF16)
    return jnp.stack([hi, (a - hi.astype(F32)).astype(BF16)])


def _dot3(f_hi, f_lo, x):
    x_hi = x.astype(BF16)
    x_lo = (x - x_hi.astype(F32)).astype(BF16)
    return _dot(f_hi, x_hi) + _dot(f_hi, x_lo) + _dot(f_lo, x_hi)


def _dft1_body(x_ref, f_ref, o_ref):
    n1h, r, c = x_ref.shape
    x = x_ref[...].reshape(n1h * r, c)
    o = _dot3(f_ref[0], f_ref[1], x) if f_ref.ndim == 3 else _dot(f_ref[...], x)
    o_ref[...] = o.reshape(o_ref.shape).astype(o_ref.dtype)


def _dft1(x4, kron1, out_dtype):
    bsz, n1h, n2, c = x4.shape
    r = kron1.shape[-1] // n1h
    n1 = kron1.shape[-2] // (2 * r)
    return pl.pallas_call(
        _dft1_body,
        grid=(bsz, n2 // r),
        in_specs=[pl.BlockSpec((None, n1h, r, c), lambda b, j: (b, 0, j, 0)),
                  pl.BlockSpec(kron1.shape, lambda b, j: (0,) * kron1.ndim, pipeline_mode=pl.Buffered(1))],
        out_specs=pl.BlockSpec((None, None, 2, n1, r, c), lambda b, j: (b, j, 0, 0, 0, 0)),
        out_shape=jax.ShapeDtypeStruct((bsz, n2 // r, 2, n1, r, c), out_dtype),
        compiler_params=_cp(("parallel", "parallel")),
        name="dft_stage1",
    )(x4, kron1)


def _spectrum_rows(a_ref, lead, kk):
    nj = a_ref.shape[len(lead)]
    return jnp.concatenate([a_ref[lead + (j, e, kk)] for e in range(2) for j in range(nj)], axis=0)


def _dft2_conv_body(kb, a_ref, f_ref, fi_ref, g_ref, o_ref):
    nj, _, _, r, _ = a_ref.shape
    n2 = nj * r
    for kk in range(kb):
        x = _dot(f_ref[kk], _spectrum_rows(a_ref, (), kk))
        xr, xi = x[:n2], x[n2:]
        gr, gi = g_ref[kk, 0], g_ref[kk, 1]
        y = jnp.concatenate([xr * gr - xi * gi, xr * gi + xi * gr], axis=0).astype(BF16)
        bq = _dot(fi_ref[kk], y).astype(o_ref.dtype)
        for e in range(2):
            for j in range(nj):
                o_ref[j, e, kk] = bq[e * n2 + j * r:e * n2 + (j + 1) * r]


def _dft2_conv(a6, f2, f2i, gspec, kb):
    bsz, nj, _, n1, r, c = a6.shape
    n2 = nj * r
    blk = pl.BlockSpec((None, nj, 2, kb, r, c), lambda k, b: (b, 0, 0, k, 0, 0))
    return pl.pallas_call(
        functools.partial(_dft2_conv_body, kb),
        grid=(n1 // kb, bsz),
        in_specs=[blk,
                  pl.BlockSpec((kb, 2 * n2, 2 * n2), lambda k, b: (k, 0, 0)),
                  pl.BlockSpec((kb, 2 * n2, 2 * n2), lambda k, b: (k, 0, 0)),
                  pl.BlockSpec((kb, 2, n2, c), lambda k, b: (k, 0, 0, 0))],
        out_specs=blk,
        out_shape=jax.ShapeDtypeStruct(a6.shape, BF16),
        compiler_params=_cp(("parallel", "arbitrary")),
        name="dft_stage2_conv",
    )(a6, f2, f2i, gspec)


def _dft2_filter_body(kb, a_ref, f_ref, o_ref):
    n2 = a_ref.shape[1] * a_ref.shape[4]
    k0 = pl.program_id(0) * kb
    for kk in range(kb):
        sgn = (1 - 2 * ((k0 + kk) & 1)).astype(F32)
        xs = [_dot3(f_ref[0, kk], f_ref[1, kk], _spectrum_rows(a_ref, (part,), kk)) for part in range(2)]
        x = xs[0] + sgn * xs[1]
        o_ref[kk, 0] = x[:n2]
        o_ref[kk, 1] = x[n2:]


def _dft2_filter(a6, f2, kb):
    _, nj, _, n1, r, c = a6.shape
    n2 = nj * r
    return pl.pallas_call(
        functools.partial(_dft2_filter_body, kb),
        grid=(n1 // kb,),
        in_specs=[pl.BlockSpec((2, nj, 2, kb, r, c), lambda k: (0, 0, 0, k, 0, 0)),
                  pl.BlockSpec((2, kb, 2 * n2, 2 * n2), lambda k: (0, k, 0, 0))],
        out_specs=pl.BlockSpec((kb, 2, n2, c), lambda k: (k, 0, 0, 0)),
        out_shape=jax.ShapeDtypeStruct((n1, 2, n2, c), F32),
        compiler_params=_cp(("parallel",)),
        name="dft_stage2_filter",
    )(a6, f2)


def _dft3_body(b_ref, f_ref, x1_ref, z_ref, sk_ref, o_ref):
    jb, _, n1, r, c = b_ref.shape
    n1h = o_ref.shape[0]
    for jj in range(jb):
        rows = slice(jj * r, (jj + 1) * r)
        y = _dot(f_ref[...], b_ref[jj].reshape(2 * n1 * r, c)).reshape(n1h, r, c)
        z = z_ref[:, rows, :].astype(F32)
        o_ref[:, rows, :] = (x1_ref[:, rows, :].astype(F32) * (y + sk_ref[...] * z)).astype(o_ref.dtype)


def _dft3(b6, kron3, x1_4, z4, skip):
    bsz, nj, _, n1, r, c = b6.shape
    n1h, n2 = n1 // 2, nj * r
    assert r == kron3.shape[0] // n1h
    jb = math.gcd(nj, max(1, DFT3_ROWS // (2 * n1)))
    row = pl.BlockSpec((None, n1h, jb * r, c), lambda b, j: (b, 0, j, 0))
    return pl.pallas_call(
        _dft3_body,
        grid=(bsz, nj // jb),
        in_specs=[pl.BlockSpec((None, jb, 2, n1, r, c), lambda b, j: (b, j, 0, 0, 0, 0)),
                  pl.BlockSpec(kron3.shape, lambda b, j: (0, 0), pipeline_mode=pl.Buffered(1)),
                  row, row,
                  pl.BlockSpec((1, c), lambda b, j: (0, 0))],
        out_specs=row,
        out_shape=jax.ShapeDtypeStruct((bsz, n1h, n2, c), BF16),
        compiler_params=_cp(("parallel", "parallel")),
        name="dft_stage3",
    )(b6, kron3, x1_4, z4, skip)


def _hyena_filter_body(seq_len, pos_ref, sc_ref, bands_ref, w1t_ref, w1c_ref, w1s_ref, b1_ref, fr1_ref,
                       w2_ref, b2_ref, fr2_ref, w3_ref, rate_ref, o_ref):
    pos = pos_ref[...]
    t = pos * (1.0 / (seq_len - 1))
    arg = (pos * (2.0 * math.pi / seq_len)) * bands_ref[...]
    hdot = lambda a, b: jnp.dot(a, b, preferred_element_type=F32, precision=HIGHEST)
    pre = t * w1t_ref[...] + hdot(jnp.cos(arg), w1c_ref[...]) - hdot(jnp.sin(arg), w1s_ref[...]) + b1_ref[...]
    hid = jnp.sin(fr1_ref[...] * pre)
    hid = jnp.sin(fr2_ref[...] * (hdot(hid, w2_ref[...]) + b2_ref[...]))
    o_ref[...] = hdot(hid, w3_ref[...]) * jnp.exp(-t * rate_ref[...]) * sc_ref[...]


def _hyena_filter(lp, s):
    ts = min(256, s)
    r = 2 * s + ts
    pos = np.concatenate([np.arange(s), s - np.arange(s), np.zeros(ts)]).astype(np.float32)[:, None]
    sc = np.ones((r, 1), np.float32)
    sc[s] = 0.0
    pad = HEAD_DIM - HYENA_HID
    bands = np.zeros((1, HEAD_DIM), np.float32)
    bands[0, :HYENA_BANDS] = np.linspace(1e-4, HYENA_BANDS - 1, HYENA_BANDS)
    rate = np.abs(np.linspace(math.log(HYENA_TARGET) / HYENA_FAST, math.log(HYENA_TARGET) / HYENA_SLOW, MIX_W))
    rate = np.tile(rate, 2).astype(np.float32)[None, :]
    half = lambda i: (0, jnp.where(i < s // ts, 0, 1))
    w1 = lp["hyena_w1"]
    padc = lambda a: jnp.pad(a, ((0, 0), (0, pad)))
    w1t = padc(w1[0:1])
    w1c = jnp.pad(w1[1:1 + HYENA_BANDS], ((0, HEAD_DIM - HYENA_BANDS), (0, pad)))
    w1s = jnp.pad(w1[1 + HYENA_BANDS:], ((0, HEAD_DIM - HYENA_BANDS), (0, pad)))
    w2 = jnp.pad(lp["hyena_w2"], ((0, pad), (0, pad)))
    w3 = jnp.pad(lp["hyena_w3"], ((0, pad), (0, 0)))
    vec = lambda a: padc(a[None, :])
    full = lambda a: pl.BlockSpec(a.shape, lambda i: (0, 0))
    args = [jnp.asarray(bands), w1t, w1c, w1s, vec(lp["hyena_b1"]), vec(lp["hyena_freq1"]), w2,
            vec(lp["hyena_b2"]), vec(lp["hyena_freq2"])]
    out = pl.pallas_call(
        functools.partial(_hyena_filter_body, s),
        grid=(r // ts,),
        in_specs=[pl.BlockSpec((ts, 1), lambda i: (i, 0)), pl.BlockSpec((ts, 1), lambda i: (i, 0))]
                 + [full(a) for a in args]
                 + [pl.BlockSpec((HEAD_DIM, MIX_W), half), pl.BlockSpec((1, MIX_W), half)],
        out_specs=pl.BlockSpec((ts, MIX_W), lambda i: (i, 0)),
        out_shape=jax.ShapeDtypeStruct((r, MIX_W), F32),
        compiler_params=_cp(("parallel",)),
        name="hyena_filter",
    )(jnp.asarray(pos), jnp.asarray(sc), *args, w3, jnp.asarray(rate))
    return out[:s], out[s:2 * s], out[2 * s:2 * s + 1]


def _dft_tables(s):
    n = 2 * s
    n2 = min(DFT_N2, s // 8)
    n1 = n // n2
    n1h = n1 // 2
    two_pi = 2.0 * math.pi
    k1 = jnp.arange(n1, dtype=jnp.int32)
    a1 = (two_pi / n1) * ((k1[:, None] * k1[None, :n1h]) % n1).astype(F32)
    f1 = jnp.concatenate([jnp.cos(a1), -jnp.sin(a1)], axis=0)
    f1i = jnp.concatenate([jnp.cos(a1).T, -jnp.sin(a1).T], axis=1) * (1.0 / n)
    j = jnp.arange(n2, dtype=jnp.int32)
    ph = (j[None, :, None] * j[None, None, :] * n1 + j[None, None, :] * k1[:, None, None]) % n
    a2 = (two_pi / n) * ph.astype(F32)
    tr, ti = jnp.cos(a2), -jnp.sin(a2)
    f2 = jnp.concatenate([jnp.concatenate([tr, -ti], axis=2), jnp.concatenate([ti, tr], axis=2)], axis=1)
    kron = lambda m, r: jnp.kron(m, jnp.eye(r, dtype=F32))
    r32, r16 = min(SUBLANES, n2), min(2 * SUBLANES, n2)
    return dict(n1=n1, n2=n2, f2_split=_split_bf16(f2), f2_bf=f2.astype(BF16),
                f2i_bf=jnp.swapaxes(f2, 1, 2).astype(BF16), kron1_split=_split_bf16(kron(f1, r32)),
                kron1_bf=kron(f1, r16).astype(BF16), kron3_bf=kron(f1i, r16).astype(BF16))


def _hyena_spectrum(lp, s, tab):
    gpos, gneg, hb0 = _hyena_filter(lp, s)
    n1, n2 = tab["n1"], tab["n2"]
    a = _dft1(jnp.stack([gpos, gneg]).reshape(2, n1 // 2, n2, MIX_W), tab["kron1_split"], F32)
    return _dft2_filter(a, tab["f2_split"], min(2, n1)), hb0


def _hyena(pbf, lp, gspec, hb0, tab, bsz, s):
    n1, n2 = tab["n1"], tab["n2"]
    z, x1 = _hyena_prep(pbf, lp, bsz, s)
    z4 = z.reshape(bsz, n1 // 2, n2, MIX_W)
    a = _dft1(z4, tab["kron1_bf"], BF16)
    bq = _dft2_conv(a, tab["f2_bf"], tab["f2i_bf"], gspec, min(8, n1))
    out = _dft3(bq, tab["kron3_bf"], x1.reshape(z4.shape), z4, lp["hyena_skip"][None, :] + hb0)
    return out.reshape(bsz * s, MIX_W)


def _merge_body(h_ref, g0_ref, g1_ref, g2_ref, g3_ref, ba_ref, bb_ref, bc_ref, bd_ref, wb_ref, wo_ref, o_ref):
    merged = None
    for idx, (g_ref, br) in enumerate(zip((g0_ref, g1_ref, g2_ref, g3_ref), (ba_ref, bb_ref, bc_ref, bd_ref))):
        term = jax.nn.sigmoid(g_ref[...].astype(F32)) * _dot(br[...], wb_ref[idx])
        merged = term if merged is None else merged + term
    o_ref[...] = h_ref[...] + _dot(merged.astype(BF16), wo_ref[...])


def _merge(h, pbf, branches, lp):
    t, d = h.shape
    tm = min(512, t)
    gcol = BF_G // d
    row = lambda w: pl.BlockSpec((tm, w), lambda i: (i, 0))
    gate = lambda idx: pl.BlockSpec((tm, d), lambda i: (i, gcol + idx))
    return pl.pallas_call(
        _merge_body,
        grid=(t // tm,),
        in_specs=[row(d), gate(0), gate(1), gate(2), gate(3),
                  row(MIX_W), row(MIX_W), row(MIX_W), row(MIX_W),
                  pl.BlockSpec(lp["w_branch"].shape, lambda i: (0, 0, 0)),
                  pl.BlockSpec(lp["w_out"].shape, lambda i: (0, 0))],
        out_specs=row(d),
        out_shape=jax.ShapeDtypeStruct((t, d), F32),
        compiler_params=_cp(("parallel",)),
        name="merge",
    )(h, pbf, pbf, pbf, pbf, *branches, lp["w_branch"], lp["w_out"])


def _ffn_body(per, final, h_ref, hp_ref, hn_ref, p_ref, gf_ref, wu_ref, cw_ref, cb_ref, wd_ref, gp_ref, wg_ref,
              wp_ref, gl_ref, o_ref):
    i = pl.program_id(0)
    tm = h_ref.shape[0]
    dff = wd_ref.shape[0]
    gf = gf_ref[...]
    h = h_ref[...]
    first = (i % per) == 0
    last = (i % per) == per - 1
    xp = jnp.where(first, 0.0, _rms(hp_ref[...], gf))
    xn = jnp.where(last, 0.0, _rms(hn_ref[...], gf))
    cat = jnp.concatenate([xp, _rms(h, gf), xn], axis=0).astype(BF16)
    cb_w = min(2048, dff)
    acc = jnp.zeros_like(h)
    for cb in range(dff // cb_w):
        lo, lo2 = cb * cb_w, dff + cb * cb_w
        u1 = _shift3(_dot(cat, wu_ref[:, lo:lo + cb_w]), cw_ref[:, lo:lo + cb_w], cb_ref[:, lo:lo + cb_w], tm)
        u2 = _shift3(_dot(cat, wu_ref[:, lo2:lo2 + cb_w]), cw_ref[:, lo2:lo2 + cb_w], cb_ref[:, lo2:lo2 + cb_w], tm)
        acc = acc + _dot((jax.nn.gelu(u1) * u2).astype(BF16), wd_ref[lo:lo + cb_w, :])
    h = h + acc
    gate = jax.nn.sigmoid(_dot(_rms(h, gp_ref[...]).astype(BF16), wg_ref[...]))
    h = h + gate * _dot(p_ref[...].astype(BF16), wp_ref[...])
    if final:
        h = _rms(h, gl_ref[...])
    o_ref[...] = h


def _ffn(h, p, lp, final_norm, s, final):
    t, d = h.shape
    tm = min(512, s)
    per = s // tm
    hb = tm // HALO
    nb = t // HALO
    full = lambda a: pl.BlockSpec(a.shape, lambda i: (0,) * a.ndim)
    row = lambda w: pl.BlockSpec((tm, w), lambda i: (i, 0))
    ws = [lp["norm_ffn"], lp["w_up"], lp["ffn_conv_w"], lp["ffn_conv_b"], lp["w_down"], lp["norm_ple"],
          lp["w_ple_gate"], lp["w_ple"], final_norm]
    return pl.pallas_call(
        functools.partial(_ffn_body, per, final),
        grid=(t // tm,),
        in_specs=[row(d),
                  pl.BlockSpec((HALO, d), lambda i: (jnp.maximum(i * hb - 1, 0), 0)),
                  pl.BlockSpec((HALO, d), lambda i: (jnp.minimum((i + 1) * hb, nb - 1), 0)),
                  row(p.shape[1])] + [full(a) for a in ws],
        out_specs=row(d),
        out_shape=jax.ShapeDtypeStruct((t, d), F32),
        compiler_params=_cp(("parallel",)),
        name="ffn_ple",
    )(h, h, h, p, *ws)


def _prepare_params(prm):
    d_model = prm["w_in"].shape[1]
    depth = prm["w_in"].shape[0]
    off_b = 2 * MIX_W
    off_gate = off_b + 4 * MIX_W
    off_c = off_gate + 4 * N_HEADS
    off_d = off_c + 3 * MIX_W
    off_g = off_d + 5 * MIX_W

    def regroup(a):
        sl = lambda lo, w: a[..., lo:lo + w]
        bf = jnp.concatenate([sl(off_c, 3 * MIX_W), sl(off_d, MIX_W), sl(off_d + 3 * MIX_W, 2 * MIX_W),
                              sl(off_g, 4 * d_model), sl(0, 2 * MIX_W), sl(off_b, 4 * MIX_W)], axis=-1)
        gates = sl(off_gate, 4 * N_HEADS).reshape(a.shape[:-1] + (4, N_HEADS))
        gates = jnp.swapaxes(gates, -1, -2)
        gates = jnp.pad(gates, [(0, 0)] * (gates.ndim - 1) + [(0, HEAD_DIM - 4)])
        fp = jnp.concatenate([sl(off_d + MIX_W, 2 * MIX_W), gates.reshape(a.shape[:-1] + (N_HEADS * HEAD_DIM,))],
                             axis=-1)
        return bf, fp

    w_bf, w_fp = regroup(prm["w_in"])
    b_bf, b_fp = regroup(prm["b_in"][:, None, :])
    gw = jnp.transpose(prm["rglru_w"], (0, 3, 4, 1, 2, 5)).reshape(depth, N_HEADS, HEAD_DIM, 4 * HEAD_DIM)
    gb = prm["rglru_b"].reshape(depth, 2, 2, N_HEADS, HEAD_DIM)
    gb = jnp.transpose(gb, (0, 3, 1, 2, 4)).reshape(depth, N_HEADS, 1, 4 * HEAD_DIM)
    row = lambda a: a[:, None, :]
    out = dict(
        norm_mix=row(prm["norm_mix"]), w_bf=w_bf.astype(BF16), b_bf=b_bf, w_fp=w_fp.astype(BF16), b_fp=b_fp,
        conv_a_w=prm["conv_a_w"], conv_a_b=row(prm["conv_a_b"]), rglru_w=gw.astype(BF16), rglru_b=gb,
        rglru_lam=prm["rglru_lam"], mlstm_norm=row(prm["mlstm_norm"]),
        hyena_conv_w=prm["hyena_conv_w"], hyena_conv_b=row(prm["hyena_conv_b"]),
        hyena_w1=prm["hyena_w1"], hyena_b1=prm["hyena_b1"], hyena_freq1=prm["hyena_freq1"],
        hyena_w2=prm["hyena_w2"], hyena_b2=prm["hyena_b2"], hyena_freq2=prm["hyena_freq2"],
        hyena_w3=prm["hyena_w3"], hyena_skip=prm["hyena_skip"], hgrn_norm=row(prm["hgrn_norm"]),
        w_branch=prm["w_branch"].astype(BF16), w_out=prm["w_out"].astype(BF16),
        norm_ffn=row(prm["norm_ffn"]), w_up=prm["w_up"].astype(BF16), ffn_conv_w=prm["ffn_conv_w"],
        ffn_conv_b=row(prm["ffn_conv_b"]), w_down=prm["w_down"].astype(BF16), norm_ple=row(prm["norm_ple"]),
        w_ple_gate=prm["w_ple_gate"].astype(BF16), w_ple=prm["w_ple"].astype(BF16),
    )
    return out


def _trunk(x, p, prm, lb_logits, final_norm):
    bsz, s, d = x.shape
    depth = p.shape[0]
    h = x.reshape(bsz * s, d)
    tab = _dft_tables(s)
    for i in range(depth):
        lp = {k: v[i] for k, v in prm.items()}
        pbf, pfp = _proj(h, lp["norm_mix"], lp["w_bf"], lp["b_bf"], lp["w_fp"], lp["b_fp"])
        gspec, hb0 = _hyena_spectrum(lp, s, tab)
        br_b, br_d = _mlstm_hgrn(pbf, pfp, lp, lb_logits, i, bsz, s)
        branches = (_rglru(pbf, lp, bsz, s), br_b, _hyena(pbf, lp, gspec, hb0, tab, bsz, s), br_d)
        h = _merge(h, pbf, branches, lp)
        h = _ffn(h, p[i].reshape(bsz * s, -1), lp, final_norm, s, i == depth - 1)
    return h.reshape(bsz, s, d)


def kernel(x_prompt, x_sample, p_prompt, p_sample, norm_mix, w_in, b_in, conv_a_w, conv_a_b, rglru_w, rglru_b, rglru_lam, mlstm_norm, hyena_conv_w, hyena_conv_b, hyena_w1, hyena_b1, hyena_freq1, hyena_w2, hyena_b2, hyena_freq2, hyena_w3, hyena_skip, hgrn_lb_logits, hgrn_norm, w_branch, w_out, norm_ffn, w_up, ffn_conv_w, ffn_conv_b, w_down, norm_ple, w_ple_gate, w_ple, final_norm):
    prm = _prepare_params(dict(
        norm_mix=norm_mix, w_in=w_in, b_in=b_in, conv_a_w=conv_a_w, conv_a_b=conv_a_b, rglru_w=rglru_w,
        rglru_b=rglru_b, rglru_lam=rglru_lam, mlstm_norm=mlstm_norm, hyena_conv_w=hyena_conv_w,
        hyena_conv_b=hyena_conv_b, hyena_w1=hyena_w1, hyena_b1=hyena_b1, hyena_freq1=hyena_freq1,
        hyena_w2=hyena_w2, hyena_b2=hyena_b2, hyena_freq2=hyena_freq2, hyena_w3=hyena_w3,
        hyena_skip=hyena_skip, hgrn_norm=hgrn_norm, w_branch=w_branch, w_out=w_out, norm_ffn=norm_ffn,
        w_up=w_up, ffn_conv_w=ffn_conv_w, ffn_conv_b=ffn_conv_b, w_down=w_down, norm_ple=norm_ple,
        w_ple_gate=w_ple_gate, w_ple=w_ple))
    fn = final_norm[None, :]
    y_prompt = _trunk(x_prompt, p_prompt, prm, hgrn_lb_logits, fn)
    y_sample = _trunk(x_sample, p_sample, prm, hgrn_lb_logits, fn)
    return (y_prompt, y_sample)
```

```python
import collections
import functools
import math

import numpy as np
import jax
import jax.numpy as jnp
from jax import lax
from jax.experimental import pallas as pl
from jax.experimental.pallas import tpu as pltpu

F32 = jnp.float32
BF16 = jnp.bfloat16
HIGHEST = lax.Precision.HIGHEST

N_HEADS = 4
HEAD_DIM = 128
MIX_W = N_HEADS * HEAD_DIM
LRU_C = 8.0
LOG2_E = math.log2(math.e)
EPS = 1e-6
STAB_INIT = -1e30
HYENA_BANDS = 16
HYENA_HID = 64
HYENA_FAST = 0.3
HYENA_SLOW = 1.5
HYENA_TARGET = 1e-2

MLSTM_CHUNK = 128
HGRN_CHUNK = 128
MIXER_GROUP = 4
HGRN_BAND = 4
SCAN_CHUNK = 512
HALO = 16
SUBLANES = 8
PROJ_COLS = 2304
DFT_N2 = 128
DFT_STEP_ROWS = 512
V7X_VMEM_LIMIT = 56 * 1024 * 1024
V7X_MXU_COLS = 256

BF_G, BF_A, BF_B, BF_D3 = 0, 4096, 5120, 7168
BF_COLS = 8704
FP_FF, FP_FB, FP_GATE = 0, 512, 1024
FP_COLS = 1536


def _cp(sem, vmem=V7X_VMEM_LIMIT):
    return pltpu.CompilerParams(dimension_semantics=sem, vmem_limit_bytes=vmem)


def _rms(x, g):
    return x * lax.rsqrt(jnp.mean(x * x, axis=-1, keepdims=True) + EPS) * g


def _log_sigmoid(x):
    return jnp.minimum(x, 0.0) - jnp.log(1.0 + jnp.exp(-jnp.abs(x)))


def _dot(a, b):
    return jnp.dot(a, b, preferred_element_type=F32)


def _dot_nt(a, b):
    return lax.dot_general(a, b, (((1,), (1,)), ((), ())), preferred_element_type=F32)


def _dot_tn(a, b):
    return lax.dot_general(a, b, (((0,), (0,)), ((), ())), preferred_element_type=F32)


def _cumsum_mm(tri, x):
    hi = x.astype(BF16)
    lo = (x - hi.astype(F32)).astype(BF16)
    return _dot(tri, hi) + _dot(tri, lo)


def _col_chunk(n):
    return max(w for w in range(V7X_MXU_COLS, PROJ_COLS + 1, V7X_MXU_COLS) if n % w == 0)


def _proj_body(n_out, h_ref, g_ref, *refs):
    xn = _rms(h_ref[...], g_ref[...]).astype(BF16)
    for k in range(n_out):
        w_ref, b_ref, o_ref = refs[2 * k], refs[2 * k + 1], refs[2 * n_out + k]
        n = w_ref.shape[1]
        tn = _col_chunk(n)
        for lo in range(0, n, tn):
            o_ref[:, lo:lo + tn] = (_dot(xn, w_ref[:, lo:lo + tn]) + b_ref[:, lo:lo + tn]).astype(o_ref.dtype)


def _proj(h, g, weights, out_dtypes):
    t, d = h.shape
    tm = min(256, t)
    once = lambda a: pl.BlockSpec(a.shape, lambda i: (0, 0), pipeline_mode=pl.Buffered(1))
    row = lambda n: pl.BlockSpec((tm, n), lambda i: (i, 0))
    flat = [a for wb in weights for a in wb]
    return pl.pallas_call(
        functools.partial(_proj_body, len(weights)),
        grid=(t // tm,),
        in_specs=[row(d), once(g)] + [once(a) for a in flat],
        out_specs=[row(w.shape[1]) for w, _ in weights],
        out_shape=[jax.ShapeDtypeStruct((t, w.shape[1]), dt) for (w, _), dt in zip(weights, out_dtypes)],
        compiler_params=_cp(("parallel",)),
        name="in_proj",
    )(h, g, *flat)


def _scan8(a, u, pos, reverse):
    n = a.shape[0]
    d = 1
    while d < SUBLANES:
        if reverse:
            m = pos < SUBLANES - d
            sh = n - d
        else:
            m = pos >= d
            sh = d
        a_s = jnp.where(m, pltpu.roll(a, sh, 0), 1.0)
        u_s = jnp.where(m, pltpu.roll(u, sh, 0), 0.0)
        u = a * u_s + u
        a = a * a_s
        d *= 2
    return a, u


def _chain8(a, u, carry, reverse):
    groups = a.shape[0] // SUBLANES
    outs = [None] * groups
    for gi in (range(groups - 1, -1, -1) if reverse else range(groups)):
        lo = gi * SUBLANES
        h = u[lo:lo + SUBLANES] + a[lo:lo + SUBLANES] * carry
        outs[gi] = h
        carry = h[0:1] if reverse else h[SUBLANES - 1:SUBLANES]
    return jnp.concatenate(outs, axis=0), carry


def _rglru_body(xa_ref, ya_ref, cw_ref, cb_ref, gw_ref, gb_ref, lam_ref, o_ref, xc_ref, hf_ref, hb_ref):
    s = xa_ref.shape[0]
    tc = min(SCAN_CHUNK, s)
    nc = s // tc
    n = tc + 2 * HALO
    pos = lax.broadcasted_iota(jnp.int32, (tc, HEAD_DIM), 0) & (SUBLANES - 1)
    cw = cw_ref[...]
    cb = cb_ref[...]
    sp = jax.nn.softplus(-lam_ref[...])

    def gates(xc, d):
        w = gw_ref[0, :, d * 256:(d + 1) * 256]
        g = jax.nn.sigmoid(_dot(xc.astype(BF16), w) + gb_ref[0, :, d * 256:(d + 1) * 256])
        log_a = (-LRU_C) * g[:, :HEAD_DIM] * sp[d:d + 1]
        a = jnp.exp(log_a)
        x2 = 2.0 * log_a
        one_m_a2 = jnp.where(x2 > -0.01, -x2 * (1.0 + 0.5 * x2 * (1.0 + x2 * (1.0 / 3.0))), 1.0 - a * a)
        u = jnp.sqrt(one_m_a2) * (g[:, HEAD_DIM:] * xc)
        return a, u

    def conv(c, carry):
        r0 = pl.multiple_of(c * tc, tc)
        x = xa_ref[pl.ds(r0, tc), :].astype(F32)
        rp = pl.multiple_of(jnp.maximum(r0 - HALO, 0), HALO)
        rn = pl.multiple_of(jnp.minimum(r0 + tc, s - HALO), HALO)
        prev = jnp.where(c > 0, xa_ref[pl.ds(rp, HALO), :].astype(F32), 0.0)
        nxt = jnp.where(c < nc - 1, xa_ref[pl.ds(rn, HALO), :].astype(F32), 0.0)
        cat = jnp.concatenate([prev, x, nxt], axis=0)
        xc_ref[pl.ds(r0, tc), :] = (
            cb + cw[0:1] * pltpu.roll(cat, 1, 0)[HALO:HALO + tc] + cw[1:2] * x
            + cw[2:3] * pltpu.roll(cat, n - 1, 0)[HALO:HALO + tc]
            + cw[3:4] * pltpu.roll(cat, n - 2, 0)[HALO:HALO + tc])
        return carry

    lax.fori_loop(0, nc, conv, 0)

    def scan(i, carry):
        c_f, c_b = carry
        rf = pl.multiple_of(i * tc, tc)
        rb = pl.multiple_of((nc - 1 - i) * tc, tc)
        a, u = gates(xc_ref[pl.ds(rf, tc), :], 0)
        a, u = _scan8(a, u, pos, False)
        h, c_f = _chain8(a, u, c_f, False)
        hf_ref[pl.ds(rf, tc), :] = h
        a, u = gates(xc_ref[pl.ds(rb, tc), :], 1)
        a, u = _scan8(a, u, pos, True)
        h, c_b = _chain8(a, u, c_b, True)
        hb_ref[pl.ds(rb, tc), :] = h
        return c_f, c_b

    zero = jnp.zeros((1, HEAD_DIM), F32)
    lax.fori_loop(0, nc, scan, (zero, zero))

    def fin(c, carry):
        r0 = pl.multiple_of(c * tc, tc)
        y = jax.nn.gelu(ya_ref[pl.ds(r0, tc), :].astype(F32))
        o_ref[pl.ds(r0, tc), :] = ((hf_ref[pl.ds(r0, tc), :] + hb_ref[pl.ds(r0, tc), :]) * y).astype(o_ref.dtype)
        return carry

    lax.fori_loop(0, nc, fin, 0)


def _rglru(pbf, lp, bsz, s):
    t = pbf.shape[0]
    ca, cy = BF_A // HEAD_DIM, (BF_A + MIX_W) // HEAD_DIM
    return pl.pallas_call(
        _rglru_body,
        grid=(bsz, N_HEADS),
        in_specs=[
            pl.BlockSpec((s, HEAD_DIM), lambda b, h: (b, ca + h)),
            pl.BlockSpec((s, HEAD_DIM), lambda b, h: (b, cy + h)),
            pl.BlockSpec((4, HEAD_DIM), lambda b, h: (0, h)),
            pl.BlockSpec((1, HEAD_DIM), lambda b, h: (0, h)),
            pl.BlockSpec((1, HEAD_DIM, 4 * HEAD_DIM), lambda b, h: (h, 0, 0)),
            pl.BlockSpec((1, 1, 4 * HEAD_DIM), lambda b, h: (h, 0, 0)),
            pl.BlockSpec((2, HEAD_DIM), lambda b, h: (0, h)),
        ],
        out_specs=pl.BlockSpec((s, HEAD_DIM), lambda b, h: (b, h)),
        out_shape=jax.ShapeDtypeStruct((t, MIX_W), BF16),
        scratch_shapes=[pltpu.VMEM((s, HEAD_DIM), F32)] * 3,
        compiler_params=_cp(("parallel", "parallel")),
        name="rglru",
    )(pbf, pbf, lp["conv_a_w"], lp["conv_a_b"], lp["rglru_w"], lp["rglru_b"], lp["rglru_lam"])


_Steps = collections.namedtuple("_Steps", "init step fin carry0")


def _run_steps(n, part, unroll):
    part.init()

    def body(i, carry):
        carry, commit = part.step(i, carry)
        commit()
        return carry

    lax.fori_loop(0, n, body, part.carry0, unroll=unroll)

    def fin(c, carry):
        part.fin(c)
        return carry

    lax.fori_loop(0, n, fin, 0)


def _mlstm_steps(q_ref, k_ref, v_ref, og_ref, g_ref, ng_ref, o_ref, hf_ref, hb_ref, stf_ref, stb_ref):
    s = q_ref.shape[0]
    cl = min(MLSTM_CHUNK, s)
    nc = s // cl
    ri = lax.broadcasted_iota(jnp.int32, (cl, cl), 0)
    ci = lax.broadcasted_iota(jnp.int32, (cl, cl), 1)
    masks = (ri >= ci, ri <= ci)
    tris = (masks[0].astype(BF16), masks[1].astype(BF16))
    grp = math.gcd(nc, MIXER_GROUP)
    rows = grp * cl
    mask_rows = tuple(jnp.concatenate([m] * grp, axis=0) for m in masks)
    ones_v = jnp.ones((rows, HEAD_DIM), BF16)
    kscale = HEAD_DIM ** -0.5
    twice = lambda a: jnp.concatenate([a, a], axis=1)
    per_chunk = lambda x: [x[g * cl:(g + 1) * cl] for g in range(grp)]
    cat = lambda xs: jnp.concatenate(xs, axis=0)
    chunks3 = lambda x: x.reshape(grp, cl, x.shape[-1])

    def load(r0, st_ref):
        return (g_ref[pl.ds(r0, rows), :], q_ref[pl.ds(r0, rows), :], k_ref[pl.ds(r0, rows), :],
                v_ref[pl.ds(r0, rows), :], st_ref[...])

    def group(vals, d, m_st):
        g, q, k, v, st = vals
        ig = jnp.broadcast_to(g[:, 2 * d:2 * d + 1], (rows, HEAD_DIM))
        lf = _log_sigmoid(jnp.broadcast_to(g[:, 2 * d + 1:2 * d + 2], (rows, HEAD_DIM)))
        b = cat([_cumsum_mm(tris[d], x) for x in per_chunk(lf)])
        tot3 = chunks3(b)[:, 0:1] if d else chunks3(b)[:, cl - 1:cl]
        c = ig - b
        lw3 = tot3 + chunks3(c)
        m_loc3 = jnp.max(lw3, axis=1, keepdims=True)
        kf = k.astype(F32) * kscale
        kw = (chunks3(kf) * jnp.exp(lw3 - m_loc3)).astype(BF16).reshape(rows, HEAD_DIM)
        kb = kf.astype(BF16)
        v1 = jnp.concatenate([v, ones_v], axis=1)
        cn_loc = [_dot_tn(a, w) for a, w in zip(per_chunk(kw), per_chunk(v1))]
        dm = jnp.where(mask_rows[d], cat([x.T for x in per_chunk(c)]), -jnp.inf)
        m_rel = jnp.max(dm, axis=1, keepdims=True)
        qk = cat([_dot_nt(a, w) for a, w in zip(per_chunk(q), per_chunk(kb))])
        am = (jnp.exp(dm - m_rel) * qk).astype(BF16)
        nd_intra = cat([_dot(a, w) for a, w in zip(per_chunk(am), per_chunk(v1))])
        outs = [None] * grp
        for gi in (range(grp - 1, -1, -1) if d else range(grp)):
            sl = slice(gi * cl, (gi + 1) * cl)
            nd_inter = _dot(q[sl], st.astype(BF16))
            mu = jnp.maximum(m_rel[sl], m_st)
            f_i = jnp.exp(m_rel[sl] - mu)
            f_s = jnp.exp(m_st - mu)
            num = f_i * nd_intra[sl, :HEAD_DIM] + f_s * nd_inter[:, :HEAD_DIM]
            den = f_i * nd_intra[sl, HEAD_DIM:] + f_s * nd_inter[:, HEAD_DIM:]
            outs[gi] = num / jnp.maximum(jnp.abs(den), jnp.exp(-(b[sl] + mu)))
            tot, m_loc = tot3[gi], m_loc3[gi]
            m_new = jnp.maximum(tot + m_st, m_loc)
            st = twice(jnp.exp(tot + m_st - m_new)) * st + twice(jnp.exp(m_loc - m_new)) * cn_loc[gi]
            m_st = m_new
        return cat(outs), m_st, st

    m0 = jnp.full((1, HEAD_DIM), STAB_INIT, F32)

    def init():
        stf_ref[...] = jnp.zeros_like(stf_ref)
        stb_ref[...] = jnp.zeros_like(stb_ref)

    def step(i, carry):
        m_f, m_b = carry
        rf = pl.multiple_of(i * rows, rows)
        rb = pl.multiple_of((nc // grp - 1 - i) * rows, rows)
        vals_f = load(rf, stf_ref)
        vals_b = load(rb, stb_ref)
        out_f, m_f, st_f = group(vals_f, 0, m_f)
        out_b, m_b, st_b = group(vals_b, 1, m_b)

        def commit():
            hf_ref[pl.ds(rf, rows), :] = out_f
            hb_ref[pl.ds(rb, rows), :] = out_b
            stf_ref[...] = st_f
            stb_ref[...] = st_b

        return (m_f, m_b), commit

    def fin(c):
        r0 = pl.multiple_of(c * rows, rows)
        y = _rms(hf_ref[pl.ds(r0, rows), :] + hb_ref[pl.ds(r0, rows), :], ng_ref[...])
        og = jax.nn.sigmoid(og_ref[pl.ds(r0, rows), :].astype(F32))
        o_ref[pl.ds(r0, rows), :] = (og * y).astype(o_ref.dtype)

    return nc // grp, _Steps(init, step, fin, (m0, m0))


def _mlstm_body(*refs):
    n, part = _mlstm_steps(*refs)
    _run_steps(n, part, 1)


def _block_ref(b, m, rev):
    c, w = b.shape
    parts = []
    for blk in range(c // (2 * m)):
        row = blk * 2 * m + (m if rev else m - 1)
        parts.append(jnp.broadcast_to(b[row:row + 1, :], (2 * m, w)))
    return parts[0] if len(parts) == 1 else jnp.concatenate(parts, axis=0)


def _hgrn_steps(layer, q_ref, ff_ref, fb_ref, i_ref, g_ref, lbl_ref, ng_ref, o_ref, of_ref, ob_ref, stf_ref,
                stb_ref):
    s = q_ref.shape[0]
    c = min(HGRN_CHUNK, s)
    nc = s // c
    band = min(HGRN_BAND, c)
    if layer > 0:
        lg = lbl_ref[...]
        e = jnp.exp(lg - jnp.max(lg, axis=0, keepdims=True))
        p = e / jnp.sum(e, axis=0, keepdims=True)
        lb = p[1:2, :]
        for r in range(2, layer + 1):
            lb = lb + p[r:r + 1, :]
        log_lb = jnp.log(lb)
        log_1m = jnp.log(1.0 - lb)
        one_m = 1.0 - lb

    grp = math.gcd(nc, MIXER_GROUP)
    rows = grp * c
    ri = lax.broadcasted_iota(jnp.int32, (c, c), 0)
    ci = lax.broadcasted_iota(jnp.int32, (c, c), 1)
    tris = ((ri >= ci).astype(BF16), (ri <= ci).astype(BF16))
    ri = lax.broadcasted_iota(jnp.int32, (rows, c), 0) & (c - 1)
    ci = lax.broadcasted_iota(jnp.int32, (rows, c), 1)
    rw = lax.broadcasted_iota(jnp.int32, (rows, HEAD_DIM), 0)
    pos = lax.broadcasted_iota(jnp.int32, (rows, 1), 0) & (band - 1)
    levels = []
    m = c // 2
    while m >= band:
        levels.append(m)
        m //= 2
    upper = {m: (rw & (2 * m - 1)) >= m for m in levels}
    same_blk = {m: (ri >> int(math.log2(2 * m))) == (ci >> int(math.log2(2 * m))) for m in levels if 2 * m < c}
    diags = ([ri - ci == dl for dl in range(band)], [ci - ri == dl for dl in range(band)])
    per_chunk = lambda x: [x[g * c:(g + 1) * c] for g in range(grp)]
    cat = lambda xs: jnp.concatenate(xs, axis=0)
    chunks3 = lambda x: x.reshape(grp, c, x.shape[-1])

    def group(r0, rev, st_ref):
        f = (fb_ref if rev else ff_ref)[pl.ds(r0, rows), :]
        e_f = jnp.exp(-jnp.abs(f))
        den = 1.0 + e_f
        lf = jnp.minimum(f, 0.0) - jnp.log(den)
        kd = jnp.where(f >= 0.0, e_f, 1.0) / den
        if layer > 0:
            x = log_1m + lf
            mx = jnp.maximum(log_lb, x)
            lf = mx + jnp.log(jnp.exp(log_lb - mx) + jnp.exp(x - mx))
            kd = one_m * kd
        b = cat([_cumsum_mm(tris[rev], x) for x in per_chunk(lf * LOG2_E)])
        tot3 = chunks3(b)[:, 0:1] if rev else chunks3(b)[:, c - 1:c]
        qs = jax.nn.silu(q_ref[pl.ds(r0, rows), :].astype(F32))
        v = i_ref[pl.ds(r0, rows), :]
        q_in = per_chunk((qs * jnp.exp2(b)).astype(BF16))
        k_out = per_chunk((chunks3(kd) * jnp.exp2(tot3 - chunks3(b))).astype(BF16).reshape(rows, HEAD_DIM))
        kv = [_dot_tn(x, y) for x, y in zip(per_chunk(v), k_out)]
        amat = None
        for m in levels:
            bref = _block_ref(b, m, rev)
            up = upper[m]
            qm = jnp.logical_not(up) if rev else up
            km = up if rev else jnp.logical_not(up)
            qt = (qs * jnp.exp2(jnp.where(qm, b - bref, -jnp.inf))).astype(BF16)
            kt = (kd * jnp.exp2(jnp.where(km, bref - b, -jnp.inf))).astype(BF16)
            am = cat([_dot_nt(x, y) for x, y in zip(per_chunk(qt), per_chunk(kt))])
            if m in same_blk:
                am = jnp.where(same_blk[m], am, 0.0)
            amat = am if amat is None else amat + am
        for dl in range(band):
            if dl == 0:
                a = jnp.sum(qs * kd, axis=1, keepdims=True)
            else:
                sh = (rows - dl) if rev else dl
                valid = (pos + dl < band) if rev else (pos >= dl)
                ex = jnp.exp2(b - pltpu.roll(b, sh, 0))
                a = jnp.where(valid, jnp.sum(qs * pltpu.roll(kd, sh, 0) * ex, axis=1, keepdims=True), 0.0)
            term = jnp.where(diags[rev][dl], a, 0.0)
            amat = term if amat is None else amat + term
        o_intra = [_dot(x, y) for x, y in zip(per_chunk(amat.astype(BF16)), per_chunk(v))]
        st = st_ref[...]
        outs = [None] * grp
        for gi in (range(grp - 1, -1, -1) if rev else range(grp)):
            outs[gi] = o_intra[gi] + _dot_nt(q_in[gi], st.astype(BF16))
            st = jnp.exp2(tot3[gi]) * st + kv[gi]
        return cat(outs), st

    def init():
        stf_ref[...] = jnp.zeros_like(stf_ref)
        stb_ref[...] = jnp.zeros_like(stb_ref)

    def step(i, carry):
        rf = pl.multiple_of(i * rows, rows)
        rb = pl.multiple_of((nc // grp - 1 - i) * rows, rows)
        out_f, st_f = group(rf, 0, stf_ref)
        out_b, st_b = group(rb, 1, stb_ref)

        def commit():
            of_ref[pl.ds(rf, rows), :] = out_f
            ob_ref[pl.ds(rb, rows), :] = out_b
            stf_ref[...] = st_f
            stb_ref[...] = st_b

        return carry, commit

    def fin(ci_):
        r0 = pl.multiple_of(ci_ * rows, rows)
        hs = of_ref[pl.ds(r0, rows), :] + ob_ref[pl.ds(r0, rows), :]
        gate = jax.nn.sigmoid(g_ref[pl.ds(r0, rows), :].astype(F32))
        o_ref[pl.ds(r0, rows), :] = (gate * _rms(hs, ng_ref[...])).astype(o_ref.dtype)

    return nc // grp, _Steps(init, step, fin, jnp.int32(0))


def _hgrn_body(layer, *refs):
    n, part = _hgrn_steps(layer, *refs)
    _run_steps(n, part, 1)


def _head_blk(s, off):
    return pl.BlockSpec((s, HEAD_DIM), lambda b, h: (b, off + h))


def _head_row(rows):
    return pl.BlockSpec((rows, HEAD_DIM), lambda b, h: (0, h))


def _mlstm_call_parts(pbf, pfp, lp, s):
    cb, cg = BF_B // HEAD_DIM, FP_GATE // HEAD_DIM
    specs = [_head_blk(s, cb), _head_blk(s, cb + 4), _head_blk(s, cb + 8), _head_blk(s, cb + 12),
             _head_blk(s, cg), _head_row(1)]
    scratch = [pltpu.VMEM((s, HEAD_DIM), F32)] * 2 + [pltpu.VMEM((HEAD_DIM, 2 * HEAD_DIM), F32)] * 2
    return specs, (pbf, pbf, pbf, pbf, pfp, lp["mlstm_norm"]), scratch


def _hgrn_call_parts(pbf, pfp, lp, lb_logits, s):
    cd = BF_D3 // HEAD_DIM
    specs = [_head_blk(s, cd), _head_blk(s, FP_FF // HEAD_DIM), _head_blk(s, FP_FB // HEAD_DIM),
             _head_blk(s, cd + 4), _head_blk(s, cd + 8), _head_row(lb_logits.shape[0]), _head_row(1)]
    scratch = [pltpu.VMEM((s, HEAD_DIM), F32)] * 2 + [pltpu.VMEM((HEAD_DIM, HEAD_DIM), F32)] * 2
    return specs, (pbf, pfp, pfp, pbf, pbf, lb_logits, lp["hgrn_norm"]), scratch


def _mlstm_hgrn(pbf, pfp, lp, lb_logits, layer, bsz, s):
    t = pbf.shape[0]
    m_specs, m_args, m_scr = _mlstm_call_parts(pbf, pfp, lp, s)
    h_specs, h_args, h_scr = _hgrn_call_parts(pbf, pfp, lp, lb_logits, s)
    out_spec = _head_blk(s, 0)
    out_shape = jax.ShapeDtypeStruct((t, MIX_W), BF16)
    m_out = pl.pallas_call(
        _mlstm_body, grid=(bsz, N_HEADS), in_specs=m_specs, out_specs=out_spec, out_shape=out_shape,
        scratch_shapes=m_scr, compiler_params=_cp(("parallel", "parallel")), name="mlstm")(*m_args)
    h_out = pl.pallas_call(
        functools.partial(_hgrn_body, layer), grid=(bsz, N_HEADS), in_specs=h_specs, out_specs=out_spec,
        out_shape=out_shape, scratch_shapes=h_scr, compiler_params=_cp(("parallel", "parallel")),
        name="hgrn2")(*h_args)
    return m_out, h_out


def _shift3(cat, w, bias, ts):
    n = ts + 2 * HALO
    return (bias + w[0:1] * pltpu.roll(cat, 1, 0)[HALO:HALO + ts] + w[1:2] * cat[HALO:HALO + ts]
            + w[2:3] * pltpu.roll(cat, n - 1, 0)[HALO:HALO + ts])


def _hyena_prep_body(x_ref, xp_ref, xn_ref, cw_ref, cb_ref, z_ref, x1_ref):
    i = pl.program_id(1)
    ts = x_ref.shape[0]
    prev = jnp.where(i > 0, xp_ref[...].astype(F32), 0.0)
    nxt = jnp.where(i < pl.num_programs(1) - 1, xn_ref[...].astype(F32), 0.0)
    cat = jnp.concatenate([prev, x_ref[...].astype(F32), nxt], axis=0)
    u = _shift3(cat, cw_ref[...], cb_ref[...], ts)
    z_ref[...] = (u[:, 2 * MIX_W:] * u[:, :MIX_W]).astype(z_ref.dtype)
    x1_ref[...] = u[:, MIX_W:2 * MIX_W].astype(x1_ref.dtype)


def _hyena_prep(pbf, lp, bsz, s):
    t = pbf.shape[0]
    ts = min(512, s)
    per = s // ts
    hb = ts // HALO
    nb = t // HALO
    w = 3 * MIX_W
    return pl.pallas_call(
        _hyena_prep_body,
        grid=(bsz, per),
        in_specs=[
            pl.BlockSpec((ts, w), lambda b, i: (b * per + i, 0)),
            pl.BlockSpec((HALO, w), lambda b, i: (jnp.maximum((b * per + i) * hb - 1, 0), 0)),
            pl.BlockSpec((HALO, w), lambda b, i: (jnp.minimum((b * per + i + 1) * hb, nb - 1), 0)),
            pl.BlockSpec((3, w), lambda b, i: (0, 0)),
            pl.BlockSpec((1, w), lambda b, i: (0, 0)),
        ],
        out_specs=[pl.BlockSpec((ts, MIX_W), lambda b, i: (b * per + i, 0)),
                   pl.BlockSpec((ts, MIX_W), lambda b, i: (b * per + i, 0))],
        out_shape=[jax.ShapeDtypeStruct((t, MIX_W), BF16), jax.ShapeDtypeStruct((t, MIX_W), BF16)],
        compiler_params=_cp(("parallel", "arbitrary")),
        name="hyena_prep",
    )(pbf, pbf, pbf, lp["hyena_conv_w"], lp["hyena_conv_b"])


def _split_bf16(a):
    hi = a.astype(BF16)
    return jnp.stack([hi, (a - hi.astype(F32)).astype(BF16)])


def _dot3(f_hi, f_lo, x):
    x_hi = x.astype(BF16)
    x_lo = (x - x_hi.astype(F32)).astype(BF16)
    return _dot(f_hi, x_hi) + _dot(f_hi, x_lo) + _dot(f_lo, x_hi)


def _dft1_body(x_ref, f_ref, o_ref):
    jb, _, n1, r, c = o_ref.shape
    n1h = x_ref.shape[0]
    for jj in range(jb):
        x = x_ref[:, jj * r:(jj + 1) * r, :].reshape(n1h * r, c)
        o = _dot3(f_ref[0], f_ref[1], x) if f_ref.ndim == 3 else _dot(f_ref[...], x)
        o_ref[jj] = o.reshape(2, n1, r, c).astype(o_ref.dtype)


def _dft1(x4, kron1, out_dtype):
    bsz, n1h, n2, c = x4.shape
    r = kron1.shape[-1] // n1h
    n1 = kron1.shape[-2] // (2 * r)
    nj = n2 // r
    jb = math.gcd(nj, max(1, DFT_STEP_ROWS // (2 * n1)))
    return pl.pallas_call(
        _dft1_body,
        grid=(bsz, nj // jb),
        in_specs=[pl.BlockSpec((None, n1h, jb * r, c), lambda b, j: (b, 0, j, 0)),
                  pl.BlockSpec(kron1.shape, lambda b, j: (0,) * kron1.ndim, pipeline_mode=pl.Buffered(1))],
        out_specs=pl.BlockSpec((None, jb, 2, n1, r, c), lambda b, j: (b, j, 0, 0, 0, 0)),
        out_shape=jax.ShapeDtypeStruct((bsz, n2 // r, 2, n1, r, c), out_dtype),
        compiler_params=_cp(("parallel", "parallel")),
        name="dft_stage1",
    )(x4, kron1)


def _spectrum_rows(a_ref, lead, kk):
    nj = a_ref.shape[len(lead)]
    return jnp.concatenate([a_ref[lead + (j, e, kk)] for e in range(2) for j in range(nj)], axis=0)


def _dft2_conv_body(kb, a_ref, f_ref, fi_ref, g_ref, o_ref):
    nj, _, _, r, _ = a_ref.shape
    n2 = nj * r
    for kk in range(kb):
        x = _dot(f_ref[kk], _spectrum_rows(a_ref, (), kk))
        xr, xi = x[:n2], x[n2:]
        gr, gi = g_ref[kk, 0], g_ref[kk, 1]
        y = jnp.concatenate([xr * gr - xi * gi, xr * gi + xi * gr], axis=0).astype(BF16)
        bq = _dot(fi_ref[kk], y).astype(o_ref.dtype)
        for e in range(2):
            for j in range(nj):
                o_ref[j, e, kk] = bq[e * n2 + j * r:e * n2 + (j + 1) * r]


def _dft2_conv(a6, f2, f2i, gspec, kb):
    bsz, nj, _, n1, r, c = a6.shape
    n2 = nj * r
    blk = pl.BlockSpec((None, nj, 2, kb, r, c), lambda k, b: (b, 0, 0, k, 0, 0))
    return pl.pallas_call(
        functools.partial(_dft2_conv_body, kb),
        grid=(n1 // kb, bsz),
        in_specs=[blk,
                  pl.BlockSpec((kb, 2 * n2, 2 * n2), lambda k, b: (k, 0, 0)),
                  pl.BlockSpec((kb, 2 * n2, 2 * n2), lambda k, b: (k, 0, 0)),
                  pl.BlockSpec((kb, 2, n2, c), lambda k, b: (k, 0, 0, 0))],
        out_specs=blk,
        out_shape=jax.ShapeDtypeStruct(a6.shape, BF16),
        compiler_params=_cp(("parallel", "arbitrary")),
        name="dft_stage2_conv",
    )(a6, f2, f2i, gspec)


def _dft2_filter_body(kb, a_ref, f_ref, o_ref):
    n2 = a_ref.shape[1] * a_ref.shape[4]
    k0 = pl.program_id(0) * kb
    for kk in range(kb):
        sgn = (1 - 2 * ((k0 + kk) & 1)).astype(F32)
        xs = [_dot3(f_ref[0, kk], f_ref[1, kk], _spectrum_rows(a_ref, (part,), kk)) for part in range(2)]
        x = xs[0] + sgn * xs[1]
        o_ref[kk, 0] = x[:n2]
        o_ref[kk, 1] = x[n2:]


def _dft2_filter(a6, f2, kb):
    _, nj, _, n1, r, c = a6.shape
    n2 = nj * r
    return pl.pallas_call(
        functools.partial(_dft2_filter_body, kb),
        grid=(n1 // kb,),
        in_specs=[pl.BlockSpec((2, nj, 2, kb, r, c), lambda k: (0, 0, 0, k, 0, 0)),
                  pl.BlockSpec((2, kb, 2 * n2, 2 * n2), lambda k: (0, k, 0, 0))],
        out_specs=pl.BlockSpec((kb, 2, n2, c), lambda k: (k, 0, 0, 0)),
        out_shape=jax.ShapeDtypeStruct((n1, 2, n2, c), F32),
        compiler_params=_cp(("parallel",)),
        name="dft_stage2_filter",
    )(a6, f2)


def _dft3_body(b_ref, f_ref, x1_ref, z_ref, sk_ref, o_ref):
    jb, _, n1, r, c = b_ref.shape
    n1h = o_ref.shape[0]
    for jj in range(jb):
        rows = slice(jj * r, (jj + 1) * r)
        y = _dot(f_ref[...], b_ref[jj].reshape(2 * n1 * r, c)).reshape(n1h, r, c)
        z = z_ref[:, rows, :].astype(F32)
        o_ref[:, rows, :] = (x1_ref[:, rows, :].astype(F32) * (y + sk_ref[...] * z)).astype(o_ref.dtype)


def _dft3(b6, kron3, x1_4, z4, skip):
    bsz, nj, _, n1, r, c = b6.shape
    n1h, n2 = n1 // 2, nj * r
    assert r == kron3.shape[0] // n1h
    jb = math.gcd(nj, max(1, DFT_STEP_ROWS // (2 * n1)))
    row = pl.BlockSpec((None, n1h, jb * r, c), lambda b, j: (b, 0, j, 0))
    return pl.pallas_call(
        _dft3_body,
        grid=(bsz, nj // jb),
        in_specs=[pl.BlockSpec((None, jb, 2, n1, r, c), lambda b, j: (b, j, 0, 0, 0, 0)),
                  pl.BlockSpec(kron3.shape, lambda b, j: (0, 0), pipeline_mode=pl.Buffered(1)),
                  row, row,
                  pl.BlockSpec((1, c), lambda b, j: (0, 0))],
        out_specs=row,
        out_shape=jax.ShapeDtypeStruct((bsz, n1h, n2, c), BF16),
        compiler_params=_cp(("parallel", "parallel")),
        name="dft_stage3",
    )(b6, kron3, x1_4, z4, skip)


def _hyena_filter_body(seq_len, pos_ref, sc_ref, bands_ref, w1t_ref, w1c_ref, w1s_ref, b1_ref, fr1_ref,
                       w2_ref, b2_ref, fr2_ref, w3_ref, rate_ref, o_ref):
    pos = pos_ref[...]
    t = pos * (1.0 / (seq_len - 1))
    arg = (pos * (2.0 * math.pi / seq_len)) * bands_ref[...]
    hdot = lambda a, b: jnp.dot(a, b, preferred_element_type=F32, precision=HIGHEST)
    pre = t * w1t_ref[...] + hdot(jnp.cos(arg), w1c_ref[...]) - hdot(jnp.sin(arg), w1s_ref[...]) + b1_ref[...]
    hid = jnp.sin(fr1_ref[...] * pre)
    hid = jnp.sin(fr2_ref[...] * (hdot(hid, w2_ref[...]) + b2_ref[...]))
    o_ref[...] = hdot(hid, w3_ref[...]) * jnp.exp(-t * rate_ref[...]) * sc_ref[...]


def _hyena_filter(lp, s):
    ts = min(256, s)
    r = 2 * s + ts
    pos = np.concatenate([np.arange(s), s - np.arange(s), np.zeros(ts)]).astype(np.float32)[:, None]
    sc = np.ones((r, 1), np.float32)
    sc[s] = 0.0
    pad = HEAD_DIM - HYENA_HID
    bands = np.zeros((1, HEAD_DIM), np.float32)
    bands[0, :HYENA_BANDS] = np.linspace(1e-4, HYENA_BANDS - 1, HYENA_BANDS)
    rate = np.abs(np.linspace(math.log(HYENA_TARGET) / HYENA_FAST, math.log(HYENA_TARGET) / HYENA_SLOW, MIX_W))
    rate = np.tile(rate, 2).astype(np.float32)[None, :]
    half = lambda i: (0, jnp.where(i < s // ts, 0, 1))
    w1 = lp["hyena_w1"]
    padc = lambda a: jnp.pad(a, ((0, 0), (0, pad)))
    w1t = padc(w1[0:1])
    w1c = jnp.pad(w1[1:1 + HYENA_BANDS], ((0, HEAD_DIM - HYENA_BANDS), (0, pad)))
    w1s = jnp.pad(w1[1 + HYENA_BANDS:], ((0, HEAD_DIM - HYENA_BANDS), (0, pad)))
    w2 = jnp.pad(lp["hyena_w2"], ((0, pad), (0, pad)))
    w3 = jnp.pad(lp["hyena_w3"], ((0, pad), (0, 0)))
    vec = lambda a: padc(a[None, :])
    full = lambda a: pl.BlockSpec(a.shape, lambda i: (0, 0))
    args = [jnp.asarray(bands), w1t, w1c, w1s, vec(lp["hyena_b1"]), vec(lp["hyena_freq1"]), w2,
            vec(lp["hyena_b2"]), vec(lp["hyena_freq2"])]
    out = pl.pallas_call(
        functools.partial(_hyena_filter_body, s),
        grid=(r // ts,),
        in_specs=[pl.BlockSpec((ts, 1), lambda i: (i, 0)), pl.BlockSpec((ts, 1), lambda i: (i, 0))]
                 + [full(a) for a in args]
                 + [pl.BlockSpec((HEAD_DIM, MIX_W), half), pl.BlockSpec((1, MIX_W), half)],
        out_specs=pl.BlockSpec((ts, MIX_W), lambda i: (i, 0)),
        out_shape=jax.ShapeDtypeStruct((r, MIX_W), F32),
        compiler_params=_cp(("parallel",)),
        name="hyena_filter",
    )(jnp.asarray(pos), jnp.asarray(sc), *args, w3, jnp.asarray(rate))
    return out[:s], out[s:2 * s], out[2 * s:2 * s + 1]


def _dft_tables(s):
    n = 2 * s
    n2 = min(DFT_N2, s // 8)
    n1 = n // n2
    n1h = n1 // 2
    two_pi = 2.0 * math.pi
    k1 = jnp.arange(n1, dtype=jnp.int32)
    a1 = (two_pi / n1) * ((k1[:, None] * k1[None, :n1h]) % n1).astype(F32)
    f1 = jnp.concatenate([jnp.cos(a1), -jnp.sin(a1)], axis=0)
    f1i = jnp.concatenate([jnp.cos(a1).T, -jnp.sin(a1).T], axis=1) * (1.0 / n)
    j = jnp.arange(n2, dtype=jnp.int32)
    ph = (j[None, :, None] * j[None, None, :] * n1 + j[None, None, :] * k1[:, None, None]) % n
    a2 = (two_pi / n) * ph.astype(F32)
    tr, ti = jnp.cos(a2), -jnp.sin(a2)
    f2 = jnp.concatenate([jnp.concatenate([tr, -ti], axis=2), jnp.concatenate([ti, tr], axis=2)], axis=1)
    kron = lambda m, r: jnp.kron(m, jnp.eye(r, dtype=F32))
    r32, r16 = min(SUBLANES, n2), min(2 * SUBLANES, n2)
    return dict(n1=n1, n2=n2, f2_split=_split_bf16(f2), f2_bf=f2.astype(BF16),
                f2i_bf=jnp.swapaxes(f2, 1, 2).astype(BF16), kron1_split=_split_bf16(kron(f1, r32)),
                kron1_bf=kron(f1, r16).astype(BF16), kron3_bf=kron(f1i, r16).astype(BF16))


def _hyena_spectrum(lp, s, tab):
    gpos, gneg, hb0 = _hyena_filter(lp, s)
    n1, n2 = tab["n1"], tab["n2"]
    a = _dft1(jnp.stack([gpos, gneg]).reshape(2, n1 // 2, n2, MIX_W), tab["kron1_split"], F32)
    return _dft2_filter(a, tab["f2_split"], min(2, n1)), hb0


def _hyena(pbf, lp, gspec, hb0, tab, bsz, s):
    n1, n2 = tab["n1"], tab["n2"]
    z, x1 = _hyena_prep(pbf, lp, bsz, s)
    z4 = z.reshape(bsz, n1 // 2, n2, MIX_W)
    a = _dft1(z4, tab["kron1_bf"], BF16)
    bq = _dft2_conv(a, tab["f2_bf"], tab["f2i_bf"], gspec, min(8, n1))
    out = _dft3(bq, tab["kron3_bf"], x1.reshape(z4.shape), z4, lp["hyena_skip"][None, :] + hb0)
    return out.reshape(bsz * s, MIX_W)


def _merge_body(h_ref, g0_ref, g1_ref, g2_ref, g3_ref, ba_ref, bb_ref, bc_ref, bd_ref, wb_ref, wo_ref, o_ref):
    merged = None
    for idx, (g_ref, br) in enumerate(zip((g0_ref, g1_ref, g2_ref, g3_ref), (ba_ref, bb_ref, bc_ref, bd_ref))):
        term = jax.nn.sigmoid(g_ref[...].astype(F32)) * _dot(br[...], wb_ref[idx])
        merged = term if merged is None else merged + term
    o_ref[...] = h_ref[...] + _dot(merged.astype(BF16), wo_ref[...])


def _merge(h, pbf, branches, lp):
    t, d = h.shape
    tm = min(512, t)
    gcol = BF_G // d
    row = lambda w: pl.BlockSpec((tm, w), lambda i: (i, 0))
    gate = lambda idx: pl.BlockSpec((tm, d), lambda i: (i, gcol + idx))
    return pl.pallas_call(
        _merge_body,
        grid=(t // tm,),
        in_specs=[row(d), gate(0), gate(1), gate(2), gate(3),
                  row(MIX_W), row(MIX_W), row(MIX_W), row(MIX_W),
                  pl.BlockSpec(lp["w_branch"].shape, lambda i: (0, 0, 0)),
                  pl.BlockSpec(lp["w_out"].shape, lambda i: (0, 0))],
        out_specs=row(d),
        out_shape=jax.ShapeDtypeStruct((t, d), F32),
        compiler_params=_cp(("parallel",)),
        name="merge",
    )(h, pbf, pbf, pbf, pbf, *branches, lp["w_branch"], lp["w_out"])


def _ffn_body(per, final, h_ref, hp_ref, hn_ref, p_ref, gf_ref, wu_ref, cw_ref, cb_ref, wd_ref, gp_ref, wg_ref,
              wp_ref, gl_ref, o_ref):
    i = pl.program_id(0)
    tm = h_ref.shape[0]
    dff = wd_ref.shape[0]
    gf = gf_ref[...]
    h = h_ref[...]
    first = (i % per) == 0
    last = (i % per) == per - 1
    xp = jnp.where(first, 0.0, _rms(hp_ref[...], gf))
    xn = jnp.where(last, 0.0, _rms(hn_ref[...], gf))
    cat = jnp.concatenate([xp, _rms(h, gf), xn], axis=0).astype(BF16)
    cb_w = min(2048, dff)
    acc = jnp.zeros_like(h)
    for cb in range(dff // cb_w):
        lo, lo2 = cb * cb_w, dff + cb * cb_w
        u1 = _shift3(_dot(cat, wu_ref[:, lo:lo + cb_w]), cw_ref[:, lo:lo + cb_w], cb_ref[:, lo:lo + cb_w], tm)
        u2 = _shift3(_dot(cat, wu_ref[:, lo2:lo2 + cb_w]), cw_ref[:, lo2:lo2 + cb_w], cb_ref[:, lo2:lo2 + cb_w], tm)
        acc = acc + _dot((jax.nn.gelu(u1) * u2).astype(BF16), wd_ref[lo:lo + cb_w, :])
    h = h + acc
    gate = jax.nn.sigmoid(_dot(_rms(h, gp_ref[...]).astype(BF16), wg_ref[...]))
    h = h + gate * _dot(p_ref[...].astype(BF16), wp_ref[...])
    if final:
        h = _rms(h, gl_ref[...])
    o_ref[...] = h


def _ffn(h, p, lp, final_norm, s, final):
    t, d = h.shape
    tm = min(512, s)
    per = s // tm
    hb = tm // HALO
    nb = t // HALO
    full = lambda a: pl.BlockSpec(a.shape, lambda i: (0,) * a.ndim)
    row = lambda w: pl.BlockSpec((tm, w), lambda i: (i, 0))
    ws = [lp["norm_ffn"], lp["w_up"], lp["ffn_conv_w"], lp["ffn_conv_b"], lp["w_down"], lp["norm_ple"],
          lp["w_ple_gate"], lp["w_ple"], final_norm]
    return pl.pallas_call(
        functools.partial(_ffn_body, per, final),
        grid=(t // tm,),
        in_specs=[row(d),
                  pl.BlockSpec((HALO, d), lambda i: (jnp.maximum(i * hb - 1, 0), 0)),
                  pl.BlockSpec((HALO, d), lambda i: (jnp.minimum((i + 1) * hb, nb - 1), 0)),
                  row(p.shape[1])] + [full(a) for a in ws],
        out_specs=row(d),
        out_shape=jax.ShapeDtypeStruct((t, d), F32),
        compiler_params=_cp(("parallel",)),
        name="ffn_ple",
    )(h, h, h, p, *ws)


def _prepare_params(prm):
    d_model = prm["w_in"].shape[1]
    depth = prm["w_in"].shape[0]
    off_b = 2 * MIX_W
    off_gate = off_b + 4 * MIX_W
    off_c = off_gate + 4 * N_HEADS
    off_d = off_c + 3 * MIX_W
    off_g = off_d + 5 * MIX_W

    def regroup(a):
        sl = lambda lo, w: a[..., lo:lo + w]
        bf = jnp.concatenate([sl(off_g, 4 * d_model), sl(0, 2 * MIX_W), sl(off_b, 4 * MIX_W), sl(off_d, MIX_W),
                              sl(off_d + 3 * MIX_W, 2 * MIX_W)], axis=-1)
        gates = sl(off_gate, 4 * N_HEADS).reshape(a.shape[:-1] + (4, N_HEADS))
        gates = jnp.swapaxes(gates, -1, -2)
        gates = jnp.pad(gates, [(0, 0)] * (gates.ndim - 1) + [(0, HEAD_DIM - 4)])
        fp = jnp.concatenate([sl(off_d + MIX_W, 2 * MIX_W), gates.reshape(a.shape[:-1] + (N_HEADS * HEAD_DIM,))],
                             axis=-1)
        return bf, sl(off_c, 3 * MIX_W), fp

    w_bf, w_hy, w_fp = regroup(prm["w_in"])
    b_bf, b_hy, b_fp = regroup(prm["b_in"][:, None, :])
    gw = jnp.transpose(prm["rglru_w"], (0, 3, 4, 1, 2, 5)).reshape(depth, N_HEADS, HEAD_DIM, 4 * HEAD_DIM)
    gb = prm["rglru_b"].reshape(depth, 2, 2, N_HEADS, HEAD_DIM)
    gb = jnp.transpose(gb, (0, 3, 1, 2, 4)).reshape(depth, N_HEADS, 1, 4 * HEAD_DIM)
    row = lambda a: a[:, None, :]
    out = dict(
        norm_mix=row(prm["norm_mix"]), w_bf=w_bf.astype(BF16), b_bf=b_bf, w_hy=w_hy.astype(BF16), b_hy=b_hy,
        w_fp=w_fp.astype(BF16), b_fp=b_fp,
        conv_a_w=prm["conv_a_w"], conv_a_b=row(prm["conv_a_b"]), rglru_w=gw.astype(BF16), rglru_b=gb,
        rglru_lam=prm["rglru_lam"], mlstm_norm=row(prm["mlstm_norm"]),
        hyena_conv_w=prm["hyena_conv_w"], hyena_conv_b=row(prm["hyena_conv_b"]),
        hyena_w1=prm["hyena_w1"], hyena_b1=prm["hyena_b1"], hyena_freq1=prm["hyena_freq1"],
        hyena_w2=prm["hyena_w2"], hyena_b2=prm["hyena_b2"], hyena_freq2=prm["hyena_freq2"],
        hyena_w3=prm["hyena_w3"], hyena_skip=prm["hyena_skip"], hgrn_norm=row(prm["hgrn_norm"]),
        w_branch=prm["w_branch"].astype(BF16), w_out=prm["w_out"].astype(BF16),
        norm_ffn=row(prm["norm_ffn"]), w_up=prm["w_up"].astype(BF16), ffn_conv_w=prm["ffn_conv_w"],
        ffn_conv_b=row(prm["ffn_conv_b"]), w_down=prm["w_down"].astype(BF16), norm_ple=row(prm["norm_ple"]),
        w_ple_gate=prm["w_ple_gate"].astype(BF16), w_ple=prm["w_ple"].astype(BF16),
    )
    return out


def _trunk(x, p, prm, lb_logits, final_norm):
    bsz, s, d = x.shape
    depth = p.shape[0]
    h = x.reshape(bsz * s, d)
    tab = _dft_tables(s)
    for i in range(depth):
        lp = {k: v[i] for k, v in prm.items()}
        pbf, phy, pfp = _proj(h, lp["norm_mix"], [(lp["w_bf"], lp["b_bf"]), (lp["w_hy"], lp["b_hy"]),
                                                  (lp["w_fp"], lp["b_fp"])], (BF16, BF16, F32))
        gspec, hb0 = _hyena_spectrum(lp, s, tab)
        br_b, br_d = _mlstm_hgrn(pbf, pfp, lp, lb_logits, i, bsz, s)
        branches = (_rglru(pbf, lp, bsz, s), br_b, _hyena(phy, lp, gspec, hb0, tab, bsz, s), br_d)
        h = _merge(h, pbf, branches, lp)
        h = _ffn(h, p[i].reshape(bsz * s, -1), lp, final_norm, s, i == depth - 1)
    return h.reshape(bsz, s, d)


def kernel(x_prompt, x_sample, p_prompt, p_sample, norm_mix, w_in, b_in, conv_a_w, conv_a_b, rglru_w, rglru_b, rglru_lam, mlstm_norm, hyena_conv_w, hyena_conv_b, hyena_w1, hyena_b1, hyena_freq1, hyena_w2, hyena_b2, hyena_freq2, hyena_w3, hyena_skip, hgrn_lb_logits, hgrn_norm, w_branch, w_out, norm_ffn, w_up, ffn_conv_w, ffn_conv_b, w_down, norm_ple, w_ple_gate, w_ple, final_norm):
    prm = _prepare_params(dict(
        norm_mix=norm_mix, w_in=w_in, b_in=b_in, conv_a_w=conv_a_w, conv_a_b=conv_a_b, rglru_w=rglru_w,
        rglru_b=rglru_b, rglru_lam=rglru_lam, mlstm_norm=mlstm_norm, hyena_conv_w=hyena_conv_w,
        hyena_conv_b=hyena_conv_b, hyena_w1=hyena_w1, hyena_b1=hyena_b1, hyena_freq1=hyena_freq1,
        hyena_w2=hyena_w2, hyena_b2=hyena_b2, hyena_freq2=hyena_freq2, hyena_w3=hyena_w3,
        hyena_skip=hyena_skip, hgrn_norm=hgrn_norm, w_branch=w_branch, w_out=w_out, norm_ffn=norm_ffn,
        w_up=w_up, ffn_conv_w=ffn_conv_w, ffn_conv_b=ffn_conv_b, w_down=w_down, norm_ple=norm_ple,
        w_ple_gate=w_ple_gate, w_ple=w_ple))
    fn = final_norm[None, :]
    y_prompt = _trunk(x_prompt, p_prompt, prm, hgrn_lb_logits, fn)
    y_sample = _trunk(x_sample, p_sample, prm, hgrn_lb_logits, fn)
    return (y_prompt, y_sample)
```

```python
import collections
import functools
import math

import numpy as np
import jax
import jax.numpy as jnp
from jax import lax
from jax.experimental import pallas as pl
from jax.experimental.pallas import tpu as pltpu

F32 = jnp.float32
BF16 = jnp.bfloat16
HIGHEST = lax.Precision.HIGHEST

N_HEADS = 4
HEAD_DIM = 128
MIX_W = N_HEADS * HEAD_DIM
LRU_C = 8.0
LOG2_E = math.log2(math.e)
SQRT_GUARD = 1e-30
EPS = 1e-6
STAB_INIT = -1e30
HYENA_BANDS = 16
HYENA_HID = 64
HYENA_FAST = 0.3
HYENA_SLOW = 1.5
HYENA_TARGET = 1e-2

MLSTM_CHUNK = 128
HGRN_CHUNK = 128
MIXER_GROUP = 4
MLSTM_BIG_GROUP_SEQ = 4096
HGRN_BAND = 4
SCAN_CHUNK = 512
HALO = 16
SUBLANES = 8
PROJ_COLS = 2304
DFT_N2 = 128
DFT_STEP_ROWS = 512
V7X_VMEM_LIMIT = 56 * 1024 * 1024
V7X_MXU_COLS = 256

BF_G, BF_A, BF_B, BF_D3 = 0, 4096, 5120, 7168
BF_COLS = 8704
FP_FF, FP_FB, FP_GATE = 0, 512, 1024
FP_COLS = 1536


def _cp(sem, vmem=V7X_VMEM_LIMIT):
    return pltpu.CompilerParams(dimension_semantics=sem, vmem_limit_bytes=vmem)


def _rms(x, g):
    return x * lax.rsqrt(jnp.mean(x * x, axis=-1, keepdims=True) + EPS) * g


def _log_sigmoid(x):
    return jnp.minimum(x, 0.0) - jnp.log(1.0 + jnp.exp(-jnp.abs(x)))


def _dot(a, b):
    return jnp.dot(a, b, preferred_element_type=F32)


def _dot_nt(a, b):
    return lax.dot_general(a, b, (((1,), (1,)), ((), ())), preferred_element_type=F32)


def _dot_tn(a, b):
    return lax.dot_general(a, b, (((0,), (0,)), ((), ())), preferred_element_type=F32)


def _cumsum_mm(tri, x):
    hi = x.astype(BF16)
    lo = (x - hi.astype(F32)).astype(BF16)
    return _dot(tri, hi) + _dot(tri, lo)


def _col_chunk(n):
    return max(w for w in range(V7X_MXU_COLS, PROJ_COLS + 1, V7X_MXU_COLS) if n % w == 0)


def _proj_body(n_out, h_ref, g_ref, *refs):
    xn = _rms(h_ref[...], g_ref[...]).astype(BF16)
    for k in range(n_out):
        w_ref, b_ref, o_ref = refs[2 * k], refs[2 * k + 1], refs[2 * n_out + k]
        n = w_ref.shape[1]
        tn = _col_chunk(n)
        for lo in range(0, n, tn):
            o_ref[:, lo:lo + tn] = (_dot(xn, w_ref[:, lo:lo + tn]) + b_ref[:, lo:lo + tn]).astype(o_ref.dtype)


def _proj(h, g, weights, out_dtypes):
    t, d = h.shape
    tm = min(256, t)
    once = lambda a: pl.BlockSpec(a.shape, lambda i: (0, 0), pipeline_mode=pl.Buffered(1))
    row = lambda n: pl.BlockSpec((tm, n), lambda i: (i, 0))
    flat = [a for wb in weights for a in wb]
    return pl.pallas_call(
        functools.partial(_proj_body, len(weights)),
        grid=(t // tm,),
        in_specs=[row(d), once(g)] + [once(a) for a in flat],
        out_specs=[row(w.shape[1]) for w, _ in weights],
        out_shape=[jax.ShapeDtypeStruct((t, w.shape[1]), dt) for (w, _), dt in zip(weights, out_dtypes)],
        compiler_params=_cp(("parallel",)),
        name="in_proj",
    )(h, g, *flat)


def _scan8(a, u, pos, reverse):
    n = a.shape[0]
    d = 1
    while d < SUBLANES:
        if reverse:
            m = pos < SUBLANES - d
            sh = n - d
        else:
            m = pos >= d
            sh = d
        a_s = jnp.where(m, pltpu.roll(a, sh, 0), 1.0)
        u_s = jnp.where(m, pltpu.roll(u, sh, 0), 0.0)
        u = a * u_s + u
        a = a * a_s
        d *= 2
    return a, u


def _chain8(a, u, carry, reverse):
    groups = a.shape[0] // SUBLANES
    outs = [None] * groups
    for gi in (range(groups - 1, -1, -1) if reverse else range(groups)):
        lo = gi * SUBLANES
        h = u[lo:lo + SUBLANES] + a[lo:lo + SUBLANES] * carry
        outs[gi] = h
        carry = h[0:1] if reverse else h[SUBLANES - 1:SUBLANES]
    return jnp.concatenate(outs, axis=0), carry


def _rglru_body(xa_ref, ya_ref, cw_ref, cb_ref, gw_ref, gb_ref, lam_ref, o_ref, xc_ref, hf_ref, hb_ref):
    s = xa_ref.shape[0]
    tc = min(SCAN_CHUNK, s)
    nc = s // tc
    n = tc + 2 * HALO
    pos = lax.broadcasted_iota(jnp.int32, (tc, HEAD_DIM), 0) & (SUBLANES - 1)
    cw = cw_ref[...]
    cb = cb_ref[...]
    sp2 = (-LRU_C * LOG2_E) * jax.nn.softplus(-lam_ref[...])

    def gates(xc, d):
        w = gw_ref[0, :, d * 256:(d + 1) * 256]
        g = jax.nn.sigmoid(_dot(xc.astype(BF16), w) + gb_ref[0, :, d * 256:(d + 1) * 256])
        a = jnp.exp2(g[:, :HEAD_DIM] * sp2[d:d + 1])
        om = 1.0 - a * a
        root = om * lax.rsqrt(jnp.maximum(om, SQRT_GUARD))
        u = root * (g[:, HEAD_DIM:] * xc)
        return a, u

    def conv(c, carry):
        r0 = pl.multiple_of(c * tc, tc)
        x = xa_ref[pl.ds(r0, tc), :].astype(F32)
        rp = pl.multiple_of(jnp.maximum(r0 - HALO, 0), HALO)
        rn = pl.multiple_of(jnp.minimum(r0 + tc, s - HALO), HALO)
        prev = jnp.where(c > 0, xa_ref[pl.ds(rp, HALO), :].astype(F32), 0.0)
        nxt = jnp.where(c < nc - 1, xa_ref[pl.ds(rn, HALO), :].astype(F32), 0.0)
        cat = jnp.concatenate([prev, x, nxt], axis=0)
        xc_ref[pl.ds(r0, tc), :] = (
            cb + cw[0:1] * pltpu.roll(cat, 1, 0)[HALO:HALO + tc] + cw[1:2] * x
            + cw[2:3] * pltpu.roll(cat, n - 1, 0)[HALO:HALO + tc]
            + cw[3:4] * pltpu.roll(cat, n - 2, 0)[HALO:HALO + tc])
        return carry

    lax.fori_loop(0, nc, conv, 0)

    def scan(i, carry):
        c_f, c_b = carry
        rf = pl.multiple_of(i * tc, tc)
        rb = pl.multiple_of((nc - 1 - i) * tc, tc)
        a, u = gates(xc_ref[pl.ds(rf, tc), :], 0)
        a, u = _scan8(a, u, pos, False)
        h, c_f = _chain8(a, u, c_f, False)
        hf_ref[pl.ds(rf, tc), :] = h
        a, u = gates(xc_ref[pl.ds(rb, tc), :], 1)
        a, u = _scan8(a, u, pos, True)
        h, c_b = _chain8(a, u, c_b, True)
        hb_ref[pl.ds(rb, tc), :] = h
        return c_f, c_b

    zero = jnp.zeros((1, HEAD_DIM), F32)
    lax.fori_loop(0, nc, scan, (zero, zero))

    def fin(c, carry):
        r0 = pl.multiple_of(c * tc, tc)
        y = jax.nn.gelu(ya_ref[pl.ds(r0, tc), :].astype(F32))
        o_ref[pl.ds(r0, tc), :] = ((hf_ref[pl.ds(r0, tc), :] + hb_ref[pl.ds(r0, tc), :]) * y).astype(o_ref.dtype)
        return carry

    lax.fori_loop(0, nc, fin, 0)


def _rglru(pbf, lp, bsz, s):
    t = pbf.shape[0]
    ca, cy = BF_A // HEAD_DIM, (BF_A + MIX_W) // HEAD_DIM
    return pl.pallas_call(
        _rglru_body,
        grid=(bsz, N_HEADS),
        in_specs=[
            pl.BlockSpec((s, HEAD_DIM), lambda b, h: (b, ca + h)),
            pl.BlockSpec((s, HEAD_DIM), lambda b, h: (b, cy + h)),
            pl.BlockSpec((4, HEAD_DIM), lambda b, h: (0, h)),
            pl.BlockSpec((1, HEAD_DIM), lambda b, h: (0, h)),
            pl.BlockSpec((1, HEAD_DIM, 4 * HEAD_DIM), lambda b, h: (h, 0, 0)),
            pl.BlockSpec((1, 1, 4 * HEAD_DIM), lambda b, h: (h, 0, 0)),
            pl.BlockSpec((2, HEAD_DIM), lambda b, h: (0, h)),
        ],
        out_specs=pl.BlockSpec((s, HEAD_DIM), lambda b, h: (b, h)),
        out_shape=jax.ShapeDtypeStruct((t, MIX_W), BF16),
        scratch_shapes=[pltpu.VMEM((s, HEAD_DIM), F32)] * 3,
        compiler_params=_cp(("parallel", "parallel")),
        name="rglru",
    )(pbf, pbf, lp["conv_a_w"], lp["conv_a_b"], lp["rglru_w"], lp["rglru_b"], lp["rglru_lam"])


_Steps = collections.namedtuple("_Steps", "init step fin carry0")


def _run_steps(n, part):
    part.init()

    def body(i, carry):
        carry, commit = part.step(i, carry)
        commit()
        return carry

    lax.fori_loop(0, n, body, part.carry0)

    def fin(c, carry):
        part.fin(c)
        return carry

    lax.fori_loop(0, n, fin, 0)


def _mlstm_steps(q_ref, k_ref, v_ref, og_ref, g_ref, ng_ref, o_ref, hf_ref, hb_ref, stf_ref, stb_ref):
    s = q_ref.shape[0]
    cl = min(MLSTM_CHUNK, s)
    nc = s // cl
    ri = lax.broadcasted_iota(jnp.int32, (cl, cl), 0)
    ci = lax.broadcasted_iota(jnp.int32, (cl, cl), 1)
    masks = (ri >= ci, ri <= ci)
    tris = (masks[0].astype(BF16), masks[1].astype(BF16))
    grp = math.gcd(nc, MIXER_GROUP * (2 if s <= MLSTM_BIG_GROUP_SEQ else 1))
    rows = grp * cl
    mask_rows = tuple(jnp.concatenate([m] * grp, axis=0) for m in masks)
    ones_v = jnp.ones((rows, HEAD_DIM), BF16)
    kscale = HEAD_DIM ** -0.5
    twice = lambda a: jnp.concatenate([a, a], axis=1)
    per_chunk = lambda x: [x[g * cl:(g + 1) * cl] for g in range(grp)]
    cat = lambda xs: jnp.concatenate(xs, axis=0)
    chunks3 = lambda x: x.reshape(grp, cl, x.shape[-1])

    def load(r0, st_ref):
        return (g_ref[pl.ds(r0, rows), :], q_ref[pl.ds(r0, rows), :], k_ref[pl.ds(r0, rows), :],
                v_ref[pl.ds(r0, rows), :], st_ref[...])

    def group(vals, d, m_st):
        g, q, k, v, st = vals
        ig = jnp.broadcast_to(g[:, 2 * d:2 * d + 1], (rows, HEAD_DIM))
        lf = _log_sigmoid(jnp.broadcast_to(g[:, 2 * d + 1:2 * d + 2], (rows, HEAD_DIM)))
        b = cat([_cumsum_mm(tris[d], x) for x in per_chunk(lf)])
        tot3 = chunks3(b)[:, 0:1] if d else chunks3(b)[:, cl - 1:cl]
        c = ig - b
        lw3 = tot3 + chunks3(c)
        m_loc3 = jnp.max(lw3, axis=1, keepdims=True)
        kf = k.astype(F32) * kscale
        kw = (chunks3(kf) * jnp.exp(lw3 - m_loc3)).astype(BF16).reshape(rows, HEAD_DIM)
        kb = kf.astype(BF16)
        v1 = jnp.concatenate([v, ones_v], axis=1)
        cn_loc = [_dot_tn(a, w) for a, w in zip(per_chunk(kw), per_chunk(v1))]
        dm = jnp.where(mask_rows[d], cat([x.T for x in per_chunk(c)]), -jnp.inf)
        m_rel = jnp.max(dm, axis=1, keepdims=True)
        qk = cat([_dot_nt(a, w) for a, w in zip(per_chunk(q), per_chunk(kb))])
        am = (jnp.exp(dm - m_rel) * qk).astype(BF16)
        nd_intra = cat([_dot(a, w) for a, w in zip(per_chunk(am), per_chunk(v1))])
        outs = [None] * grp
        for gi in (range(grp - 1, -1, -1) if d else range(grp)):
            sl = slice(gi * cl, (gi + 1) * cl)
            nd_inter = _dot(q[sl], st.astype(BF16))
            mu = jnp.maximum(m_rel[sl], m_st)
            f_i = jnp.exp(m_rel[sl] - mu)
            f_s = jnp.exp(m_st - mu)
            num = f_i * nd_intra[sl, :HEAD_DIM] + f_s * nd_inter[:, :HEAD_DIM]
            den = f_i * nd_intra[sl, HEAD_DIM:] + f_s * nd_inter[:, HEAD_DIM:]
            outs[gi] = num / jnp.maximum(jnp.abs(den), jnp.exp(-(b[sl] + mu)))
            tot, m_loc = tot3[gi], m_loc3[gi]
            m_new = jnp.maximum(tot + m_st, m_loc)
            st = twice(jnp.exp(tot + m_st - m_new)) * st + twice(jnp.exp(m_loc - m_new)) * cn_loc[gi]
            m_st = m_new
        return cat(outs), m_st, st

    m0 = jnp.full((1, HEAD_DIM), STAB_INIT, F32)

    def init():
        stf_ref[...] = jnp.zeros_like(stf_ref)
        stb_ref[...] = jnp.zeros_like(stb_ref)

    def step(i, carry):
        m_f, m_b = carry
        rf = pl.multiple_of(i * rows, rows)
        rb = pl.multiple_of((nc // grp - 1 - i) * rows, rows)
        vals_f = load(rf, stf_ref)
        vals_b = load(rb, stb_ref)
        out_f, m_f, st_f = group(vals_f, 0, m_f)
        out_b, m_b, st_b = group(vals_b, 1, m_b)

        def commit():
            hf_ref[pl.ds(rf, rows), :] = out_f
            hb_ref[pl.ds(rb, rows), :] = out_b
            stf_ref[...] = st_f
            stb_ref[...] = st_b

        return (m_f, m_b), commit

    def fin(c):
        r0 = pl.multiple_of(c * rows, rows)
        y = _rms(hf_ref[pl.ds(r0, rows), :] + hb_ref[pl.ds(r0, rows), :], ng_ref[...])
        og = jax.nn.sigmoid(og_ref[pl.ds(r0, rows), :].astype(F32))
        o_ref[pl.ds(r0, rows), :] = (og * y).astype(o_ref.dtype)

    return nc // grp, _Steps(init, step, fin, (m0, m0))


def _mlstm_body(*refs):
    n, part = _mlstm_steps(*refs)
    _run_steps(n, part)


def _block_ref(b, m, rev):
    c, w = b.shape
    parts = []
    for blk in range(c // (2 * m)):
        row = blk * 2 * m + (m if rev else m - 1)
        parts.append(jnp.broadcast_to(b[row:row + 1, :], (2 * m, w)))
    return parts[0] if len(parts) == 1 else jnp.concatenate(parts, axis=0)


def _hgrn_steps(layer, q_ref, ff_ref, fb_ref, i_ref, g_ref, lbl_ref, ng_ref, o_ref, of_ref, ob_ref, stf_ref,
                stb_ref):
    s = q_ref.shape[0]
    c = min(HGRN_CHUNK, s)
    nc = s // c
    band = min(HGRN_BAND, c)
    if layer > 0:
        lg = lbl_ref[...]
        e = jnp.exp(lg - jnp.max(lg, axis=0, keepdims=True))
        p = e / jnp.sum(e, axis=0, keepdims=True)
        lb = p[1:2, :]
        for r in range(2, layer + 1):
            lb = lb + p[r:r + 1, :]
        log_lb = jnp.log(lb)
        log_1m = jnp.log(1.0 - lb)
        one_m = 1.0 - lb

    grp = math.gcd(nc, MIXER_GROUP)
    rows = grp * c
    ri = lax.broadcasted_iota(jnp.int32, (c, c), 0)
    ci = lax.broadcasted_iota(jnp.int32, (c, c), 1)
    tris = ((ri >= ci).astype(BF16), (ri <= ci).astype(BF16))
    ri = lax.broadcasted_iota(jnp.int32, (rows, c), 0) & (c - 1)
    ci = lax.broadcasted_iota(jnp.int32, (rows, c), 1)
    rw = lax.broadcasted_iota(jnp.int32, (rows, HEAD_DIM), 0)
    pos = lax.broadcasted_iota(jnp.int32, (rows, 1), 0) & (band - 1)
    levels = []
    m = c // 2
    while m >= band:
        levels.append(m)
        m //= 2
    upper = {m: (rw & (2 * m - 1)) >= m for m in levels}
    same_blk = {m: (ri >> int(math.log2(2 * m))) == (ci >> int(math.log2(2 * m))) for m in levels if 2 * m < c}
    diags = ([ri - ci == dl for dl in range(band)], [ci - ri == dl for dl in range(band)])
    per_chunk = lambda x: [x[g * c:(g + 1) * c] for g in range(grp)]
    cat = lambda xs: jnp.concatenate(xs, axis=0)
    chunks3 = lambda x: x.reshape(grp, c, x.shape[-1])

    def group(r0, rev, st_ref):
        f = (fb_ref if rev else ff_ref)[pl.ds(r0, rows), :]
        e_f = jnp.exp(-jnp.abs(f))
        den = 1.0 + e_f
        lf = jnp.minimum(f, 0.0) - jnp.log(den)
        kd = jnp.where(f >= 0.0, e_f, 1.0) / den
        if layer > 0:
            x = log_1m + lf
            mx = jnp.maximum(log_lb, x)
            lf = mx + jnp.log(jnp.exp(log_lb - mx) + jnp.exp(x - mx))
            kd = one_m * kd
        b = cat([_cumsum_mm(tris[rev], x) for x in per_chunk(lf * LOG2_E)])
        tot3 = chunks3(b)[:, 0:1] if rev else chunks3(b)[:, c - 1:c]
        qs = jax.nn.silu(q_ref[pl.ds(r0, rows), :].astype(F32))
        v = i_ref[pl.ds(r0, rows), :]
        q_in = per_chunk((qs * jnp.exp2(b)).astype(BF16))
        k_out = per_chunk((chunks3(kd) * jnp.exp2(tot3 - chunks3(b))).astype(BF16).reshape(rows, HEAD_DIM))
        kv = [_dot_tn(x, y) for x, y in zip(per_chunk(v), k_out)]
        amat = None
        for m in levels:
            bref = _block_ref(b, m, rev)
            up = upper[m]
            qm = jnp.logical_not(up) if rev else up
            km = up if rev else jnp.logical_not(up)
            qt = (qs * jnp.exp2(jnp.where(qm, b - bref, -jnp.inf))).astype(BF16)
            kt = (kd * jnp.exp2(jnp.where(km, bref - b, -jnp.inf))).astype(BF16)
            am = cat([_dot_nt(x, y) for x, y in zip(per_chunk(qt), per_chunk(kt))])
            if m in same_blk:
                am = jnp.where(same_blk[m], am, 0.0)
            amat = am if amat is None else amat + am
        for dl in range(band):
            if dl == 0:
                a = jnp.sum(qs * kd, axis=1, keepdims=True)
            else:
                sh = (rows - dl) if rev else dl
                valid = (pos + dl < band) if rev else (pos >= dl)
                ex = jnp.exp2(b - pltpu.roll(b, sh, 0))
                a = jnp.where(valid, jnp.sum(qs * pltpu.roll(kd, sh, 0) * ex, axis=1, keepdims=True), 0.0)
            term = jnp.where(diags[rev][dl], a, 0.0)
            amat = term if amat is None else amat + term
        o_intra = [_dot(x, y) for x, y in zip(per_chunk(amat.astype(BF16)), per_chunk(v))]
        st = st_ref[...]
        outs = [None] * grp
        for gi in (range(grp - 1, -1, -1) if rev else range(grp)):
            outs[gi] = o_intra[gi] + _dot_nt(q_in[gi], st.astype(BF16))
            st = jnp.exp2(tot3[gi]) * st + kv[gi]
        return cat(outs), st

    def init():
        stf_ref[...] = jnp.zeros_like(stf_ref)
        stb_ref[...] = jnp.zeros_like(stb_ref)

    def step(i, carry):
        rf = pl.multiple_of(i * rows, rows)
        rb = pl.multiple_of((nc // grp - 1 - i) * rows, rows)
        out_f, st_f = group(rf, 0, stf_ref)
        out_b, st_b = group(rb, 1, stb_ref)

        def commit():
            of_ref[pl.ds(rf, rows), :] = out_f
            ob_ref[pl.ds(rb, rows), :] = out_b
            stf_ref[...] = st_f
            stb_ref[...] = st_b

        return carry, commit

    def fin(ci_):
        r0 = pl.multiple_of(ci_ * rows, rows)
        hs = of_ref[pl.ds(r0, rows), :] + ob_ref[pl.ds(r0, rows), :]
        gate = jax.nn.sigmoid(g_ref[pl.ds(r0, rows), :].astype(F32))
        o_ref[pl.ds(r0, rows), :] = (gate * _rms(hs, ng_ref[...])).astype(o_ref.dtype)

    return nc // grp, _Steps(init, step, fin, jnp.int32(0))


def _hgrn_body(layer, *refs):
    n, part = _hgrn_steps(layer, *refs)
    _run_steps(n, part)


def _head_blk(s, off):
    return pl.BlockSpec((s, HEAD_DIM), lambda b, h: (b, off + h))


def _head_row(rows):
    return pl.BlockSpec((rows, HEAD_DIM), lambda b, h: (0, h))


def _mlstm_call_parts(pbf, pfp, lp, s):
    cb, cg = BF_B // HEAD_DIM, FP_GATE // HEAD_DIM
    specs = [_head_blk(s, cb), _head_blk(s, cb + 4), _head_blk(s, cb + 8), _head_blk(s, cb + 12),
             _head_blk(s, cg), _head_row(1)]
    scratch = [pltpu.VMEM((s, HEAD_DIM), F32)] * 2 + [pltpu.VMEM((HEAD_DIM, 2 * HEAD_DIM), F32)] * 2
    return specs, (pbf, pbf, pbf, pbf, pfp, lp["mlstm_norm"]), scratch


def _hgrn_call_parts(pbf, pfp, lp, lb_logits, s):
    cd = BF_D3 // HEAD_DIM
    specs = [_head_blk(s, cd), _head_blk(s, FP_FF // HEAD_DIM), _head_blk(s, FP_FB // HEAD_DIM),
             _head_blk(s, cd + 4), _head_blk(s, cd + 8), _head_row(lb_logits.shape[0]), _head_row(1)]
    scratch = [pltpu.VMEM((s, HEAD_DIM), F32)] * 2 + [pltpu.VMEM((HEAD_DIM, HEAD_DIM), F32)] * 2
    return specs, (pbf, pfp, pfp, pbf, pbf, lb_logits, lp["hgrn_norm"]), scratch


def _mlstm_hgrn(pbf, pfp, lp, lb_logits, layer, bsz, s):
    t = pbf.shape[0]
    m_specs, m_args, m_scr = _mlstm_call_parts(pbf, pfp, lp, s)
    h_specs, h_args, h_scr = _hgrn_call_parts(pbf, pfp, lp, lb_logits, s)
    out_spec = _head_blk(s, 0)
    out_shape = jax.ShapeDtypeStruct((t, MIX_W), BF16)
    m_out = pl.pallas_call(
        _mlstm_body, grid=(bsz, N_HEADS), in_specs=m_specs, out_specs=out_spec, out_shape=out_shape,
        scratch_shapes=m_scr, compiler_params=_cp(("parallel", "parallel")), name="mlstm")(*m_args)
    h_out = pl.pallas_call(
        functools.partial(_hgrn_body, layer), grid=(bsz, N_HEADS), in_specs=h_specs, out_specs=out_spec,
        out_shape=out_shape, scratch_shapes=h_scr, compiler_params=_cp(("parallel", "parallel")),
        name="hgrn2")(*h_args)
    return m_out, h_out


def _shift3(cat, w, bias, ts):
    n = ts + 2 * HALO
    return (bias + w[0:1] * pltpu.roll(cat, 1, 0)[HALO:HALO + ts] + w[1:2] * cat[HALO:HALO + ts]
            + w[2:3] * pltpu.roll(cat, n - 1, 0)[HALO:HALO + ts])


def _hyena_prep_body(x_ref, xp_ref, xn_ref, cw_ref, cb_ref, z_ref, x1_ref):
    i = pl.program_id(1)
    ts = x_ref.shape[0]
    prev = jnp.where(i > 0, xp_ref[...].astype(F32), 0.0)
    nxt = jnp.where(i < pl.num_programs(1) - 1, xn_ref[...].astype(F32), 0.0)
    cat = jnp.concatenate([prev, x_ref[...].astype(F32), nxt], axis=0)
    u = _shift3(cat, cw_ref[...], cb_ref[...], ts)
    z_ref[...] = (u[:, 2 * MIX_W:] * u[:, :MIX_W]).astype(z_ref.dtype)
    x1_ref[...] = u[:, MIX_W:2 * MIX_W].astype(x1_ref.dtype)


def _hyena_prep(pbf, lp, bsz, s):
    t = pbf.shape[0]
    ts = min(512, s)
    per = s // ts
    hb = ts // HALO
    nb = t // HALO
    w = 3 * MIX_W
    return pl.pallas_call(
        _hyena_prep_body,
        grid=(bsz, per),
        in_specs=[
            pl.BlockSpec((ts, w), lambda b, i: (b * per + i, 0)),
            pl.BlockSpec((HALO, w), lambda b, i: (jnp.maximum((b * per + i) * hb - 1, 0), 0)),
            pl.BlockSpec((HALO, w), lambda b, i: (jnp.minimum((b * per + i + 1) * hb, nb - 1), 0)),
            pl.BlockSpec((3, w), lambda b, i: (0, 0)),
            pl.BlockSpec((1, w), lambda b, i: (0, 0)),
        ],
        out_specs=[pl.BlockSpec((ts, MIX_W), lambda b, i: (b * per + i, 0)),
                   pl.BlockSpec((ts, MIX_W), lambda b, i: (b * per + i, 0))],
        out_shape=[jax.ShapeDtypeStruct((t, MIX_W), BF16), jax.ShapeDtypeStruct((t, MIX_W), BF16)],
        compiler_params=_cp(("parallel", "arbitrary")),
        name="hyena_prep",
    )(pbf, pbf, pbf, lp["hyena_conv_w"], lp["hyena_conv_b"])


def _split_bf16(a):
    hi = a.astype(BF16)
    return jnp.stack([hi, (a - hi.astype(F32)).astype(BF16)])


def _dot3(f_hi, f_lo, x):
    x_hi = x.astype(BF16)
    x_lo = (x - x_hi.astype(F32)).astype(BF16)
    return _dot(f_hi, x_hi) + _dot(f_hi, x_lo) + _dot(f_lo, x_hi)


def _dft1_body(x_ref, f_ref, o_ref):
    jb, _, n1, r, c = o_ref.shape
    n1h = x_ref.shape[0]
    for jj in range(jb):
        x = x_ref[:, jj * r:(jj + 1) * r, :].reshape(n1h * r, c)
        o = _dot3(f_ref[0], f_ref[1], x) if f_ref.ndim == 3 else _dot(f_ref[...], x)
        o_ref[jj] = o.reshape(2, n1, r, c).astype(o_ref.dtype)


def _dft1(x4, kron1, out_dtype):
    bsz, n1h, n2, c = x4.shape
    r = kron1.shape[-1] // n1h
    n1 = kron1.shape[-2] // (2 * r)
    nj = n2 // r
    jb = math.gcd(nj, max(1, DFT_STEP_ROWS // (2 * n1)))
    return pl.pallas_call(
        _dft1_body,
        grid=(bsz, nj // jb),
        in_specs=[pl.BlockSpec((None, n1h, jb * r, c), lambda b, j: (b, 0, j, 0)),
                  pl.BlockSpec(kron1.shape, lambda b, j: (0,) * kron1.ndim, pipeline_mode=pl.Buffered(1))],
        out_specs=pl.BlockSpec((None, jb, 2, n1, r, c), lambda b, j: (b, j, 0, 0, 0, 0)),
        out_shape=jax.ShapeDtypeStruct((bsz, n2 // r, 2, n1, r, c), out_dtype),
        compiler_params=_cp(("parallel", "parallel")),
        name="dft_stage1",
    )(x4, kron1)


def _spectrum_rows(a_ref, lead, kk):
    nj = a_ref.shape[len(lead)]
    return jnp.concatenate([a_ref[lead + (j, e, kk)] for e in range(2) for j in range(nj)], axis=0)


def _dft2_conv_body(kb, a_ref, f_ref, fi_ref, g_ref, o_ref):
    nj, _, _, r, _ = a_ref.shape
    n2 = nj * r
    for kk in range(kb):
        x = _dot(f_ref[kk], _spectrum_rows(a_ref, (), kk))
        xr, xi = x[:n2], x[n2:]
        gr, gi = g_ref[kk, 0], g_ref[kk, 1]
        y = jnp.concatenate([xr * gr - xi * gi, xr * gi + xi * gr], axis=0).astype(BF16)
        bq = _dot(fi_ref[kk], y).astype(o_ref.dtype)
        for e in range(2):
            for j in range(nj):
                o_ref[j, e, kk] = bq[e * n2 + j * r:e * n2 + (j + 1) * r]


def _dft2_conv(a6, f2, f2i, gspec, kb):
    bsz, nj, _, n1, r, c = a6.shape
    n2 = nj * r
    blk = pl.BlockSpec((None, nj, 2, kb, r, c), lambda k, b: (b, 0, 0, k, 0, 0))
    return pl.pallas_call(
        functools.partial(_dft2_conv_body, kb),
        grid=(n1 // kb, bsz),
        in_specs=[blk,
                  pl.BlockSpec((kb, 2 * n2, 2 * n2), lambda k, b: (k, 0, 0)),
                  pl.BlockSpec((kb, 2 * n2, 2 * n2), lambda k, b: (k, 0, 0)),
                  pl.BlockSpec((kb, 2, n2, c), lambda k, b: (k, 0, 0, 0))],
        out_specs=blk,
        out_shape=jax.ShapeDtypeStruct(a6.shape, BF16),
        compiler_params=_cp(("parallel", "arbitrary")),
        name="dft_stage2_conv",
    )(a6, f2, f2i, gspec)


def _dft2_filter_body(kb, a_ref, f_ref, o_ref):
    n2 = a_ref.shape[1] * a_ref.shape[4]
    k0 = pl.program_id(0) * kb
    for kk in range(kb):
        sgn = (1 - 2 * ((k0 + kk) & 1)).astype(F32)
        xs = [_dot3(f_ref[0, kk], f_ref[1, kk], _spectrum_rows(a_ref, (part,), kk)) for part in range(2)]
        x = xs[0] + sgn * xs[1]
        o_ref[kk, 0] = x[:n2]
        o_ref[kk, 1] = x[n2:]


def _dft2_filter(a6, f2, kb):
    _, nj, _, n1, r, c = a6.shape
    n2 = nj * r
    return pl.pallas_call(
        functools.partial(_dft2_filter_body, kb),
        grid=(n1 // kb,),
        in_specs=[pl.BlockSpec((2, nj, 2, kb, r, c), lambda k: (0, 0, 0, k, 0, 0)),
                  pl.BlockSpec((2, kb, 2 * n2, 2 * n2), lambda k: (0, k, 0, 0))],
        out_specs=pl.BlockSpec((kb, 2, n2, c), lambda k: (k, 0, 0, 0)),
        out_shape=jax.ShapeDtypeStruct((n1, 2, n2, c), F32),
        compiler_params=_cp(("parallel",)),
        name="dft_stage2_filter",
    )(a6, f2)


def _dft3_body(b_ref, f_ref, x1_ref, z_ref, sk_ref, o_ref):
    jb, _, n1, r, c = b_ref.shape
    n1h = o_ref.shape[0]
    for jj in range(jb):
        rows = slice(jj * r, (jj + 1) * r)
        y = _dot(f_ref[...], b_ref[jj].reshape(2 * n1 * r, c)).reshape(n1h, r, c)
        z = z_ref[:, rows, :].astype(F32)
        o_ref[:, rows, :] = (x1_ref[:, rows, :].astype(F32) * (y + sk_ref[...] * z)).astype(o_ref.dtype)


def _dft3(b6, kron3, x1_4, z4, skip):
    bsz, nj, _, n1, r, c = b6.shape
    n1h, n2 = n1 // 2, nj * r
    assert r == kron3.shape[0] // n1h
    jb = math.gcd(nj, max(1, DFT_STEP_ROWS // (2 * n1)))
    row = pl.BlockSpec((None, n1h, jb * r, c), lambda b, j: (b, 0, j, 0))
    return pl.pallas_call(
        _dft3_body,
        grid=(bsz, nj // jb),
        in_specs=[pl.BlockSpec((None, jb, 2, n1, r, c), lambda b, j: (b, j, 0, 0, 0, 0)),
                  pl.BlockSpec(kron3.shape, lambda b, j: (0, 0), pipeline_mode=pl.Buffered(1)),
                  row, row,
                  pl.BlockSpec((1, c), lambda b, j: (0, 0))],
        out_specs=row,
        out_shape=jax.ShapeDtypeStruct((bsz, n1h, n2, c), BF16),
        compiler_params=_cp(("parallel", "parallel")),
        name="dft_stage3",
    )(b6, kron3, x1_4, z4, skip)


def _hyena_filter_body(seq_len, pos_ref, sc_ref, bands_ref, w1t_ref, w1c_ref, w1s_ref, b1_ref, fr1_ref,
                       w2_ref, b2_ref, fr2_ref, w3_ref, rate_ref, o_ref):
    pos = pos_ref[...]
    t = pos * (1.0 / (seq_len - 1))
    arg = (pos * (2.0 * math.pi / seq_len)) * bands_ref[...]
    hdot = lambda a, b: jnp.dot(a, b, preferred_element_type=F32, precision=HIGHEST)
    pre = t * w1t_ref[...] + hdot(jnp.cos(arg), w1c_ref[...]) - hdot(jnp.sin(arg), w1s_ref[...]) + b1_ref[...]
    hid = jnp.sin(fr1_ref[...] * pre)
    hid = jnp.sin(fr2_ref[...] * (hdot(hid, w2_ref[...]) + b2_ref[...]))
    o_ref[...] = hdot(hid, w3_ref[...]) * jnp.exp(-t * rate_ref[...]) * sc_ref[...]


def _hyena_filter(lp, s):
    ts = min(256, s)
    r = 2 * s + ts
    pos = np.concatenate([np.arange(s), s - np.arange(s), np.zeros(ts)]).astype(np.float32)[:, None]
    sc = np.ones((r, 1), np.float32)
    sc[s] = 0.0
    pad = HEAD_DIM - HYENA_HID
    bands = np.zeros((1, HEAD_DIM), np.float32)
    bands[0, :HYENA_BANDS] = np.linspace(1e-4, HYENA_BANDS - 1, HYENA_BANDS)
    rate = np.abs(np.linspace(math.log(HYENA_TARGET) / HYENA_FAST, math.log(HYENA_TARGET) / HYENA_SLOW, MIX_W))
    rate = np.tile(rate, 2).astype(np.float32)[None, :]
    half = lambda i: (0, jnp.where(i < s // ts, 0, 1))
    w1 = lp["hyena_w1"]
    padc = lambda a: jnp.pad(a, ((0, 0), (0, pad)))
    w1t = padc(w1[0:1])
    w1c = jnp.pad(w1[1:1 + HYENA_BANDS], ((0, HEAD_DIM - HYENA_BANDS), (0, pad)))
    w1s = jnp.pad(w1[1 + HYENA_BANDS:], ((0, HEAD_DIM - HYENA_BANDS), (0, pad)))
    w2 = jnp.pad(lp["hyena_w2"], ((0, pad), (0, pad)))
    w3 = jnp.pad(lp["hyena_w3"], ((0, pad), (0, 0)))
    vec = lambda a: padc(a[None, :])
    full = lambda a: pl.BlockSpec(a.shape, lambda i: (0, 0))
    args = [jnp.asarray(bands), w1t, w1c, w1s, vec(lp["hyena_b1"]), vec(lp["hyena_freq1"]), w2,
            vec(lp["hyena_b2"]), vec(lp["hyena_freq2"])]
    out = pl.pallas_call(
        functools.partial(_hyena_filter_body, s),
        grid=(r // ts,),
        in_specs=[pl.BlockSpec((ts, 1), lambda i: (i, 0)), pl.BlockSpec((ts, 1), lambda i: (i, 0))]
                 + [full(a) for a in args]
                 + [pl.BlockSpec((HEAD_DIM, MIX_W), half), pl.BlockSpec((1, MIX_W), half)],
        out_specs=pl.BlockSpec((ts, MIX_W), lambda i: (i, 0)),
        out_shape=jax.ShapeDtypeStruct((r, MIX_W), F32),
        compiler_params=_cp(("parallel",)),
        name="hyena_filter",
    )(jnp.asarray(pos), jnp.asarray(sc), *args, w3, jnp.asarray(rate))
    return out[:s], out[s:2 * s], out[2 * s:2 * s + 1]


def _dft_tables(s):
    n = 2 * s
    n2 = min(DFT_N2, s // 8)
    n1 = n // n2
    n1h = n1 // 2
    two_pi = 2.0 * math.pi
    k1 = jnp.arange(n1, dtype=jnp.int32)
    a1 = (two_pi / n1) * ((k1[:, None] * k1[None, :n1h]) % n1).astype(F32)
    f1 = jnp.concatenate([jnp.cos(a1), -jnp.sin(a1)], axis=0)
    f1i = jnp.concatenate([jnp.cos(a1).T, -jnp.sin(a1).T], axis=1) * (1.0 / n)
    j = jnp.arange(n2, dtype=jnp.int32)
    ph = (j[None, :, None] * j[None, None, :] * n1 + j[None, None, :] * k1[:, None, None]) % n
    a2 = (two_pi / n) * ph.astype(F32)
    tr, ti = jnp.cos(a2), -jnp.sin(a2)
    f2 = jnp.concatenate([jnp.concatenate([tr, -ti], axis=2), jnp.concatenate([ti, tr], axis=2)], axis=1)
    kron = lambda m, r: jnp.kron(m, jnp.eye(r, dtype=F32))
    r32, r16 = min(SUBLANES, n2), min(2 * SUBLANES, n2)
    return dict(n1=n1, n2=n2, f2_split=_split_bf16(f2), f2_bf=f2.astype(BF16),
                f2i_bf=jnp.swapaxes(f2, 1, 2).astype(BF16), kron1_split=_split_bf16(kron(f1, r32)),
                kron1_bf=kron(f1, r16).astype(BF16), kron3_bf=kron(f1i, r16).astype(BF16))


def _hyena_spectrum(lp, s, tab):
    gpos, gneg, hb0 = _hyena_filter(lp, s)
    n1, n2 = tab["n1"], tab["n2"]
    a = _dft1(jnp.stack([gpos, gneg]).reshape(2, n1 // 2, n2, MIX_W), tab["kron1_split"], F32)
    return _dft2_filter(a, tab["f2_split"], min(2, n1)), hb0


def _hyena(pbf, lp, gspec, hb0, tab, bsz, s):
    n1, n2 = tab["n1"], tab["n2"]
    z, x1 = _hyena_prep(pbf, lp, bsz, s)
    z4 = z.reshape(bsz, n1 // 2, n2, MIX_W)
    a = _dft1(z4, tab["kron1_bf"], BF16)
    bq = _dft2_conv(a, tab["f2_bf"], tab["f2i_bf"], gspec, min(8, n1))
    out = _dft3(bq, tab["kron3_bf"], x1.reshape(z4.shape), z4, lp["hyena_skip"][None, :] + hb0)
    return out.reshape(bsz * s, MIX_W)


def _merge_body(h_ref, g0_ref, g1_ref, g2_ref, g3_ref, ba_ref, bb_ref, bc_ref, bd_ref, wb_ref, wo_ref, o_ref):
    merged = None
    for idx, (g_ref, br) in enumerate(zip((g0_ref, g1_ref, g2_ref, g3_ref), (ba_ref, bb_ref, bc_ref, bd_ref))):
        term = jax.nn.sigmoid(g_ref[...].astype(F32)) * _dot(br[...], wb_ref[idx])
        merged = term if merged is None else merged + term
    o_ref[...] = h_ref[...] + _dot(merged.astype(BF16), wo_ref[...])


def _merge(h, pbf, branches, lp):
    t, d = h.shape
    tm = min(512, t)
    gcol = BF_G // d
    row = lambda w: pl.BlockSpec((tm, w), lambda i: (i, 0))
    gate = lambda idx: pl.BlockSpec((tm, d), lambda i: (i, gcol + idx))
    return pl.pallas_call(
        _merge_body,
        grid=(t // tm,),
        in_specs=[row(d), gate(0), gate(1), gate(2), gate(3),
                  row(MIX_W), row(MIX_W), row(MIX_W), row(MIX_W),
                  pl.BlockSpec(lp["w_branch"].shape, lambda i: (0, 0, 0)),
                  pl.BlockSpec(lp["w_out"].shape, lambda i: (0, 0))],
        out_specs=row(d),
        out_shape=jax.ShapeDtypeStruct((t, d), F32),
        compiler_params=_cp(("parallel",)),
        name="merge",
    )(h, pbf, pbf, pbf, pbf, *branches, lp["w_branch"], lp["w_out"])


def _ffn_body(per, final, h_ref, hp_ref, hn_ref, p_ref, gf_ref, wu_ref, cw_ref, cb_ref, wd_ref, gp_ref, wg_ref,
              wp_ref, gl_ref, o_ref):
    i = pl.program_id(0)
    tm = h_ref.shape[0]
    dff = wd_ref.shape[0]
    gf = gf_ref[...]
    h = h_ref[...]
    first = (i % per) == 0
    last = (i % per) == per - 1
    xp = jnp.where(first, 0.0, _rms(hp_ref[...], gf))
    xn = jnp.where(last, 0.0, _rms(hn_ref[...], gf))
    cat = jnp.concatenate([xp, _rms(h, gf), xn], axis=0).astype(BF16)
    cb_w = min(2048, dff)
    acc = jnp.zeros_like(h)
    for cb in range(dff // cb_w):
        lo, lo2 = cb * cb_w, dff + cb * cb_w
        u1 = _shift3(_dot(cat, wu_ref[:, lo:lo + cb_w]), cw_ref[:, lo:lo + cb_w], cb_ref[:, lo:lo + cb_w], tm)
        u2 = _shift3(_dot(cat, wu_ref[:, lo2:lo2 + cb_w]), cw_ref[:, lo2:lo2 + cb_w], cb_ref[:, lo2:lo2 + cb_w], tm)
        acc = acc + _dot((jax.nn.gelu(u1) * u2).astype(BF16), wd_ref[lo:lo + cb_w, :])
    h = h + acc
    gate = jax.nn.sigmoid(_dot(_rms(h, gp_ref[...]).astype(BF16), wg_ref[...]))
    h = h + gate * _dot(p_ref[...].astype(BF16), wp_ref[...])
    if final:
        h = _rms(h, gl_ref[...])
    o_ref[...] = h


def _ffn(h, p, lp, final_norm, s, final):
    t, d = h.shape
    tm = min(512, s)
    per = s // tm
    hb = tm // HALO
    nb = t // HALO
    full = lambda a: pl.BlockSpec(a.shape, lambda i: (0,) * a.ndim)
    row = lambda w: pl.BlockSpec((tm, w), lambda i: (i, 0))
    ws = [lp["norm_ffn"], lp["w_up"], lp["ffn_conv_w"], lp["ffn_conv_b"], lp["w_down"], lp["norm_ple"],
          lp["w_ple_gate"], lp["w_ple"], final_norm]
    return pl.pallas_call(
        functools.partial(_ffn_body, per, final),
        grid=(t // tm,),
        in_specs=[row(d),
                  pl.BlockSpec((HALO, d), lambda i: (jnp.maximum(i * hb - 1, 0), 0)),
                  pl.BlockSpec((HALO, d), lambda i: (jnp.minimum((i + 1) * hb, nb - 1), 0)),
                  row(p.shape[1])] + [full(a) for a in ws],
        out_specs=row(d),
        out_shape=jax.ShapeDtypeStruct((t, d), F32),
        compiler_params=_cp(("parallel",)),
        name="ffn_ple",
    )(h, h, h, p, *ws)


def _prepare_params(prm):
    d_model = prm["w_in"].shape[1]
    depth = prm["w_in"].shape[0]
    off_b = 2 * MIX_W
    off_gate = off_b + 4 * MIX_W
    off_c = off_gate + 4 * N_HEADS
    off_d = off_c + 3 * MIX_W
    off_g = off_d + 5 * MIX_W

    def regroup(a):
        sl = lambda lo, w: a[..., lo:lo + w]
        bf = jnp.concatenate([sl(off_g, 4 * d_model), sl(0, 2 * MIX_W), sl(off_b, 4 * MIX_W), sl(off_d, MIX_W),
                              sl(off_d + 3 * MIX_W, 2 * MIX_W)], axis=-1)
        gates = sl(off_gate, 4 * N_HEADS).reshape(a.shape[:-1] + (4, N_HEADS))
        gates = jnp.swapaxes(gates, -1, -2)
        gates = jnp.pad(gates, [(0, 0)] * (gates.ndim - 1) + [(0, HEAD_DIM - 4)])
        fp = jnp.concatenate([sl(off_d + MIX_W, 2 * MIX_W), gates.reshape(a.shape[:-1] + (N_HEADS * HEAD_DIM,))],
                             axis=-1)
        return bf, sl(off_c, 3 * MIX_W), fp

    w_bf, w_hy, w_fp = regroup(prm["w_in"])
    b_bf, b_hy, b_fp = regroup(prm["b_in"][:, None, :])
    gw = jnp.transpose(prm["rglru_w"], (0, 3, 4, 1, 2, 5)).reshape(depth, N_HEADS, HEAD_DIM, 4 * HEAD_DIM)
    gb = prm["rglru_b"].reshape(depth, 2, 2, N_HEADS, HEAD_DIM)
    gb = jnp.transpose(gb, (0, 3, 1, 2, 4)).reshape(depth, N_HEADS, 1, 4 * HEAD_DIM)
    row = lambda a: a[:, None, :]
    out = dict(
        norm_mix=row(prm["norm_mix"]), w_bf=w_bf.astype(BF16), b_bf=b_bf, w_hy=w_hy.astype(BF16), b_hy=b_hy,
        w_fp=w_fp.astype(BF16), b_fp=b_fp,
        conv_a_w=prm["conv_a_w"], conv_a_b=row(prm["conv_a_b"]), rglru_w=gw.astype(BF16), rglru_b=gb,
        rglru_lam=prm["rglru_lam"], mlstm_norm=row(prm["mlstm_norm"]),
        hyena_conv_w=prm["hyena_conv_w"], hyena_conv_b=row(prm["hyena_conv_b"]),
        hyena_w1=prm["hyena_w1"], hyena_b1=prm["hyena_b1"], hyena_freq1=prm["hyena_freq1"],
        hyena_w2=prm["hyena_w2"], hyena_b2=prm["hyena_b2"], hyena_freq2=prm["hyena_freq2"],
        hyena_w3=prm["hyena_w3"], hyena_skip=prm["hyena_skip"], hgrn_norm=row(prm["hgrn_norm"]),
        w_branch=prm["w_branch"].astype(BF16), w_out=prm["w_out"].astype(BF16),
        norm_ffn=row(prm["norm_ffn"]), w_up=prm["w_up"].astype(BF16), ffn_conv_w=prm["ffn_conv_w"],
        ffn_conv_b=row(prm["ffn_conv_b"]), w_down=prm["w_down"].astype(BF16), norm_ple=row(prm["norm_ple"]),
        w_ple_gate=prm["w_ple_gate"].astype(BF16), w_ple=prm["w_ple"].astype(BF16),
    )
    return out


def _trunk(x, p, prm, lb_logits, final_norm):
    bsz, s, d = x.shape
    depth = p.shape[0]
    h = x.reshape(bsz * s, d)
    tab = _dft_tables(s)
    for i in range(depth):
        lp = {k: v[i] for k, v in prm.items()}
        pbf, phy, pfp = _proj(h, lp["norm_mix"], [(lp["w_bf"], lp["b_bf"]), (lp["w_hy"], lp["b_hy"]),
                                                  (lp["w_fp"], lp["b_fp"])], (BF16, BF16, F32))
        gspec, hb0 = _hyena_spectrum(lp, s, tab)
        br_b, br_d = _mlstm_hgrn(pbf, pfp, lp, lb_logits, i, bsz, s)
        branches = (_rglru(pbf, lp, bsz, s), br_b, _hyena(phy, lp, gspec, hb0, tab, bsz, s), br_d)
        h = _merge(h, pbf, branches, lp)
        h = _ffn(h, p[i].reshape(bsz * s, -1), lp, final_norm, s, i == depth - 1)
    return h.reshape(bsz, s, d)


def kernel(x_prompt, x_sample, p_prompt, p_sample, norm_mix, w_in, b_in, conv_a_w, conv_a_b, rglru_w, rglru_b, rglru_lam, mlstm_norm, hyena_conv_w, hyena_conv_b, hyena_w1, hyena_b1, hyena_freq1, hyena_w2, hyena_b2, hyena_freq2, hyena_w3, hyena_skip, hgrn_lb_logits, hgrn_norm, w_branch, w_out, norm_ffn, w_up, ffn_conv_w, ffn_conv_b, w_down, norm_ple, w_ple_gate, w_ple, final_norm):
    prm = _prepare_params(dict(
        norm_mix=norm_mix, w_in=w_in, b_in=b_in, conv_a_w=conv_a_w, conv_a_b=conv_a_b, rglru_w=rglru_w,
        rglru_b=rglru_b, rglru_lam=rglru_lam, mlstm_norm=mlstm_norm, hyena_conv_w=hyena_conv_w,
        hyena_conv_b=hyena_conv_b, hyena_w1=hyena_w1, hyena_b1=hyena_b1, hyena_freq1=hyena_freq1,
        hyena_w2=hyena_w2, hyena_b2=hyena_b2, hyena_freq2=hyena_freq2, hyena_w3=hyena_w3,
        hyena_skip=hyena_skip, hgrn_norm=hgrn_norm, w_branch=w_branch, w_out=w_out, norm_ffn=norm_ffn,
        w_up=w_up, ffn_conv_w=ffn_conv_w, ffn_conv_b=ffn_conv_b, w_down=w_down, norm_ple=norm_ple,
        w_ple_gate=w_ple_gate, w_ple=w_ple))
    fn = final_norm[None, :]
    y_prompt = _trunk(x_prompt, p_prompt, prm, hgrn_lb_logits, fn)
    y_sample = _trunk(x_sample, p_sample, prm, hgrn_lb_logits, fn)
    return (y_prompt, y_sample)
```

```python
import collections
import functools
import math

import numpy as np
import jax
import jax.numpy as jnp
from jax import lax
from jax.experimental import pallas as pl
from jax.experimental.pallas import tpu as pltpu

F32 = jnp.float32
BF16 = jnp.bfloat16
HIGHEST = lax.Precision.HIGHEST

N_HEADS = 4
HEAD_DIM = 128
MIX_W = N_HEADS * HEAD_DIM
LRU_C = 8.0
LOG2_E = math.log2(math.e)
SQRT_GUARD = 1e-30
EPS = 1e-6
STAB_INIT = -1e30
HYENA_BANDS = 16
HYENA_HID = 64
HYENA_FAST = 0.3
HYENA_SLOW = 1.5
HYENA_TARGET = 1e-2

MLSTM_CHUNK = 128
HGRN_CHUNK = 128
MIXER_GROUP = 4
MLSTM_BIG_GROUP_SEQ = 4096
HGRN_BAND = 4
SCAN_CHUNK = 512
HALO = 16
SUBLANES = 8
PROJ_COLS = 2304
DFT_N2 = 128
DFT_STEP_ROWS = 512
V7X_VMEM_LIMIT = 56 * 1024 * 1024
V7X_MXU_COLS = 256

BF_G, BF_A, BF_B, BF_D3 = 0, 4096, 5120, 7168
FP_FF, FP_FB, FP_GATE = 0, 512, 1024


def _cp(sem, vmem=V7X_VMEM_LIMIT):
    return pltpu.CompilerParams(dimension_semantics=sem, vmem_limit_bytes=vmem)


def _rms(x, g):
    return x * lax.rsqrt(jnp.mean(x * x, axis=-1, keepdims=True) + EPS) * g


def _log2_sigmoid(x):
    xs = x * LOG2_E
    return jnp.minimum(xs, 0.0) - jnp.log(1.0 + jnp.exp2(-jnp.abs(xs))) * LOG2_E


def _dot(a, b):
    return jnp.dot(a, b, preferred_element_type=F32)


def _dot_nt(a, b):
    return lax.dot_general(a, b, (((1,), (1,)), ((), ())), preferred_element_type=F32)


def _dot_tn(a, b):
    return lax.dot_general(a, b, (((0,), (0,)), ((), ())), preferred_element_type=F32)


def _cumsum_mm(tri, x):
    hi = x.astype(BF16)
    lo = (x - hi.astype(F32)).astype(BF16)
    return _dot(tri, hi) + _dot(tri, lo)


def _col_chunk(n):
    return max(w for w in range(V7X_MXU_COLS, PROJ_COLS + 1, V7X_MXU_COLS) if n % w == 0)


def _proj_body(n_out, h_ref, g_ref, *refs):
    xn = _rms(h_ref[...], g_ref[...]).astype(BF16)
    for k in range(n_out):
        w_ref, b_ref, o_ref = refs[2 * k], refs[2 * k + 1], refs[2 * n_out + k]
        n = w_ref.shape[1]
        tn = _col_chunk(n)
        for lo in range(0, n, tn):
            o_ref[:, lo:lo + tn] = (_dot(xn, w_ref[:, lo:lo + tn]) + b_ref[:, lo:lo + tn]).astype(o_ref.dtype)


def _proj(h, g, weights, out_dtypes):
    t, d = h.shape
    tm = min(256, t)
    once = lambda a: pl.BlockSpec(a.shape, lambda i: (0, 0), pipeline_mode=pl.Buffered(1))
    row = lambda n: pl.BlockSpec((tm, n), lambda i: (i, 0))
    flat = [a for wb in weights for a in wb]
    return pl.pallas_call(
        functools.partial(_proj_body, len(weights)),
        grid=(t // tm,),
        in_specs=[row(d), once(g)] + [once(a) for a in flat],
        out_specs=[row(w.shape[1]) for w, _ in weights],
        out_shape=[jax.ShapeDtypeStruct((t, w.shape[1]), dt) for (w, _), dt in zip(weights, out_dtypes)],
        compiler_params=_cp(("parallel",)),
        name="in_proj",
    )(h, g, *flat)


def _scan8(a, u, pos, reverse):
    n = a.shape[0]
    d = 1
    while d < SUBLANES:
        if reverse:
            m = pos < SUBLANES - d
            sh = n - d
        else:
            m = pos >= d
            sh = d
        a_s = jnp.where(m, pltpu.roll(a, sh, 0), 1.0)
        u_s = jnp.where(m, pltpu.roll(u, sh, 0), 0.0)
        u = a * u_s + u
        a = a * a_s
        d *= 2
    return a, u


def _chain8(a, u, carry, reverse):
    groups = a.shape[0] // SUBLANES
    outs = [None] * groups
    for gi in (range(groups - 1, -1, -1) if reverse else range(groups)):
        lo = gi * SUBLANES
        h = u[lo:lo + SUBLANES] + a[lo:lo + SUBLANES] * carry
        outs[gi] = h
        carry = h[0:1] if reverse else h[SUBLANES - 1:SUBLANES]
    return jnp.concatenate(outs, axis=0), carry


def _rglru_body(xa_ref, ya_ref, cw_ref, cb_ref, gw_ref, gb_ref, lam_ref, o_ref, xc_ref, hf_ref, hb_ref):
    s = xa_ref.shape[0]
    tc = min(SCAN_CHUNK, s)
    nc = s // tc
    n = tc + 2 * HALO
    pos = lax.broadcasted_iota(jnp.int32, (tc, HEAD_DIM), 0) & (SUBLANES - 1)
    cw = cw_ref[...]
    cb = cb_ref[...]
    sp2 = (-LRU_C * LOG2_E) * jax.nn.softplus(-lam_ref[...])

    def gates(xc, d):
        w = gw_ref[0, :, d * 256:(d + 1) * 256]
        g = jax.nn.sigmoid(_dot(xc.astype(BF16), w) + gb_ref[0, :, d * 256:(d + 1) * 256])
        a = jnp.exp2(g[:, :HEAD_DIM] * sp2[d:d + 1])
        om = 1.0 - a * a
        root = om * lax.rsqrt(jnp.maximum(om, SQRT_GUARD))
        u = root * (g[:, HEAD_DIM:] * xc)
        return a, u

    def conv(c, carry):
        r0 = pl.multiple_of(c * tc, tc)
        x = xa_ref[pl.ds(r0, tc), :].astype(F32)
        rp = pl.multiple_of(jnp.maximum(r0 - HALO, 0), HALO)
        rn = pl.multiple_of(jnp.minimum(r0 + tc, s - HALO), HALO)
        prev = jnp.where(c > 0, xa_ref[pl.ds(rp, HALO), :].astype(F32), 0.0)
        nxt = jnp.where(c < nc - 1, xa_ref[pl.ds(rn, HALO), :].astype(F32), 0.0)
        cat = jnp.concatenate([prev, x, nxt], axis=0)
        xc_ref[pl.ds(r0, tc), :] = (
            cb + cw[0:1] * pltpu.roll(cat, 1, 0)[HALO:HALO + tc] + cw[1:2] * x
            + cw[2:3] * pltpu.roll(cat, n - 1, 0)[HALO:HALO + tc]
            + cw[3:4] * pltpu.roll(cat, n - 2, 0)[HALO:HALO + tc])
        return carry

    lax.fori_loop(0, nc, conv, 0)

    def scan(i, carry):
        c_f, c_b = carry
        rf = pl.multiple_of(i * tc, tc)
        rb = pl.multiple_of((nc - 1 - i) * tc, tc)
        a, u = gates(xc_ref[pl.ds(rf, tc), :], 0)
        a, u = _scan8(a, u, pos, False)
        h, c_f = _chain8(a, u, c_f, False)
        hf_ref[pl.ds(rf, tc), :] = h
        a, u = gates(xc_ref[pl.ds(rb, tc), :], 1)
        a, u = _scan8(a, u, pos, True)
        h, c_b = _chain8(a, u, c_b, True)
        hb_ref[pl.ds(rb, tc), :] = h
        return c_f, c_b

    zero = jnp.zeros((1, HEAD_DIM), F32)
    lax.fori_loop(0, nc, scan, (zero, zero))

    def fin(c, carry):
        r0 = pl.multiple_of(c * tc, tc)
        y = jax.nn.gelu(ya_ref[pl.ds(r0, tc), :].astype(F32))
        o_ref[pl.ds(r0, tc), :] = ((hf_ref[pl.ds(r0, tc), :] + hb_ref[pl.ds(r0, tc), :]) * y).astype(o_ref.dtype)
        return carry

    lax.fori_loop(0, nc, fin, 0)


def _rglru(pbf, lp, bsz, s):
    t = pbf.shape[0]
    ca, cy = BF_A // HEAD_DIM, (BF_A + MIX_W) // HEAD_DIM
    return pl.pallas_call(
        _rglru_body,
        grid=(bsz, N_HEADS),
        in_specs=[
            pl.BlockSpec((s, HEAD_DIM), lambda b, h: (b, ca + h)),
            pl.BlockSpec((s, HEAD_DIM), lambda b, h: (b, cy + h)),
            pl.BlockSpec((4, HEAD_DIM), lambda b, h: (0, h)),
            pl.BlockSpec((1, HEAD_DIM), lambda b, h: (0, h)),
            pl.BlockSpec((1, HEAD_DIM, 4 * HEAD_DIM), lambda b, h: (h, 0, 0)),
            pl.BlockSpec((1, 1, 4 * HEAD_DIM), lambda b, h: (h, 0, 0)),
            pl.BlockSpec((2, HEAD_DIM), lambda b, h: (0, h)),
        ],
        out_specs=pl.BlockSpec((s, HEAD_DIM), lambda b, h: (b, h)),
        out_shape=jax.ShapeDtypeStruct((t, MIX_W), BF16),
        scratch_shapes=[pltpu.VMEM((s, HEAD_DIM), F32)] * 3,
        compiler_params=_cp(("parallel", "parallel")),
        name="rglru",
    )(pbf, pbf, lp["conv_a_w"], lp["conv_a_b"], lp["rglru_w"], lp["rglru_b"], lp["rglru_lam"])


_Steps = collections.namedtuple("_Steps", "init step fin carry0")


def _run_steps(n, part):
    part.init()

    def body(i, carry):
        carry, commit = part.step(i, carry)
        commit()
        return carry

    lax.fori_loop(0, n, body, part.carry0)

    def fin(c, carry):
        part.fin(c)
        return carry

    lax.fori_loop(0, n, fin, 0)


def _mlstm_steps(q_ref, k_ref, v_ref, og_ref, g_ref, ng_ref, o_ref, hf_ref, hb_ref, stf_ref, stb_ref):
    s = q_ref.shape[0]
    cl = min(MLSTM_CHUNK, s)
    nc = s // cl
    ri = lax.broadcasted_iota(jnp.int32, (cl, cl), 0)
    ci = lax.broadcasted_iota(jnp.int32, (cl, cl), 1)
    masks = (ri >= ci, ri <= ci)
    tris = (masks[0].astype(BF16), masks[1].astype(BF16))
    grp = math.gcd(nc, MIXER_GROUP * (2 if s <= MLSTM_BIG_GROUP_SEQ else 1))
    rows = grp * cl
    mask_rows = tuple(jnp.concatenate([m] * grp, axis=0) for m in masks)
    ones_v = jnp.ones((rows, HEAD_DIM), BF16)
    kscale = HEAD_DIM ** -0.5
    twice = lambda a: jnp.concatenate([a, a], axis=1)
    per_chunk = lambda x: [x[g * cl:(g + 1) * cl] for g in range(grp)]
    cat = lambda xs: jnp.concatenate(xs, axis=0)
    chunks3 = lambda x: x.reshape(grp, cl, x.shape[-1])

    def load(r0, st_ref):
        return (g_ref[pl.ds(r0, rows), :], q_ref[pl.ds(r0, rows), :], k_ref[pl.ds(r0, rows), :],
                v_ref[pl.ds(r0, rows), :], st_ref[...])

    def group(vals, d, m_st):
        g, q, k, v, st = vals
        ig = jnp.broadcast_to(g[:, 2 * d:2 * d + 1], (rows, HEAD_DIM)) * LOG2_E
        lf = _log2_sigmoid(jnp.broadcast_to(g[:, 2 * d + 1:2 * d + 2], (rows, HEAD_DIM)))
        b = cat([_cumsum_mm(tris[d], x) for x in per_chunk(lf)])
        tot3 = chunks3(b)[:, 0:1] if d else chunks3(b)[:, cl - 1:cl]
        c = ig - b
        lw3 = tot3 + chunks3(c)
        m_loc3 = jnp.max(lw3, axis=1, keepdims=True)
        kf = k.astype(F32) * kscale
        kw = (chunks3(kf) * jnp.exp2(lw3 - m_loc3)).astype(BF16).reshape(rows, HEAD_DIM)
        kb = kf.astype(BF16)
        v1 = jnp.concatenate([v, ones_v], axis=1)
        cn_loc = [_dot_tn(a, w) for a, w in zip(per_chunk(kw), per_chunk(v1))]
        dm = jnp.where(mask_rows[d], cat([x.T for x in per_chunk(c)]), -jnp.inf)
        m_rel = jnp.max(dm, axis=1, keepdims=True)
        qk = cat([_dot_nt(a, w) for a, w in zip(per_chunk(q), per_chunk(kb))])
        am = (jnp.exp2(dm - m_rel) * qk).astype(BF16)
        nd_intra = cat([_dot(a, w) for a, w in zip(per_chunk(am), per_chunk(v1))])
        outs = [None] * grp
        for gi in (range(grp - 1, -1, -1) if d else range(grp)):
            sl = slice(gi * cl, (gi + 1) * cl)
            nd_inter = _dot(q[sl], st.astype(BF16))
            mu = jnp.maximum(m_rel[sl], m_st)
            f_i = jnp.exp2(m_rel[sl] - mu)
            f_s = jnp.exp2(m_st - mu)
            num = f_i * nd_intra[sl, :HEAD_DIM] + f_s * nd_inter[:, :HEAD_DIM]
            den = f_i * nd_intra[sl, HEAD_DIM:] + f_s * nd_inter[:, HEAD_DIM:]
            outs[gi] = num / jnp.maximum(jnp.abs(den), jnp.exp2(-(b[sl] + mu)))
            tot, m_loc = tot3[gi], m_loc3[gi]
            m_new = jnp.maximum(tot + m_st, m_loc)
            st = twice(jnp.exp2(tot + m_st - m_new)) * st + twice(jnp.exp2(m_loc - m_new)) * cn_loc[gi]
            m_st = m_new
        return cat(outs), m_st, st

    m0 = jnp.full((1, HEAD_DIM), STAB_INIT, F32)

    def init():
        stf_ref[...] = jnp.zeros_like(stf_ref)
        stb_ref[...] = jnp.zeros_like(stb_ref)

    def step(i, carry):
        m_f, m_b = carry
        rf = pl.multiple_of(i * rows, rows)
        rb = pl.multiple_of((nc // grp - 1 - i) * rows, rows)
        vals_f = load(rf, stf_ref)
        vals_b = load(rb, stb_ref)
        out_f, m_f, st_f = group(vals_f, 0, m_f)
        out_b, m_b, st_b = group(vals_b, 1, m_b)

        def commit():
            hf_ref[pl.ds(rf, rows), :] = out_f
            hb_ref[pl.ds(rb, rows), :] = out_b
            stf_ref[...] = st_f
            stb_ref[...] = st_b

        return (m_f, m_b), commit

    def fin(c):
        r0 = pl.multiple_of(c * rows, rows)
        y = _rms(hf_ref[pl.ds(r0, rows), :] + hb_ref[pl.ds(r0, rows), :], ng_ref[...])
        og = jax.nn.sigmoid(og_ref[pl.ds(r0, rows), :].astype(F32))
        o_ref[pl.ds(r0, rows), :] = (og * y).astype(o_ref.dtype)

    return nc // grp, _Steps(init, step, fin, (m0, m0))


def _mlstm_body(*refs):
    n, part = _mlstm_steps(*refs)
    _run_steps(n, part)


def _block_ref(b, m, rev):
    c, w = b.shape
    parts = []
    for blk in range(c // (2 * m)):
        row = blk * 2 * m + (m if rev else m - 1)
        parts.append(jnp.broadcast_to(b[row:row + 1, :], (2 * m, w)))
    return parts[0] if len(parts) == 1 else jnp.concatenate(parts, axis=0)


def _hgrn_steps(layer, q_ref, ff_ref, fb_ref, i_ref, g_ref, lbl_ref, ng_ref, o_ref, of_ref, ob_ref, stf_ref,
                stb_ref):
    s = q_ref.shape[0]
    c = min(HGRN_CHUNK, s)
    nc = s // c
    band = min(HGRN_BAND, c)
    if layer > 0:
        lg = lbl_ref[...]
        e = jnp.exp(lg - jnp.max(lg, axis=0, keepdims=True))
        p = e / jnp.sum(e, axis=0, keepdims=True)
        lb = p[1:2, :]
        for r in range(2, layer + 1):
            lb = lb + p[r:r + 1, :]
        log_lb = jnp.log(lb)
        log_1m = jnp.log(1.0 - lb)
        one_m = 1.0 - lb

    grp = math.gcd(nc, MIXER_GROUP)
    rows = grp * c
    ri = lax.broadcasted_iota(jnp.int32, (c, c), 0)
    ci = lax.broadcasted_iota(jnp.int32, (c, c), 1)
    tris = ((ri >= ci).astype(BF16), (ri <= ci).astype(BF16))
    ri = lax.broadcasted_iota(jnp.int32, (rows, c), 0) & (c - 1)
    ci = lax.broadcasted_iota(jnp.int32, (rows, c), 1)
    rw = lax.broadcasted_iota(jnp.int32, (rows, HEAD_DIM), 0)
    pos = lax.broadcasted_iota(jnp.int32, (rows, 1), 0) & (band - 1)
    levels = []
    m = c // 2
    while m >= band:
        levels.append(m)
        m //= 2
    upper = {m: (rw & (2 * m - 1)) >= m for m in levels}
    same_blk = {m: (ri >> int(math.log2(2 * m))) == (ci >> int(math.log2(2 * m))) for m in levels if 2 * m < c}
    diags = ([ri - ci == dl for dl in range(band)], [ci - ri == dl for dl in range(band)])
    per_chunk = lambda x: [x[g * c:(g + 1) * c] for g in range(grp)]
    cat = lambda xs: jnp.concatenate(xs, axis=0)
    chunks3 = lambda x: x.reshape(grp, c, x.shape[-1])

    def group(r0, rev, st_ref):
        f = (fb_ref if rev else ff_ref)[pl.ds(r0, rows), :]
        e_f = jnp.exp(-jnp.abs(f))
        den = 1.0 + e_f
        lf = jnp.minimum(f, 0.0) - jnp.log(den)
        kd = jnp.where(f >= 0.0, e_f, 1.0) / den
        if layer > 0:
            x = log_1m + lf
            mx = jnp.maximum(log_lb, x)
            lf = mx + jnp.log(jnp.exp(log_lb - mx) + jnp.exp(x - mx))
            kd = one_m * kd
        b = cat([_cumsum_mm(tris[rev], x) for x in per_chunk(lf * LOG2_E)])
        tot3 = chunks3(b)[:, 0:1] if rev else chunks3(b)[:, c - 1:c]
        qs = jax.nn.silu(q_ref[pl.ds(r0, rows), :].astype(F32))
        v = i_ref[pl.ds(r0, rows), :]
        q_in = per_chunk((qs * jnp.exp2(b)).astype(BF16))
        k_out = per_chunk((chunks3(kd) * jnp.exp2(tot3 - chunks3(b))).astype(BF16).reshape(rows, HEAD_DIM))
        kv = [_dot_tn(x, y) for x, y in zip(per_chunk(v), k_out)]
        amat = None
        for m in levels:
            bref = _block_ref(b, m, rev)
            up = upper[m]
            qm = jnp.logical_not(up) if rev else up
            km = up if rev else jnp.logical_not(up)
            qt = (qs * jnp.exp2(jnp.where(qm, b - bref, -jnp.inf))).astype(BF16)
            kt = (kd * jnp.exp2(jnp.where(km, bref - b, -jnp.inf))).astype(BF16)
            am = cat([_dot_nt(x, y) for x, y in zip(per_chunk(qt), per_chunk(kt))])
            if m in same_blk:
                am = jnp.where(same_blk[m], am, 0.0)
            amat = am if amat is None else amat + am
        for dl in range(band):
            if dl == 0:
                a = jnp.sum(qs * kd, axis=1, keepdims=True)
            else:
                sh = (rows - dl) if rev else dl
                valid = (pos + dl < band) if rev else (pos >= dl)
                ex = jnp.exp2(b - pltpu.roll(b, sh, 0))
                a = jnp.where(valid, jnp.sum(qs * pltpu.roll(kd, sh, 0) * ex, axis=1, keepdims=True), 0.0)
            term = jnp.where(diags[rev][dl], a, 0.0)
            amat = term if amat is None else amat + term
        o_intra = [_dot(x, y) for x, y in zip(per_chunk(amat.astype(BF16)), per_chunk(v))]
        st = st_ref[...]
        outs = [None] * grp
        for gi in (range(grp - 1, -1, -1) if rev else range(grp)):
            outs[gi] = o_intra[gi] + _dot_nt(q_in[gi], st.astype(BF16))
            st = jnp.exp2(tot3[gi]) * st + kv[gi]
        return cat(outs), st

    def init():
        stf_ref[...] = jnp.zeros_like(stf_ref)
        stb_ref[...] = jnp.zeros_like(stb_ref)

    def step(i, carry):
        rf = pl.multiple_of(i * rows, rows)
        rb = pl.multiple_of((nc // grp - 1 - i) * rows, rows)
        out_f, st_f = group(rf, 0, stf_ref)
        out_b, st_b = group(rb, 1, stb_ref)

        def commit():
            of_ref[pl.ds(rf, rows), :] = out_f
            ob_ref[pl.ds(rb, rows), :] = out_b
            stf_ref[...] = st_f
            stb_ref[...] = st_b

        return carry, commit

    def fin(ci_):
        r0 = pl.multiple_of(ci_ * rows, rows)
        hs = of_ref[pl.ds(r0, rows), :] + ob_ref[pl.ds(r0, rows), :]
        gate = jax.nn.sigmoid(g_ref[pl.ds(r0, rows), :].astype(F32))
        o_ref[pl.ds(r0, rows), :] = (gate * _rms(hs, ng_ref[...])).astype(o_ref.dtype)

    return nc // grp, _Steps(init, step, fin, jnp.int32(0))


def _hgrn_body(layer, *refs):
    n, part = _hgrn_steps(layer, *refs)
    _run_steps(n, part)


def _head_blk(s, off):
    return pl.BlockSpec((s, HEAD_DIM), lambda b, h: (b, off + h))


def _head_row(rows):
    return pl.BlockSpec((rows, HEAD_DIM), lambda b, h: (0, h))


def _mlstm_call_parts(pbf, pfp, lp, s):
    cb, cg = BF_B // HEAD_DIM, FP_GATE // HEAD_DIM
    specs = [_head_blk(s, cb), _head_blk(s, cb + 4), _head_blk(s, cb + 8), _head_blk(s, cb + 12),
             _head_blk(s, cg), _head_row(1)]
    scratch = [pltpu.VMEM((s, HEAD_DIM), F32)] * 2 + [pltpu.VMEM((HEAD_DIM, 2 * HEAD_DIM), F32)] * 2
    return specs, (pbf, pbf, pbf, pbf, pfp, lp["mlstm_norm"]), scratch


def _hgrn_call_parts(pbf, pfp, lp, lb_logits, s):
    cd = BF_D3 // HEAD_DIM
    specs = [_head_blk(s, cd), _head_blk(s, FP_FF // HEAD_DIM), _head_blk(s, FP_FB // HEAD_DIM),
             _head_blk(s, cd + 4), _head_blk(s, cd + 8), _head_row(lb_logits.shape[0]), _head_row(1)]
    scratch = [pltpu.VMEM((s, HEAD_DIM), F32)] * 2 + [pltpu.VMEM((HEAD_DIM, HEAD_DIM), F32)] * 2
    return specs, (pbf, pfp, pfp, pbf, pbf, lb_logits, lp["hgrn_norm"]), scratch


def _mlstm_hgrn(pbf, pfp, lp, lb_logits, layer, bsz, s):
    t = pbf.shape[0]
    m_specs, m_args, m_scr = _mlstm_call_parts(pbf, pfp, lp, s)
    h_specs, h_args, h_scr = _hgrn_call_parts(pbf, pfp, lp, lb_logits, s)
    out_spec = _head_blk(s, 0)
    out_shape = jax.ShapeDtypeStruct((t, MIX_W), BF16)
    m_out = pl.pallas_call(
        _mlstm_body, grid=(bsz, N_HEADS), in_specs=m_specs, out_specs=out_spec, out_shape=out_shape,
        scratch_shapes=m_scr, compiler_params=_cp(("parallel", "parallel")), name="mlstm")(*m_args)
    h_out = pl.pallas_call(
        functools.partial(_hgrn_body, layer), grid=(bsz, N_HEADS), in_specs=h_specs, out_specs=out_spec,
        out_shape=out_shape, scratch_shapes=h_scr, compiler_params=_cp(("parallel", "parallel")),
        name="hgrn2")(*h_args)
    return m_out, h_out


def _shift3(cat, w, bias, ts):
    n = ts + 2 * HALO
    return (bias + w[0:1] * pltpu.roll(cat, 1, 0)[HALO:HALO + ts] + w[1:2] * cat[HALO:HALO + ts]
            + w[2:3] * pltpu.roll(cat, n - 1, 0)[HALO:HALO + ts])


def _hyena_prep_body(x_ref, xp_ref, xn_ref, cw_ref, cb_ref, z_ref, x1_ref):
    i = pl.program_id(1)
    ts = x_ref.shape[0]
    prev = jnp.where(i > 0, xp_ref[...].astype(F32), 0.0)
    nxt = jnp.where(i < pl.num_programs(1) - 1, xn_ref[...].astype(F32), 0.0)
    cat = jnp.concatenate([prev, x_ref[...].astype(F32), nxt], axis=0)
    u = _shift3(cat, cw_ref[...], cb_ref[...], ts)
    z_ref[...] = (u[:, 2 * MIX_W:] * u[:, :MIX_W]).astype(z_ref.dtype)
    x1_ref[...] = u[:, MIX_W:2 * MIX_W].astype(x1_ref.dtype)


def _hyena_prep(pbf, lp, bsz, s):
    t = pbf.shape[0]
    ts = min(512, s)
    per = s // ts
    hb = ts // HALO
    nb = t // HALO
    w = 3 * MIX_W
    return pl.pallas_call(
        _hyena_prep_body,
        grid=(bsz, per),
        in_specs=[
            pl.BlockSpec((ts, w), lambda b, i: (b * per + i, 0)),
            pl.BlockSpec((HALO, w), lambda b, i: (jnp.maximum((b * per + i) * hb - 1, 0), 0)),
            pl.BlockSpec((HALO, w), lambda b, i: (jnp.minimum((b * per + i + 1) * hb, nb - 1), 0)),
            pl.BlockSpec((3, w), lambda b, i: (0, 0)),
            pl.BlockSpec((1, w), lambda b, i: (0, 0)),
        ],
        out_specs=[pl.BlockSpec((ts, MIX_W), lambda b, i: (b * per + i, 0)),
                   pl.BlockSpec((ts, MIX_W), lambda b, i: (b * per + i, 0))],
        out_shape=[jax.ShapeDtypeStruct((t, MIX_W), BF16), jax.ShapeDtypeStruct((t, MIX_W), BF16)],
        compiler_params=_cp(("parallel", "arbitrary")),
        name="hyena_prep",
    )(pbf, pbf, pbf, lp["hyena_conv_w"], lp["hyena_conv_b"])


def _split_bf16(a):
    hi = a.astype(BF16)
    return jnp.stack([hi, (a - hi.astype(F32)).astype(BF16)])


def _dot3(f_hi, f_lo, x):
    x_hi = x.astype(BF16)
    x_lo = (x - x_hi.astype(F32)).astype(BF16)
    return _dot(f_hi, x_hi) + _dot(f_hi, x_lo) + _dot(f_lo, x_hi)


def _dft1_body(x_ref, f_ref, o_ref):
    jb, _, n1, r, c = o_ref.shape
    n1h = x_ref.shape[0]
    for jj in range(jb):
        x = x_ref[:, jj * r:(jj + 1) * r, :].reshape(n1h * r, c)
        o = _dot3(f_ref[0], f_ref[1], x) if f_ref.ndim == 3 else _dot(f_ref[...], x)
        o_ref[jj] = o.reshape(2, n1, r, c).astype(o_ref.dtype)


def _dft1(x4, kron1, out_dtype):
    bsz, n1h, n2, c = x4.shape
    r = kron1.shape[-1] // n1h
    n1 = kron1.shape[-2] // (2 * r)
    nj = n2 // r
    jb = math.gcd(nj, max(1, DFT_STEP_ROWS // (2 * n1)))
    return pl.pallas_call(
        _dft1_body,
        grid=(bsz, nj // jb),
        in_specs=[pl.BlockSpec((None, n1h, jb * r, c), lambda b, j: (b, 0, j, 0)),
                  pl.BlockSpec(kron1.shape, lambda b, j: (0,) * kron1.ndim, pipeline_mode=pl.Buffered(1))],
        out_specs=pl.BlockSpec((None, jb, 2, n1, r, c), lambda b, j: (b, j, 0, 0, 0, 0)),
        out_shape=jax.ShapeDtypeStruct((bsz, n2 // r, 2, n1, r, c), out_dtype),
        compiler_params=_cp(("parallel", "parallel")),
        name="dft_stage1",
    )(x4, kron1)


def _spectrum_rows(a_ref, lead, kk):
    nj = a_ref.shape[len(lead)]
    return jnp.concatenate([a_ref[lead + (j, e, kk)] for e in range(2) for j in range(nj)], axis=0)


def _dft2_conv_body(kb, a_ref, f_ref, fi_ref, g_ref, o_ref):
    nj, _, _, r, _ = a_ref.shape
    n2 = nj * r
    for kk in range(kb):
        x = _dot(f_ref[kk], _spectrum_rows(a_ref, (), kk))
        xr, xi = x[:n2], x[n2:]
        gr, gi = g_ref[kk, 0], g_ref[kk, 1]
        y = jnp.concatenate([xr * gr - xi * gi, xr * gi + xi * gr], axis=0).astype(BF16)
        bq = _dot(fi_ref[kk], y).astype(o_ref.dtype)
        for e in range(2):
            for j in range(nj):
                o_ref[j, e, kk] = bq[e * n2 + j * r:e * n2 + (j + 1) * r]


def _dft2_conv(a6, f2, f2i, gspec, kb):
    bsz, nj, _, n1, r, c = a6.shape
    n2 = nj * r
    blk = pl.BlockSpec((None, nj, 2, kb, r, c), lambda k, b: (b, 0, 0, k, 0, 0))
    return pl.pallas_call(
        functools.partial(_dft2_conv_body, kb),
        grid=(n1 // kb, bsz),
        in_specs=[blk,
                  pl.BlockSpec((kb, 2 * n2, 2 * n2), lambda k, b: (k, 0, 0)),
                  pl.BlockSpec((kb, 2 * n2, 2 * n2), lambda k, b: (k, 0, 0)),
                  pl.BlockSpec((kb, 2, n2, c), lambda k, b: (k, 0, 0, 0))],
        out_specs=blk,
        out_shape=jax.ShapeDtypeStruct(a6.shape, BF16),
        compiler_params=_cp(("parallel", "arbitrary")),
        name="dft_stage2_conv",
    )(a6, f2, f2i, gspec)


def _dft2_filter_body(kb, a_ref, f_ref, o_ref):
    n2 = a_ref.shape[1] * a_ref.shape[4]
    k0 = pl.program_id(0) * kb
    for kk in range(kb):
        sgn = (1 - 2 * ((k0 + kk) & 1)).astype(F32)
        xs = [_dot3(f_ref[0, kk], f_ref[1, kk], _spectrum_rows(a_ref, (part,), kk)) for part in range(2)]
        x = xs[0] + sgn * xs[1]
        o_ref[kk, 0] = x[:n2]
        o_ref[kk, 1] = x[n2:]


def _dft2_filter(a6, f2, kb):
    _, nj, _, n1, r, c = a6.shape
    n2 = nj * r
    return pl.pallas_call(
        functools.partial(_dft2_filter_body, kb),
        grid=(n1 // kb,),
        in_specs=[pl.BlockSpec((2, nj, 2, kb, r, c), lambda k: (0, 0, 0, k, 0, 0)),
                  pl.BlockSpec((2, kb, 2 * n2, 2 * n2), lambda k: (0, k, 0, 0))],
        out_specs=pl.BlockSpec((kb, 2, n2, c), lambda k: (k, 0, 0, 0)),
        out_shape=jax.ShapeDtypeStruct((n1, 2, n2, c), F32),
        compiler_params=_cp(("parallel",)),
        name="dft_stage2_filter",
    )(a6, f2)


def _dft3_body(b_ref, f_ref, x1_ref, z_ref, sk_ref, o_ref):
    jb, _, n1, r, c = b_ref.shape
    n1h = o_ref.shape[0]
    for jj in range(jb):
        rows = slice(jj * r, (jj + 1) * r)
        y = _dot(f_ref[...], b_ref[jj].reshape(2 * n1 * r, c)).reshape(n1h, r, c)
        z = z_ref[:, rows, :].astype(F32)
        o_ref[:, rows, :] = (x1_ref[:, rows, :].astype(F32) * (y + sk_ref[...] * z)).astype(o_ref.dtype)


def _dft3(b6, kron3, x1_4, z4, skip):
    bsz, nj, _, n1, r, c = b6.shape
    n1h, n2 = n1 // 2, nj * r
    assert r == kron3.shape[0] // n1h
    jb = math.gcd(nj, max(1, DFT_STEP_ROWS // (2 * n1)))
    row = pl.BlockSpec((None, n1h, jb * r, c), lambda b, j: (b, 0, j, 0))
    return pl.pallas_call(
        _dft3_body,
        grid=(bsz, nj // jb),
        in_specs=[pl.BlockSpec((None, jb, 2, n1, r, c), lambda b, j: (b, j, 0, 0, 0, 0)),
                  pl.BlockSpec(kron3.shape, lambda b, j: (0, 0), pipeline_mode=pl.Buffered(1)),
                  row, row,
                  pl.BlockSpec((1, c), lambda b, j: (0, 0))],
        out_specs=row,
        out_shape=jax.ShapeDtypeStruct((bsz, n1h, n2, c), BF16),
        compiler_params=_cp(("parallel", "parallel")),
        name="dft_stage3",
    )(b6, kron3, x1_4, z4, skip)


def _hyena_filter_body(seq_len, pos_ref, sc_ref, bands_ref, w1t_ref, w1c_ref, w1s_ref, b1_ref, fr1_ref,
                       w2_ref, b2_ref, fr2_ref, w3_ref, rate_ref, o_ref):
    pos = pos_ref[...]
    t = pos * (1.0 / (seq_len - 1))
    arg = (pos * (2.0 * math.pi / seq_len)) * bands_ref[...]
    hdot = lambda a, b: jnp.dot(a, b, preferred_element_type=F32, precision=HIGHEST)
    pre = t * w1t_ref[...] + hdot(jnp.cos(arg), w1c_ref[...]) - hdot(jnp.sin(arg), w1s_ref[...]) + b1_ref[...]
    hid = jnp.sin(fr1_ref[...] * pre)
    hid = jnp.sin(fr2_ref[...] * (hdot(hid, w2_ref[...]) + b2_ref[...]))
    o_ref[...] = hdot(hid, w3_ref[...]) * jnp.exp(-t * rate_ref[...]) * sc_ref[...]


def _hyena_filter(lp, s):
    ts = min(256, s)
    r = 2 * s + ts
    pos = np.concatenate([np.arange(s), s - np.arange(s), np.zeros(ts)]).astype(np.float32)[:, None]
    sc = np.ones((r, 1), np.float32)
    sc[s] = 0.0
    pad = HEAD_DIM - HYENA_HID
    bands = np.zeros((1, HEAD_DIM), np.float32)
    bands[0, :HYENA_BANDS] = np.linspace(1e-4, HYENA_BANDS - 1, HYENA_BANDS)
    rate = np.abs(np.linspace(math.log(HYENA_TARGET) / HYENA_FAST, math.log(HYENA_TARGET) / HYENA_SLOW, MIX_W))
    rate = np.tile(rate, 2).astype(np.float32)[None, :]
    half = lambda i: (0, jnp.where(i < s // ts, 0, 1))
    w1 = lp["hyena_w1"]
    padc = lambda a: jnp.pad(a, ((0, 0), (0, pad)))
    w1t = padc(w1[0:1])
    w1c = jnp.pad(w1[1:1 + HYENA_BANDS], ((0, HEAD_DIM - HYENA_BANDS), (0, pad)))
    w1s = jnp.pad(w1[1 + HYENA_BANDS:], ((0, HEAD_DIM - HYENA_BANDS), (0, pad)))
    w2 = jnp.pad(lp["hyena_w2"], ((0, pad), (0, pad)))
    w3 = jnp.pad(lp["hyena_w3"], ((0, pad), (0, 0)))
    vec = lambda a: padc(a[None, :])
    full = lambda a: pl.BlockSpec(a.shape, lambda i: (0, 0))
    args = [jnp.asarray(bands), w1t, w1c, w1s, vec(lp["hyena_b1"]), vec(lp["hyena_freq1"]), w2,
            vec(lp["hyena_b2"]), vec(lp["hyena_freq2"])]
    out = pl.pallas_call(
        functools.partial(_hyena_filter_body, s),
        grid=(r // ts,),
        in_specs=[pl.BlockSpec((ts, 1), lambda i: (i, 0)), pl.BlockSpec((ts, 1), lambda i: (i, 0))]
                 + [full(a) for a in args]
                 + [pl.BlockSpec((HEAD_DIM, MIX_W), half), pl.BlockSpec((1, MIX_W), half)],
        out_specs=pl.BlockSpec((ts, MIX_W), lambda i: (i, 0)),
        out_shape=jax.ShapeDtypeStruct((r, MIX_W), F32),
        compiler_params=_cp(("parallel",)),
        name="hyena_filter",
    )(jnp.asarray(pos), jnp.asarray(sc), *args, w3, jnp.asarray(rate))
    return out[:s], out[s:2 * s], out[2 * s:2 * s + 1]


def _dft_tables(s):
    n = 2 * s
    n2 = min(DFT_N2, s // 8)
    n1 = n // n2
    n1h = n1 // 2
    two_pi = 2.0 * math.pi
    k1 = jnp.arange(n1, dtype=jnp.int32)
    a1 = (two_pi / n1) * ((k1[:, None] * k1[None, :n1h]) % n1).astype(F32)
    f1 = jnp.concatenate([jnp.cos(a1), -jnp.sin(a1)], axis=0)
    f1i = jnp.concatenate([jnp.cos(a1).T, -jnp.sin(a1).T], axis=1) * (1.0 / n)
    j = jnp.arange(n2, dtype=jnp.int32)
    ph = (j[None, :, None] * j[None, None, :] * n1 + j[None, None, :] * k1[:, None, None]) % n
    a2 = (two_pi / n) * ph.astype(F32)
    tr, ti = jnp.cos(a2), -jnp.sin(a2)
    f2 = jnp.concatenate([jnp.concatenate([tr, -ti], axis=2), jnp.concatenate([ti, tr], axis=2)], axis=1)
    kron = lambda m, r: jnp.kron(m, jnp.eye(r, dtype=F32))
    r32, r16 = min(SUBLANES, n2), min(2 * SUBLANES, n2)
    return dict(n1=n1, n2=n2, f2_split=_split_bf16(f2), f2_bf=f2.astype(BF16),
                f2i_bf=jnp.swapaxes(f2, 1, 2).astype(BF16), kron1_split=_split_bf16(kron(f1, r32)),
                kron1_bf=kron(f1, r16).astype(BF16), kron3_bf=kron(f1i, r16).astype(BF16))


def _hyena_spectrum(lp, s, tab):
    gpos, gneg, hb0 = _hyena_filter(lp, s)
    n1, n2 = tab["n1"], tab["n2"]
    a = _dft1(jnp.stack([gpos, gneg]).reshape(2, n1 // 2, n2, MIX_W), tab["kron1_split"], F32)
    return _dft2_filter(a, tab["f2_split"], min(2, n1)), hb0


def _hyena(pbf, lp, gspec, hb0, tab, bsz, s):
    n1, n2 = tab["n1"], tab["n2"]
    z, x1 = _hyena_prep(pbf, lp, bsz, s)
    z4 = z.reshape(bsz, n1 // 2, n2, MIX_W)
    a = _dft1(z4, tab["kron1_bf"], BF16)
    bq = _dft2_conv(a, tab["f2_bf"], tab["f2i_bf"], gspec, min(8, n1))
    out = _dft3(bq, tab["kron3_bf"], x1.reshape(z4.shape), z4, lp["hyena_skip"][None, :] + hb0)
    return out.reshape(bsz * s, MIX_W)


def _merge_body(h_ref, g0_ref, g1_ref, g2_ref, g3_ref, ba_ref, bb_ref, bc_ref, bd_ref, wb_ref, wo_ref, o_ref):
    merged = None
    for idx, (g_ref, br) in enumerate(zip((g0_ref, g1_ref, g2_ref, g3_ref), (ba_ref, bb_ref, bc_ref, bd_ref))):
        term = jax.nn.sigmoid(g_ref[...].astype(F32)) * _dot(br[...], wb_ref[idx])
        merged = term if merged is None else merged + term
    o_ref[...] = h_ref[...] + _dot(merged.astype(BF16), wo_ref[...])


def _merge(h, pbf, branches, lp):
    t, d = h.shape
    tm = min(512, t)
    gcol = BF_G // d
    row = lambda w: pl.BlockSpec((tm, w), lambda i: (i, 0))
    gate = lambda idx: pl.BlockSpec((tm, d), lambda i: (i, gcol + idx))
    return pl.pallas_call(
        _merge_body,
        grid=(t // tm,),
        in_specs=[row(d), gate(0), gate(1), gate(2), gate(3),
                  row(MIX_W), row(MIX_W), row(MIX_W), row(MIX_W),
                  pl.BlockSpec(lp["w_branch"].shape, lambda i: (0, 0, 0)),
                  pl.BlockSpec(lp["w_out"].shape, lambda i: (0, 0))],
        out_specs=row(d),
        out_shape=jax.ShapeDtypeStruct((t, d), F32),
        compiler_params=_cp(("parallel",)),
        name="merge",
    )(h, pbf, pbf, pbf, pbf, *branches, lp["w_branch"], lp["w_out"])


def _ffn_body(per, final, h_ref, hp_ref, hn_ref, p_ref, gf_ref, wu_ref, cw_ref, cb_ref, wd_ref, gp_ref, wg_ref,
              wp_ref, gl_ref, o_ref):
    i = pl.program_id(0)
    tm = h_ref.shape[0]
    dff = wd_ref.shape[0]
    gf = gf_ref[...]
    h = h_ref[...]
    first = (i % per) == 0
    last = (i % per) == per - 1
    xp = jnp.where(first, 0.0, _rms(hp_ref[...], gf))
    xn = jnp.where(last, 0.0, _rms(hn_ref[...], gf))
    cat = jnp.concatenate([xp, _rms(h, gf), xn], axis=0).astype(BF16)
    cb_w = min(2048, dff)
    acc = jnp.zeros_like(h)
    for cb in range(dff // cb_w):
        lo, lo2 = cb * cb_w, dff + cb * cb_w
        u1 = _shift3(_dot(cat, wu_ref[:, lo:lo + cb_w]), cw_ref[:, lo:lo + cb_w], cb_ref[:, lo:lo + cb_w], tm)
        u2 = _shift3(_dot(cat, wu_ref[:, lo2:lo2 + cb_w]), cw_ref[:, lo2:lo2 + cb_w], cb_ref[:, lo2:lo2 + cb_w], tm)
        acc = acc + _dot((jax.nn.gelu(u1) * u2).astype(BF16), wd_ref[lo:lo + cb_w, :])
    h = h + acc
    gate = jax.nn.sigmoid(_dot(_rms(h, gp_ref[...]).astype(BF16), wg_ref[...]))
    h = h + gate * _dot(p_ref[...].astype(BF16), wp_ref[...])
    if final:
        h = _rms(h, gl_ref[...])
    o_ref[...] = h


def _ffn(h, p, lp, final_norm, s, final):
    t, d = h.shape
    tm = min(512, s)
    per = s // tm
    hb = tm // HALO
    nb = t // HALO
    full = lambda a: pl.BlockSpec(a.shape, lambda i: (0,) * a.ndim)
    row = lambda w: pl.BlockSpec((tm, w), lambda i: (i, 0))
    ws = [lp["norm_ffn"], lp["w_up"], lp["ffn_conv_w"], lp["ffn_conv_b"], lp["w_down"], lp["norm_ple"],
          lp["w_ple_gate"], lp["w_ple"], final_norm]
    return pl.pallas_call(
        functools.partial(_ffn_body, per, final),
        grid=(t // tm,),
        in_specs=[row(d),
                  pl.BlockSpec((HALO, d), lambda i: (jnp.maximum(i * hb - 1, 0), 0)),
                  pl.BlockSpec((HALO, d), lambda i: (jnp.minimum((i + 1) * hb, nb - 1), 0)),
                  row(p.shape[1])] + [full(a) for a in ws],
        out_specs=row(d),
        out_shape=jax.ShapeDtypeStruct((t, d), F32),
        compiler_params=_cp(("parallel",)),
        name="ffn_ple",
    )(h, h, h, p, *ws)


def _prepare_params(prm):
    d_model = prm["w_in"].shape[1]
    depth = prm["w_in"].shape[0]
    off_b = 2 * MIX_W
    off_gate = off_b + 4 * MIX_W
    off_c = off_gate + 4 * N_HEADS
    off_d = off_c + 3 * MIX_W
    off_g = off_d + 5 * MIX_W

    def regroup(a):
        sl = lambda lo, w: a[..., lo:lo + w]
        bf = jnp.concatenate([sl(off_g, 4 * d_model), sl(0, 2 * MIX_W), sl(off_b, 4 * MIX_W), sl(off_d, MIX_W),
                              sl(off_d + 3 * MIX_W, 2 * MIX_W)], axis=-1)
        gates = sl(off_gate, 4 * N_HEADS).reshape(a.shape[:-1] + (4, N_HEADS))
        gates = jnp.swapaxes(gates, -1, -2)
        gates = jnp.pad(gates, [(0, 0)] * (gates.ndim - 1) + [(0, HEAD_DIM - 4)])
        fp = jnp.concatenate([sl(off_d + MIX_W, 2 * MIX_W), gates.reshape(a.shape[:-1] + (N_HEADS * HEAD_DIM,))],
                             axis=-1)
        return bf, sl(off_c, 3 * MIX_W), fp

    w_bf, w_hy, w_fp = regroup(prm["w_in"])
    b_bf, b_hy, b_fp = regroup(prm["b_in"][:, None, :])
    gw = jnp.transpose(prm["rglru_w"], (0, 3, 4, 1, 2, 5)).reshape(depth, N_HEADS, HEAD_DIM, 4 * HEAD_DIM)
    gb = prm["rglru_b"].reshape(depth, 2, 2, N_HEADS, HEAD_DIM)
    gb = jnp.transpose(gb, (0, 3, 1, 2, 4)).reshape(depth, N_HEADS, 1, 4 * HEAD_DIM)
    row = lambda a: a[:, None, :]
    out = dict(
        norm_mix=row(prm["norm_mix"]), w_bf=w_bf.astype(BF16), b_bf=b_bf, w_hy=w_hy.astype(BF16), b_hy=b_hy,
        w_fp=w_fp.astype(BF16), b_fp=b_fp,
        conv_a_w=prm["conv_a_w"], conv_a_b=row(prm["conv_a_b"]), rglru_w=gw.astype(BF16), rglru_b=gb,
        rglru_lam=prm["rglru_lam"], mlstm_norm=row(prm["mlstm_norm"]),
        hyena_conv_w=prm["hyena_conv_w"], hyena_conv_b=row(prm["hyena_conv_b"]),
        hyena_w1=prm["hyena_w1"], hyena_b1=prm["hyena_b1"], hyena_freq1=prm["hyena_freq1"],
        hyena_w2=prm["hyena_w2"], hyena_b2=prm["hyena_b2"], hyena_freq2=prm["hyena_freq2"],
        hyena_w3=prm["hyena_w3"], hyena_skip=prm["hyena_skip"], hgrn_norm=row(prm["hgrn_norm"]),
        w_branch=prm["w_branch"].astype(BF16), w_out=prm["w_out"].astype(BF16),
        norm_ffn=row(prm["norm_ffn"]), w_up=prm["w_up"].astype(BF16), ffn_conv_w=prm["ffn_conv_w"],
        ffn_conv_b=row(prm["ffn_conv_b"]), w_down=prm["w_down"].astype(BF16), norm_ple=row(prm["norm_ple"]),
        w_ple_gate=prm["w_ple_gate"].astype(BF16), w_ple=prm["w_ple"].astype(BF16),
    )
    return out


def _trunk(x, p, prm, lb_logits, final_norm):
    bsz, s, d = x.shape
    depth = p.shape[0]
    h = x.reshape(bsz * s, d)
    tab = _dft_tables(s)
    for i in range(depth):
        lp = {k: v[i] for k, v in prm.items()}
        pbf, phy, pfp = _proj(h, lp["norm_mix"], [(lp["w_bf"], lp["b_bf"]), (lp["w_hy"], lp["b_hy"]),
                                                  (lp["w_fp"], lp["b_fp"])], (BF16, BF16, F32))
        gspec, hb0 = _hyena_spectrum(lp, s, tab)
        br_b, br_d = _mlstm_hgrn(pbf, pfp, lp, lb_logits, i, bsz, s)
        branches = (_rglru(pbf, lp, bsz, s), br_b, _hyena(phy, lp, gspec, hb0, tab, bsz, s), br_d)
        h = _merge(h, pbf, branches, lp)
        h = _ffn(h, p[i].reshape(bsz * s, -1), lp, final_norm, s, i == depth - 1)
    return h.reshape(bsz, s, d)


def kernel(x_prompt, x_sample, p_prompt, p_sample, norm_mix, w_in, b_in, conv_a_w, conv_a_b, rglru_w, rglru_b, rglru_lam, mlstm_norm, hyena_conv_w, hyena_conv_b, hyena_w1, hyena_b1, hyena_freq1, hyena_w2, hyena_b2, hyena_freq2, hyena_w3, hyena_skip, hgrn_lb_logits, hgrn_norm, w_branch, w_out, norm_ffn, w_up, ffn_conv_w, ffn_conv_b, w_down, norm_ple, w_ple_gate, w_ple, final_norm):
    prm = _prepare_params(dict(
        norm_mix=norm_mix, w_in=w_in, b_in=b_in, conv_a_w=conv_a_w, conv_a_b=conv_a_b, rglru_w=rglru_w,
        rglru_b=rglru_b, rglru_lam=rglru_lam, mlstm_norm=mlstm_norm, hyena_conv_w=hyena_conv_w,
        hyena_conv_b=hyena_conv_b, hyena_w1=hyena_w1, hyena_b1=hyena_b1, hyena_freq1=hyena_freq1,
        hyena_w2=hyena_w2, hyena_b2=hyena_b2, hyena_freq2=hyena_freq2, hyena_w3=hyena_w3,
        hyena_skip=hyena_skip, hgrn_norm=hgrn_norm, w_branch=w_branch, w_out=w_out, norm_ffn=norm_ffn,
        w_up=w_up, ffn_conv_w=ffn_conv_w, ffn_conv_b=ffn_conv_b, w_down=w_down, norm_ple=norm_ple,
        w_ple_gate=w_ple_gate, w_ple=w_ple))
    fn = final_norm[None, :]
    y_prompt = _trunk(x_prompt, p_prompt, prm, hgrn_lb_logits, fn)
    y_sample = _trunk(x_sample, p_sample, prm, hgrn_lb_logits, fn)
    return (y_prompt, y_sample)
```

```python
import collections
import functools
import math

import numpy as np
import jax
import jax.numpy as jnp
from jax import lax
from jax.experimental import pallas as pl
from jax.experimental.pallas import tpu as pltpu

F32 = jnp.float32
BF16 = jnp.bfloat16
HIGHEST = lax.Precision.HIGHEST

N_HEADS = 4
HEAD_DIM = 128
MIX_W = N_HEADS * HEAD_DIM
LRU_C = 8.0
LOG2_E = math.log2(math.e)
GELU_C1 = -2.0 * math.sqrt(2.0 / math.pi) * LOG2_E
GELU_C3 = 0.044715 * GELU_C1
SQRT_GUARD = 1e-30
EPS = 1e-6
STAB_INIT = -1e30
HYENA_BANDS = 16
HYENA_HID = 64
HYENA_FAST = 0.3
HYENA_SLOW = 1.5
HYENA_TARGET = 1e-2

MLSTM_CHUNK = 128
HGRN_CHUNK = 128
MIXER_GROUP = 4
MLSTM_BIG_GROUP_SEQ = 4096
HGRN_BAND = 4
SCAN_CHUNK = 512
HALO = 16
SUBLANES = 8
PROJ_COLS = 2304
DFT_N2 = 128
DFT_STEP_ROWS = 512
V7X_VMEM_LIMIT = 56 * 1024 * 1024
V7X_MXU_COLS = 256

BF_G, BF_A, BF_B, BF_D3 = 0, 4096, 5120, 7168
FP_FF, FP_FB, FP_GATE = 0, 512, 1024


def _cp(sem, vmem=V7X_VMEM_LIMIT):
    return pltpu.CompilerParams(dimension_semantics=sem, vmem_limit_bytes=vmem)


def _rms(x, g):
    return x * lax.rsqrt(jnp.mean(x * x, axis=-1, keepdims=True) + EPS) * g


def _log2_sigmoid(x):
    xs = x * LOG2_E
    return jnp.minimum(xs, 0.0) - jnp.log(1.0 + jnp.exp2(-jnp.abs(xs))) * LOG2_E


def _gelu(x):
    return x / (1.0 + jnp.exp2(x * (GELU_C1 + GELU_C3 * (x * x))))


def _dot(a, b):
    return jnp.dot(a, b, preferred_element_type=F32)


def _dot_nt(a, b):
    return lax.dot_general(a, b, (((1,), (1,)), ((), ())), preferred_element_type=F32)


def _dot_tn(a, b):
    return lax.dot_general(a, b, (((0,), (0,)), ((), ())), preferred_element_type=F32)


def _cumsum_mm(tri, x):
    hi = x.astype(BF16)
    lo = (x - hi.astype(F32)).astype(BF16)
    return _dot(tri, hi) + _dot(tri, lo)


def _col_chunk(n):
    return max(w for w in range(V7X_MXU_COLS, PROJ_COLS + 1, V7X_MXU_COLS) if n % w == 0)


def _proj_body(n_out, h_ref, g_ref, *refs):
    xn = _rms(h_ref[...], g_ref[...]).astype(BF16)
    for k in range(n_out):
        w_ref, b_ref, o_ref = refs[2 * k], refs[2 * k + 1], refs[2 * n_out + k]
        n = w_ref.shape[1]
        tn = _col_chunk(n)
        for lo in range(0, n, tn):
            o_ref[:, lo:lo + tn] = (_dot(xn, w_ref[:, lo:lo + tn]) + b_ref[:, lo:lo + tn]).astype(o_ref.dtype)


def _proj(h, g, weights, out_dtypes):
    t, d = h.shape
    tm = min(256, t)
    once = lambda a: pl.BlockSpec(a.shape, lambda i: (0, 0), pipeline_mode=pl.Buffered(1))
    row = lambda n: pl.BlockSpec((tm, n), lambda i: (i, 0))
    flat = [a for wb in weights for a in wb]
    return pl.pallas_call(
        functools.partial(_proj_body, len(weights)),
        grid=(t // tm,),
        in_specs=[row(d), once(g)] + [once(a) for a in flat],
        out_specs=[row(w.shape[1]) for w, _ in weights],
        out_shape=[jax.ShapeDtypeStruct((t, w.shape[1]), dt) for (w, _), dt in zip(weights, out_dtypes)],
        compiler_params=_cp(("parallel",)),
        name="in_proj",
    )(h, g, *flat)


def _scan8(a, u, pos, reverse):
    n = a.shape[0]
    d = 1
    while d < SUBLANES:
        if reverse:
            m = pos < SUBLANES - d
            sh = n - d
        else:
            m = pos >= d
            sh = d
        a_s = jnp.where(m, pltpu.roll(a, sh, 0), 1.0)
        u_s = jnp.where(m, pltpu.roll(u, sh, 0), 0.0)
        u = a * u_s + u
        a = a * a_s
        d *= 2
    return a, u


def _chain8(a, u, carry, reverse):
    groups = a.shape[0] // SUBLANES
    outs = [None] * groups
    for gi in (range(groups - 1, -1, -1) if reverse else range(groups)):
        lo = gi * SUBLANES
        h = u[lo:lo + SUBLANES] + a[lo:lo + SUBLANES] * carry
        outs[gi] = h
        carry = h[0:1] if reverse else h[SUBLANES - 1:SUBLANES]
    return jnp.concatenate(outs, axis=0), carry


def _rglru_body(xa_ref, ya_ref, cw_ref, cb_ref, gw_ref, gb_ref, lam_ref, o_ref, xc_ref, hf_ref, hb_ref):
    s = xa_ref.shape[0]
    tc = min(SCAN_CHUNK, s)
    nc = s // tc
    n = tc + 2 * HALO
    pos = lax.broadcasted_iota(jnp.int32, (tc, HEAD_DIM), 0) & (SUBLANES - 1)
    cw = cw_ref[...]
    cb = cb_ref[...]
    sp2 = (-LRU_C * LOG2_E) * jax.nn.softplus(-lam_ref[...])

    def gates(xc, d):
        w = gw_ref[0, :, d * 256:(d + 1) * 256]
        g = jax.nn.sigmoid(_dot(xc.astype(BF16), w) + gb_ref[0, :, d * 256:(d + 1) * 256])
        a = jnp.exp2(g[:, :HEAD_DIM] * sp2[d:d + 1])
        om = 1.0 - a * a
        root = om * lax.rsqrt(jnp.maximum(om, SQRT_GUARD))
        u = root * (g[:, HEAD_DIM:] * xc)
        return a, u

    def conv(c, carry):
        r0 = pl.multiple_of(c * tc, tc)
        x = xa_ref[pl.ds(r0, tc), :].astype(F32)
        rp = pl.multiple_of(jnp.maximum(r0 - HALO, 0), HALO)
        rn = pl.multiple_of(jnp.minimum(r0 + tc, s - HALO), HALO)
        prev = jnp.where(c > 0, xa_ref[pl.ds(rp, HALO), :].astype(F32), 0.0)
        nxt = jnp.where(c < nc - 1, xa_ref[pl.ds(rn, HALO), :].astype(F32), 0.0)
        cat = jnp.concatenate([prev, x, nxt], axis=0)
        xc_ref[pl.ds(r0, tc), :] = (
            cb + cw[0:1] * pltpu.roll(cat, 1, 0)[HALO:HALO + tc] + cw[1:2] * x
            + cw[2:3] * pltpu.roll(cat, n - 1, 0)[HALO:HALO + tc]
            + cw[3:4] * pltpu.roll(cat, n - 2, 0)[HALO:HALO + tc])
        return carry

    lax.fori_loop(0, nc, conv, 0)

    def scan(i, carry):
        c_f, c_b = carry
        rf = pl.multiple_of(i * tc, tc)
        rb = pl.multiple_of((nc - 1 - i) * tc, tc)
        a, u = gates(xc_ref[pl.ds(rf, tc), :], 0)
        a, u = _scan8(a, u, pos, False)
        h, c_f = _chain8(a, u, c_f, False)
        hf_ref[pl.ds(rf, tc), :] = h
        a, u = gates(xc_ref[pl.ds(rb, tc), :], 1)
        a, u = _scan8(a, u, pos, True)
        h, c_b = _chain8(a, u, c_b, True)
        hb_ref[pl.ds(rb, tc), :] = h
        return c_f, c_b

    zero = jnp.zeros((1, HEAD_DIM), F32)
    lax.fori_loop(0, nc, scan, (zero, zero))

    def fin(c, carry):
        r0 = pl.multiple_of(c * tc, tc)
        y = _gelu(ya_ref[pl.ds(r0, tc), :].astype(F32))
        o_ref[pl.ds(r0, tc), :] = ((hf_ref[pl.ds(r0, tc), :] + hb_ref[pl.ds(r0, tc), :]) * y).astype(o_ref.dtype)
        return carry

    lax.fori_loop(0, nc, fin, 0)


def _rglru(pbf, lp, bsz, s):
    t = pbf.shape[0]
    ca, cy = BF_A // HEAD_DIM, (BF_A + MIX_W) // HEAD_DIM
    return pl.pallas_call(
        _rglru_body,
        grid=(bsz, N_HEADS),
        in_specs=[
            pl.BlockSpec((s, HEAD_DIM), lambda b, h: (b, ca + h)),
            pl.BlockSpec((s, HEAD_DIM), lambda b, h: (b, cy + h)),
            pl.BlockSpec((4, HEAD_DIM), lambda b, h: (0, h)),
            pl.BlockSpec((1, HEAD_DIM), lambda b, h: (0, h)),
            pl.BlockSpec((1, HEAD_DIM, 4 * HEAD_DIM), lambda b, h: (h, 0, 0)),
            pl.BlockSpec((1, 1, 4 * HEAD_DIM), lambda b, h: (h, 0, 0)),
            pl.BlockSpec((2, HEAD_DIM), lambda b, h: (0, h)),
        ],
        out_specs=pl.BlockSpec((s, HEAD_DIM), lambda b, h: (b, h)),
        out_shape=jax.ShapeDtypeStruct((t, MIX_W), BF16),
        scratch_shapes=[pltpu.VMEM((s, HEAD_DIM), F32)] * 3,
        compiler_params=_cp(("parallel", "parallel")),
        name="rglru",
    )(pbf, pbf, lp["conv_a_w"], lp["conv_a_b"], lp["rglru_w"], lp["rglru_b"], lp["rglru_lam"])


_Steps = collections.namedtuple("_Steps", "init step fin carry0")


def _run_steps(n, part):
    part.init()

    def body(i, carry):
        carry, commit = part.step(i, carry)
        commit()
        return carry

    lax.fori_loop(0, n, body, part.carry0)

    def fin(c, carry):
        part.fin(c)
        return carry

    lax.fori_loop(0, n, fin, 0)


def _mlstm_steps(q_ref, k_ref, v_ref, og_ref, g_ref, ng_ref, o_ref, hf_ref, hb_ref, stf_ref, stb_ref):
    s = q_ref.shape[0]
    cl = min(MLSTM_CHUNK, s)
    nc = s // cl
    ri = lax.broadcasted_iota(jnp.int32, (cl, cl), 0)
    ci = lax.broadcasted_iota(jnp.int32, (cl, cl), 1)
    masks = (ri >= ci, ri <= ci)
    tris = (masks[0].astype(BF16), masks[1].astype(BF16))
    grp = math.gcd(nc, MIXER_GROUP * (2 if s <= MLSTM_BIG_GROUP_SEQ else 1))
    rows = grp * cl
    mask_rows = tuple(jnp.concatenate([m] * grp, axis=0) for m in masks)
    ones_v = jnp.ones((rows, HEAD_DIM), BF16)
    kscale = HEAD_DIM ** -0.5
    twice = lambda a: jnp.concatenate([a, a], axis=1)
    per_chunk = lambda x: [x[g * cl:(g + 1) * cl] for g in range(grp)]
    cat = lambda xs: jnp.concatenate(xs, axis=0)
    chunks3 = lambda x: x.reshape(grp, cl, x.shape[-1])

    def load(r0, st_ref):
        return (g_ref[pl.ds(r0, rows), :], q_ref[pl.ds(r0, rows), :], k_ref[pl.ds(r0, rows), :],
                v_ref[pl.ds(r0, rows), :], st_ref[...])

    def group(vals, d, m_st):
        g, q, k, v, st = vals
        ig = jnp.broadcast_to(g[:, 2 * d:2 * d + 1], (rows, HEAD_DIM)) * LOG2_E
        lf = _log2_sigmoid(jnp.broadcast_to(g[:, 2 * d + 1:2 * d + 2], (rows, HEAD_DIM)))
        b = cat([_cumsum_mm(tris[d], x) for x in per_chunk(lf)])
        tot3 = chunks3(b)[:, 0:1] if d else chunks3(b)[:, cl - 1:cl]
        c = ig - b
        lw3 = tot3 + chunks3(c)
        m_loc3 = jnp.max(lw3, axis=1, keepdims=True)
        kf = k.astype(F32) * kscale
        kw = (chunks3(kf) * jnp.exp2(lw3 - m_loc3)).astype(BF16).reshape(rows, HEAD_DIM)
        kb = kf.astype(BF16)
        v1 = jnp.concatenate([v, ones_v], axis=1)
        cn_loc = [_dot_tn(a, w) for a, w in zip(per_chunk(kw), per_chunk(v1))]
        dm = jnp.where(mask_rows[d], cat([x.T for x in per_chunk(c)]), -jnp.inf)
        m_rel = jnp.max(dm, axis=1, keepdims=True)
        qk = cat([_dot_nt(a, w) for a, w in zip(per_chunk(q), per_chunk(kb))])
        am = (jnp.exp2(dm - m_rel) * qk).astype(BF16)
        nd_intra = cat([_dot(a, w) for a, w in zip(per_chunk(am), per_chunk(v1))])
        outs = [None] * grp
        for gi in (range(grp - 1, -1, -1) if d else range(grp)):
            sl = slice(gi * cl, (gi + 1) * cl)
            nd_inter = _dot(q[sl], st.astype(BF16))
            mu = jnp.maximum(m_rel[sl], m_st)
            f_i = jnp.exp2(m_rel[sl] - mu)
            f_s = jnp.exp2(m_st - mu)
            num = f_i * nd_intra[sl, :HEAD_DIM] + f_s * nd_inter[:, :HEAD_DIM]
            den = f_i * nd_intra[sl, HEAD_DIM:] + f_s * nd_inter[:, HEAD_DIM:]
            outs[gi] = num / jnp.maximum(jnp.abs(den), jnp.exp2(-(b[sl] + mu)))
            tot, m_loc = tot3[gi], m_loc3[gi]
            m_new = jnp.maximum(tot + m_st, m_loc)
            st = twice(jnp.exp2(tot + m_st - m_new)) * st + twice(jnp.exp2(m_loc - m_new)) * cn_loc[gi]
            m_st = m_new
        return cat(outs), m_st, st

    m0 = jnp.full((1, HEAD_DIM), STAB_INIT, F32)

    def init():
        stf_ref[...] = jnp.zeros_like(stf_ref)
        stb_ref[...] = jnp.zeros_like(stb_ref)

    def step(i, carry):
        m_f, m_b = carry
        rf = pl.multiple_of(i * rows, rows)
        rb = pl.multiple_of((nc // grp - 1 - i) * rows, rows)
        vals_f = load(rf, stf_ref)
        vals_b = load(rb, stb_ref)
        out_f, m_f, st_f = group(vals_f, 0, m_f)
        out_b, m_b, st_b = group(vals_b, 1, m_b)

        def commit():
            hf_ref[pl.ds(rf, rows), :] = out_f
            hb_ref[pl.ds(rb, rows), :] = out_b
            stf_ref[...] = st_f
            stb_ref[...] = st_b

        return (m_f, m_b), commit

    def fin(c):
        r0 = pl.multiple_of(c * rows, rows)
        y = _rms(hf_ref[pl.ds(r0, rows), :] + hb_ref[pl.ds(r0, rows), :], ng_ref[...])
        og = jax.nn.sigmoid(og_ref[pl.ds(r0, rows), :].astype(F32))
        o_ref[pl.ds(r0, rows), :] = (og * y).astype(o_ref.dtype)

    return nc // grp, _Steps(init, step, fin, (m0, m0))


def _mlstm_body(*refs):
    n, part = _mlstm_steps(*refs)
    _run_steps(n, part)


def _block_ref(b, m, rev):
    c, w = b.shape
    parts = []
    for blk in range(c // (2 * m)):
        row = blk * 2 * m + (m if rev else m - 1)
        parts.append(jnp.broadcast_to(b[row:row + 1, :], (2 * m, w)))
    return parts[0] if len(parts) == 1 else jnp.concatenate(parts, axis=0)


def _hgrn_steps(layer, q_ref, ff_ref, fb_ref, i_ref, g_ref, lbl_ref, ng_ref, o_ref, of_ref, ob_ref, stf_ref,
                stb_ref):
    s = q_ref.shape[0]
    c = min(HGRN_CHUNK, s)
    nc = s // c
    band = min(HGRN_BAND, c)
    if layer > 0:
        lg = lbl_ref[...]
        e = jnp.exp(lg - jnp.max(lg, axis=0, keepdims=True))
        p = e / jnp.sum(e, axis=0, keepdims=True)
        lb = p[1:2, :]
        for r in range(2, layer + 1):
            lb = lb + p[r:r + 1, :]
        log_lb = jnp.log(lb)
        log_1m = jnp.log(1.0 - lb)
        one_m = 1.0 - lb

    grp = math.gcd(nc, MIXER_GROUP)
    rows = grp * c
    ri = lax.broadcasted_iota(jnp.int32, (c, c), 0)
    ci = lax.broadcasted_iota(jnp.int32, (c, c), 1)
    tris = ((ri >= ci).astype(BF16), (ri <= ci).astype(BF16))
    ri = lax.broadcasted_iota(jnp.int32, (rows, c), 0) & (c - 1)
    ci = lax.broadcasted_iota(jnp.int32, (rows, c), 1)
    rw = lax.broadcasted_iota(jnp.int32, (rows, HEAD_DIM), 0)
    pos = lax.broadcasted_iota(jnp.int32, (rows, 1), 0) & (band - 1)
    levels = []
    m = c // 2
    while m >= band:
        levels.append(m)
        m //= 2
    upper = {m: (rw & (2 * m - 1)) >= m for m in levels}
    same_blk = {m: (ri >> int(math.log2(2 * m))) == (ci >> int(math.log2(2 * m))) for m in levels if 2 * m < c}
    diags = ([ri - ci == dl for dl in range(band)], [ci - ri == dl for dl in range(band)])
    per_chunk = lambda x: [x[g * c:(g + 1) * c] for g in range(grp)]
    cat = lambda xs: jnp.concatenate(xs, axis=0)
    chunks3 = lambda x: x.reshape(grp, c, x.shape[-1])

    def group(r0, rev, st_ref):
        f = (fb_ref if rev else ff_ref)[pl.ds(r0, rows), :]
        e_f = jnp.exp(-jnp.abs(f))
        den = 1.0 + e_f
        lf = jnp.minimum(f, 0.0) - jnp.log(den)
        kd = jnp.where(f >= 0.0, e_f, 1.0) / den
        if layer > 0:
            x = log_1m + lf
            mx = jnp.maximum(log_lb, x)
            lf = mx + jnp.log(jnp.exp(log_lb - mx) + jnp.exp(x - mx))
            kd = one_m * kd
        b = cat([_cumsum_mm(tris[rev], x) for x in per_chunk(lf * LOG2_E)])
        tot3 = chunks3(b)[:, 0:1] if rev else chunks3(b)[:, c - 1:c]
        qs = jax.nn.silu(q_ref[pl.ds(r0, rows), :].astype(F32))
        v = i_ref[pl.ds(r0, rows), :]
        q_in = per_chunk((qs * jnp.exp2(b)).astype(BF16))
        k_out = per_chunk((chunks3(kd) * jnp.exp2(tot3 - chunks3(b))).astype(BF16).reshape(rows, HEAD_DIM))
        kv = [_dot_tn(x, y) for x, y in zip(per_chunk(v), k_out)]
        amat = None
        for m in levels:
            bref = _block_ref(b, m, rev)
            up = upper[m]
            qm = jnp.logical_not(up) if rev else up
            km = up if rev else jnp.logical_not(up)
            qt = (qs * jnp.exp2(jnp.where(qm, b - bref, -jnp.inf))).astype(BF16)
            kt = (kd * jnp.exp2(jnp.where(km, bref - b, -jnp.inf))).astype(BF16)
            am = cat([_dot_nt(x, y) for x, y in zip(per_chunk(qt), per_chunk(kt))])
            if m in same_blk:
                am = jnp.where(same_blk[m], am, 0.0)
            amat = am if amat is None else amat + am
        for dl in range(band):
            if dl == 0:
                a = jnp.sum(qs * kd, axis=1, keepdims=True)
            else:
                sh = (rows - dl) if rev else dl
                valid = (pos + dl < band) if rev else (pos >= dl)
                ex = jnp.exp2(b - pltpu.roll(b, sh, 0))
                a = jnp.where(valid, jnp.sum(qs * pltpu.roll(kd, sh, 0) * ex, axis=1, keepdims=True), 0.0)
            term = jnp.where(diags[rev][dl], a, 0.0)
            amat = term if amat is None else amat + term
        o_intra = [_dot(x, y) for x, y in zip(per_chunk(amat.astype(BF16)), per_chunk(v))]
        st = st_ref[...]
        outs = [None] * grp
        for gi in (range(grp - 1, -1, -1) if rev else range(grp)):
            outs[gi] = o_intra[gi] + _dot_nt(q_in[gi], st.astype(BF16))
            st = jnp.exp2(tot3[gi]) * st + kv[gi]
        return cat(outs), st

    def init():
        stf_ref[...] = jnp.zeros_like(stf_ref)
        stb_ref[...] = jnp.zeros_like(stb_ref)

    def step(i, carry):
        rf = pl.multiple_of(i * rows, rows)
        rb = pl.multiple_of((nc // grp - 1 - i) * rows, rows)
        out_f, st_f = group(rf, 0, stf_ref)
        out_b, st_b = group(rb, 1, stb_ref)

        def commit():
            of_ref[pl.ds(rf, rows), :] = out_f
            ob_ref[pl.ds(rb, rows), :] = out_b
            stf_ref[...] = st_f
            stb_ref[...] = st_b

        return carry, commit

    def fin(ci_):
        r0 = pl.multiple_of(ci_ * rows, rows)
        hs = of_ref[pl.ds(r0, rows), :] + ob_ref[pl.ds(r0, rows), :]
        gate = jax.nn.sigmoid(g_ref[pl.ds(r0, rows), :].astype(F32))
        o_ref[pl.ds(r0, rows), :] = (gate * _rms(hs, ng_ref[...])).astype(o_ref.dtype)

    return nc // grp, _Steps(init, step, fin, jnp.int32(0))


def _hgrn_body(layer, *refs):
    n, part = _hgrn_steps(layer, *refs)
    _run_steps(n, part)


def _head_blk(s, off):
    return pl.BlockSpec((s, HEAD_DIM), lambda b, h: (b, off + h))


def _head_row(rows):
    return pl.BlockSpec((rows, HEAD_DIM), lambda b, h: (0, h))


def _mlstm_call_parts(pbf, pfp, lp, s):
    cb, cg = BF_B // HEAD_DIM, FP_GATE // HEAD_DIM
    specs = [_head_blk(s, cb), _head_blk(s, cb + 4), _head_blk(s, cb + 8), _head_blk(s, cb + 12),
             _head_blk(s, cg), _head_row(1)]
    scratch = [pltpu.VMEM((s, HEAD_DIM), F32)] * 2 + [pltpu.VMEM((HEAD_DIM, 2 * HEAD_DIM), F32)] * 2
    return specs, (pbf, pbf, pbf, pbf, pfp, lp["mlstm_norm"]), scratch


def _hgrn_call_parts(pbf, pfp, lp, lb_logits, s):
    cd = BF_D3 // HEAD_DIM
    specs = [_head_blk(s, cd), _head_blk(s, FP_FF // HEAD_DIM), _head_blk(s, FP_FB // HEAD_DIM),
             _head_blk(s, cd + 4), _head_blk(s, cd + 8), _head_row(lb_logits.shape[0]), _head_row(1)]
    scratch = [pltpu.VMEM((s, HEAD_DIM), F32)] * 2 + [pltpu.VMEM((HEAD_DIM, HEAD_DIM), F32)] * 2
    return specs, (pbf, pfp, pfp, pbf, pbf, lb_logits, lp["hgrn_norm"]), scratch


def _mlstm_hgrn(pbf, pfp, lp, lb_logits, layer, bsz, s):
    t = pbf.shape[0]
    m_specs, m_args, m_scr = _mlstm_call_parts(pbf, pfp, lp, s)
    h_specs, h_args, h_scr = _hgrn_call_parts(pbf, pfp, lp, lb_logits, s)
    out_spec = _head_blk(s, 0)
    out_shape = jax.ShapeDtypeStruct((t, MIX_W), BF16)
    m_out = pl.pallas_call(
        _mlstm_body, grid=(bsz, N_HEADS), in_specs=m_specs, out_specs=out_spec, out_shape=out_shape,
        scratch_shapes=m_scr, compiler_params=_cp(("parallel", "parallel")), name="mlstm")(*m_args)
    h_out = pl.pallas_call(
        functools.partial(_hgrn_body, layer), grid=(bsz, N_HEADS), in_specs=h_specs, out_specs=out_spec,
        out_shape=out_shape, scratch_shapes=h_scr, compiler_params=_cp(("parallel", "parallel")),
        name="hgrn2")(*h_args)
    return m_out, h_out


def _shift3(cat, w, bias, ts):
    n = ts + 2 * HALO
    return (bias + w[0:1] * pltpu.roll(cat, 1, 0)[HALO:HALO + ts] + w[1:2] * cat[HALO:HALO + ts]
            + w[2:3] * pltpu.roll(cat, n - 1, 0)[HALO:HALO + ts])


def _hyena_prep_body(x_ref, xp_ref, xn_ref, cw_ref, cb_ref, z_ref, x1_ref):
    i = pl.program_id(1)
    ts = x_ref.shape[0]
    prev = jnp.where(i > 0, xp_ref[...].astype(F32), 0.0)
    nxt = jnp.where(i < pl.num_programs(1) - 1, xn_ref[...].astype(F32), 0.0)
    cat = jnp.concatenate([prev, x_ref[...].astype(F32), nxt], axis=0)
    u = _shift3(cat, cw_ref[...], cb_ref[...], ts)
    z_ref[...] = (u[:, 2 * MIX_W:] * u[:, :MIX_W]).astype(z_ref.dtype)
    x1_ref[...] = u[:, MIX_W:2 * MIX_W].astype(x1_ref.dtype)


def _hyena_prep(pbf, lp, bsz, s):
    t = pbf.shape[0]
    ts = min(512, s)
    per = s // ts
    hb = ts // HALO
    nb = t // HALO
    w = 3 * MIX_W
    return pl.pallas_call(
        _hyena_prep_body,
        grid=(bsz, per),
        in_specs=[
            pl.BlockSpec((ts, w), lambda b, i: (b * per + i, 0)),
            pl.BlockSpec((HALO, w), lambda b, i: (jnp.maximum((b * per + i) * hb - 1, 0), 0)),
            pl.BlockSpec((HALO, w), lambda b, i: (jnp.minimum((b * per + i + 1) * hb, nb - 1), 0)),
            pl.BlockSpec((3, w), lambda b, i: (0, 0)),
            pl.BlockSpec((1, w), lambda b, i: (0, 0)),
        ],
        out_specs=[pl.BlockSpec((ts, MIX_W), lambda b, i: (b * per + i, 0)),
                   pl.BlockSpec((ts, MIX_W), lambda b, i: (b * per + i, 0))],
        out_shape=[jax.ShapeDtypeStruct((t, MIX_W), BF16), jax.ShapeDtypeStruct((t, MIX_W), BF16)],
        compiler_params=_cp(("parallel", "arbitrary")),
        name="hyena_prep",
    )(pbf, pbf, pbf, lp["hyena_conv_w"], lp["hyena_conv_b"])


def _split_bf16(a):
    hi = a.astype(BF16)
    return jnp.stack([hi, (a - hi.astype(F32)).astype(BF16)])


def _dot3(f_hi, f_lo, x):
    x_hi = x.astype(BF16)
    x_lo = (x - x_hi.astype(F32)).astype(BF16)
    return _dot(f_hi, x_hi) + _dot(f_hi, x_lo) + _dot(f_lo, x_hi)


def _dft1_body(x_ref, f_ref, o_ref):
    jb, _, n1, r, c = o_ref.shape
    n1h = x_ref.shape[0]
    for jj in range(jb):
        x = x_ref[:, jj * r:(jj + 1) * r, :].reshape(n1h * r, c)
        o = _dot3(f_ref[0], f_ref[1], x) if f_ref.ndim == 3 else _dot(f_ref[...], x)
        o_ref[jj] = o.reshape(2, n1, r, c).astype(o_ref.dtype)


def _dft1(x4, kron1, out_dtype):
    bsz, n1h, n2, c = x4.shape
    r = kron1.shape[-1] // n1h
    n1 = kron1.shape[-2] // (2 * r)
    nj = n2 // r
    jb = math.gcd(nj, max(1, DFT_STEP_ROWS // (2 * n1)))
    return pl.pallas_call(
        _dft1_body,
        grid=(bsz, nj // jb),
        in_specs=[pl.BlockSpec((None, n1h, jb * r, c), lambda b, j: (b, 0, j, 0)),
                  pl.BlockSpec(kron1.shape, lambda b, j: (0,) * kron1.ndim, pipeline_mode=pl.Buffered(1))],
        out_specs=pl.BlockSpec((None, jb, 2, n1, r, c), lambda b, j: (b, j, 0, 0, 0, 0)),
        out_shape=jax.ShapeDtypeStruct((bsz, n2 // r, 2, n1, r, c), out_dtype),
        compiler_params=_cp(("parallel", "parallel")),
        name="dft_stage1",
    )(x4, kron1)


def _spectrum_rows(a_ref, lead, kk):
    nj = a_ref.shape[len(lead)]
    return jnp.concatenate([a_ref[lead + (j, e, kk)] for e in range(2) for j in range(nj)], axis=0)


def _dft2_conv_body(kb, a_ref, f_ref, fi_ref, g_ref, o_ref):
    nj, _, _, r, _ = a_ref.shape
    n2 = nj * r
    for kk in range(kb):
        x = _dot(f_ref[kk], _spectrum_rows(a_ref, (), kk))
        xr, xi = x[:n2], x[n2:]
        gr, gi = g_ref[kk, 0], g_ref[kk, 1]
        y = jnp.concatenate([xr * gr - xi * gi, xr * gi + xi * gr], axis=0).astype(BF16)
        bq = _dot(fi_ref[kk], y).astype(o_ref.dtype)
        for e in range(2):
            for j in range(nj):
                o_ref[j, e, kk] = bq[e * n2 + j * r:e * n2 + (j + 1) * r]


def _dft2_conv(a6, f2, f2i, gspec, kb):
    bsz, nj, _, n1, r, c = a6.shape
    n2 = nj * r
    blk = pl.BlockSpec((None, nj, 2, kb, r, c), lambda k, b: (b, 0, 0, k, 0, 0))
    return pl.pallas_call(
        functools.partial(_dft2_conv_body, kb),
        grid=(n1 // kb, bsz),
        in_specs=[blk,
                  pl.BlockSpec((kb, 2 * n2, 2 * n2), lambda k, b: (k, 0, 0)),
                  pl.BlockSpec((kb, 2 * n2, 2 * n2), lambda k, b: (k, 0, 0)),
                  pl.BlockSpec((kb, 2, n2, c), lambda k, b: (k, 0, 0, 0))],
        out_specs=blk,
        out_shape=jax.ShapeDtypeStruct(a6.shape, BF16),
        compiler_params=_cp(("parallel", "arbitrary")),
        name="dft_stage2_conv",
    )(a6, f2, f2i, gspec)


def _dft2_filter_body(kb, a_ref, f_ref, o_ref):
    n2 = a_ref.shape[1] * a_ref.shape[4]
    k0 = pl.program_id(0) * kb
    for kk in range(kb):
        sgn = (1 - 2 * ((k0 + kk) & 1)).astype(F32)
        xs = [_dot3(f_ref[0, kk], f_ref[1, kk], _spectrum_rows(a_ref, (part,), kk)) for part in range(2)]
        x = xs[0] + sgn * xs[1]
        o_ref[kk, 0] = x[:n2]
        o_ref[kk, 1] = x[n2:]


def _dft2_filter(a6, f2, kb):
    _, nj, _, n1, r, c = a6.shape
    n2 = nj * r
    return pl.pallas_call(
        functools.partial(_dft2_filter_body, kb),
        grid=(n1 // kb,),
        in_specs=[pl.BlockSpec((2, nj, 2, kb, r, c), lambda k: (0, 0, 0, k, 0, 0)),
                  pl.BlockSpec((2, kb, 2 * n2, 2 * n2), lambda k: (0, k, 0, 0))],
        out_specs=pl.BlockSpec((kb, 2, n2, c), lambda k: (k, 0, 0, 0)),
        out_shape=jax.ShapeDtypeStruct((n1, 2, n2, c), F32),
        compiler_params=_cp(("parallel",)),
        name="dft_stage2_filter",
    )(a6, f2)


def _dft3_body(b_ref, f_ref, x1_ref, z_ref, sk_ref, o_ref):
    jb, _, n1, r, c = b_ref.shape
    n1h = o_ref.shape[0]
    for jj in range(jb):
        rows = slice(jj * r, (jj + 1) * r)
        y = _dot(f_ref[...], b_ref[jj].reshape(2 * n1 * r, c)).reshape(n1h, r, c)
        z = z_ref[:, rows, :].astype(F32)
        o_ref[:, rows, :] = (x1_ref[:, rows, :].astype(F32) * (y + sk_ref[...] * z)).astype(o_ref.dtype)


def _dft3(b6, kron3, x1_4, z4, skip):
    bsz, nj, _, n1, r, c = b6.shape
    n1h, n2 = n1 // 2, nj * r
    assert r == kron3.shape[0] // n1h
    jb = math.gcd(nj, max(1, DFT_STEP_ROWS // (2 * n1)))
    row = pl.BlockSpec((None, n1h, jb * r, c), lambda b, j: (b, 0, j, 0))
    return pl.pallas_call(
        _dft3_body,
        grid=(bsz, nj // jb),
        in_specs=[pl.BlockSpec((None, jb, 2, n1, r, c), lambda b, j: (b, j, 0, 0, 0, 0)),
                  pl.BlockSpec(kron3.shape, lambda b, j: (0, 0), pipeline_mode=pl.Buffered(1)),
                  row, row,
                  pl.BlockSpec((1, c), lambda b, j: (0, 0))],
        out_specs=row,
        out_shape=jax.ShapeDtypeStruct((bsz, n1h, n2, c), BF16),
        compiler_params=_cp(("parallel", "parallel")),
        name="dft_stage3",
    )(b6, kron3, x1_4, z4, skip)


def _hyena_filter_body(seq_len, pos_ref, sc_ref, bands_ref, w1t_ref, w1c_ref, w1s_ref, b1_ref, fr1_ref,
                       w2_ref, b2_ref, fr2_ref, w3_ref, rate_ref, o_ref):
    pos = pos_ref[...]
    t = pos * (1.0 / (seq_len - 1))
    arg = (pos * (2.0 * math.pi / seq_len)) * bands_ref[...]
    hdot = lambda a, b: jnp.dot(a, b, preferred_element_type=F32, precision=HIGHEST)
    pre = t * w1t_ref[...] + hdot(jnp.cos(arg), w1c_ref[...]) - hdot(jnp.sin(arg), w1s_ref[...]) + b1_ref[...]
    hid = jnp.sin(fr1_ref[...] * pre)
    hid = jnp.sin(fr2_ref[...] * (hdot(hid, w2_ref[...]) + b2_ref[...]))
    o_ref[...] = hdot(hid, w3_ref[...]) * jnp.exp(-t * rate_ref[...]) * sc_ref[...]


def _hyena_filter(lp, s):
    ts = min(256, s)
    r = 2 * s + ts
    pos = np.concatenate([np.arange(s), s - np.arange(s), np.zeros(ts)]).astype(np.float32)[:, None]
    sc = np.ones((r, 1), np.float32)
    sc[s] = 0.0
    pad = HEAD_DIM - HYENA_HID
    bands = np.zeros((1, HEAD_DIM), np.float32)
    bands[0, :HYENA_BANDS] = np.linspace(1e-4, HYENA_BANDS - 1, HYENA_BANDS)
    rate = np.abs(np.linspace(math.log(HYENA_TARGET) / HYENA_FAST, math.log(HYENA_TARGET) / HYENA_SLOW, MIX_W))
    rate = np.tile(rate, 2).astype(np.float32)[None, :]
    half = lambda i: (0, jnp.where(i < s // ts, 0, 1))
    w1 = lp["hyena_w1"]
    padc = lambda a: jnp.pad(a, ((0, 0), (0, pad)))
    w1t = padc(w1[0:1])
    w1c = jnp.pad(w1[1:1 + HYENA_BANDS], ((0, HEAD_DIM - HYENA_BANDS), (0, pad)))
    w1s = jnp.pad(w1[1 + HYENA_BANDS:], ((0, HEAD_DIM - HYENA_BANDS), (0, pad)))
    w2 = jnp.pad(lp["hyena_w2"], ((0, pad), (0, pad)))
    w3 = jnp.pad(lp["hyena_w3"], ((0, pad), (0, 0)))
    vec = lambda a: padc(a[None, :])
    full = lambda a: pl.BlockSpec(a.shape, lambda i: (0, 0))
    args = [jnp.asarray(bands), w1t, w1c, w1s, vec(lp["hyena_b1"]), vec(lp["hyena_freq1"]), w2,
            vec(lp["hyena_b2"]), vec(lp["hyena_freq2"])]
    out = pl.pallas_call(
        functools.partial(_hyena_filter_body, s),
        grid=(r // ts,),
        in_specs=[pl.BlockSpec((ts, 1), lambda i: (i, 0)), pl.BlockSpec((ts, 1), lambda i: (i, 0))]
                 + [full(a) for a in args]
                 + [pl.BlockSpec((HEAD_DIM, MIX_W), half), pl.BlockSpec((1, MIX_W), half)],
        out_specs=pl.BlockSpec((ts, MIX_W), lambda i: (i, 0)),
        out_shape=jax.ShapeDtypeStruct((r, MIX_W), F32),
        compiler_params=_cp(("parallel",)),
        name="hyena_filter",
    )(jnp.asarray(pos), jnp.asarray(sc), *args, w3, jnp.asarray(rate))
    return out[:s], out[s:2 * s], out[2 * s:2 * s + 1]


def _dft_tables(s):
    n = 2 * s
    n2 = min(DFT_N2, s // 8)
    n1 = n // n2
    n1h = n1 // 2
    two_pi = 2.0 * math.pi
    k1 = jnp.arange(n1, dtype=jnp.int32)
    a1 = (two_pi / n1) * ((k1[:, None] * k1[None, :n1h]) % n1).astype(F32)
    f1 = jnp.concatenate([jnp.cos(a1), -jnp.sin(a1)], axis=0)
    f1i = jnp.concatenate([jnp.cos(a1).T, -jnp.sin(a1).T], axis=1) * (1.0 / n)
    j = jnp.arange(n2, dtype=jnp.int32)
    ph = (j[None, :, None] * j[None, None, :] * n1 + j[None, None, :] * k1[:, None, None]) % n
    a2 = (two_pi / n) * ph.astype(F32)
    tr, ti = jnp.cos(a2), -jnp.sin(a2)
    f2 = jnp.concatenate([jnp.concatenate([tr, -ti], axis=2), jnp.concatenate([ti, tr], axis=2)], axis=1)
    kron = lambda m, r: jnp.kron(m, jnp.eye(r, dtype=F32))
    r32, r16 = min(SUBLANES, n2), min(2 * SUBLANES, n2)
    return dict(n1=n1, n2=n2, f2_split=_split_bf16(f2), f2_bf=f2.astype(BF16),
                f2i_bf=jnp.swapaxes(f2, 1, 2).astype(BF16), kron1_split=_split_bf16(kron(f1, r32)),
                kron1_bf=kron(f1, r16).astype(BF16), kron3_bf=kron(f1i, r16).astype(BF16))


def _hyena_spectrum(lp, s, tab):
    gpos, gneg, hb0 = _hyena_filter(lp, s)
    n1, n2 = tab["n1"], tab["n2"]
    a = _dft1(jnp.stack([gpos, gneg]).reshape(2, n1 // 2, n2, MIX_W), tab["kron1_split"], F32)
    return _dft2_filter(a, tab["f2_split"], min(2, n1)), hb0


def _hyena(pbf, lp, gspec, hb0, tab, bsz, s):
    n1, n2 = tab["n1"], tab["n2"]
    z, x1 = _hyena_prep(pbf, lp, bsz, s)
    z4 = z.reshape(bsz, n1 // 2, n2, MIX_W)
    a = _dft1(z4, tab["kron1_bf"], BF16)
    bq = _dft2_conv(a, tab["f2_bf"], tab["f2i_bf"], gspec, min(8, n1))
    out = _dft3(bq, tab["kron3_bf"], x1.reshape(z4.shape), z4, lp["hyena_skip"][None, :] + hb0)
    return out.reshape(bsz * s, MIX_W)


def _merge_body(h_ref, g0_ref, g1_ref, g2_ref, g3_ref, ba_ref, bb_ref, bc_ref, bd_ref, wb_ref, wo_ref, o_ref):
    merged = None
    for idx, (g_ref, br) in enumerate(zip((g0_ref, g1_ref, g2_ref, g3_ref), (ba_ref, bb_ref, bc_ref, bd_ref))):
        term = jax.nn.sigmoid(g_ref[...].astype(F32)) * _dot(br[...], wb_ref[idx])
        merged = term if merged is None else merged + term
    o_ref[...] = h_ref[...] + _dot(merged.astype(BF16), wo_ref[...])


def _merge(h, pbf, branches, lp):
    t, d = h.shape
    tm = min(512, t)
    gcol = BF_G // d
    row = lambda w: pl.BlockSpec((tm, w), lambda i: (i, 0))
    gate = lambda idx: pl.BlockSpec((tm, d), lambda i: (i, gcol + idx))
    return pl.pallas_call(
        _merge_body,
        grid=(t // tm,),
        in_specs=[row(d), gate(0), gate(1), gate(2), gate(3),
                  row(MIX_W), row(MIX_W), row(MIX_W), row(MIX_W),
                  pl.BlockSpec(lp["w_branch"].shape, lambda i: (0, 0, 0)),
                  pl.BlockSpec(lp["w_out"].shape, lambda i: (0, 0))],
        out_specs=row(d),
        out_shape=jax.ShapeDtypeStruct((t, d), F32),
        compiler_params=_cp(("parallel",)),
        name="merge",
    )(h, pbf, pbf, pbf, pbf, *branches, lp["w_branch"], lp["w_out"])


def _ffn_body(per, final, h_ref, hp_ref, hn_ref, p_ref, gf_ref, wu_ref, cw_ref, cb_ref, wd_ref, gp_ref, wg_ref,
              wp_ref, gl_ref, o_ref):
    i = pl.program_id(0)
    tm = h_ref.shape[0]
    dff = wd_ref.shape[0]
    gf = gf_ref[...]
    h = h_ref[...]
    first = (i % per) == 0
    last = (i % per) == per - 1
    xp = jnp.where(first, 0.0, _rms(hp_ref[...], gf))
    xn = jnp.where(last, 0.0, _rms(hn_ref[...], gf))
    cat = jnp.concatenate([xp, _rms(h, gf), xn], axis=0).astype(BF16)
    cb_w = min(2048, dff)
    acc = jnp.zeros_like(h)
    for cb in range(dff // cb_w):
        lo, lo2 = cb * cb_w, dff + cb * cb_w
        u1 = _shift3(_dot(cat, wu_ref[:, lo:lo + cb_w]), cw_ref[:, lo:lo + cb_w], cb_ref[:, lo:lo + cb_w], tm)
        u2 = _shift3(_dot(cat, wu_ref[:, lo2:lo2 + cb_w]), cw_ref[:, lo2:lo2 + cb_w], cb_ref[:, lo2:lo2 + cb_w], tm)
        acc = acc + _dot((_gelu(u1) * u2).astype(BF16), wd_ref[lo:lo + cb_w, :])
    h = h + acc
    gate = jax.nn.sigmoid(_dot(_rms(h, gp_ref[...]).astype(BF16), wg_ref[...]))
    h = h + gate * _dot(p_ref[...].astype(BF16), wp_ref[...])
    if final:
        h = _rms(h, gl_ref[...])
    o_ref[...] = h


def _ffn(h, p, lp, final_norm, s, final):
    t, d = h.shape
    tm = min(512, s)
    per = s // tm
    hb = tm // HALO
    nb = t // HALO
    full = lambda a: pl.BlockSpec(a.shape, lambda i: (0,) * a.ndim)
    row = lambda w: pl.BlockSpec((tm, w), lambda i: (i, 0))
    ws = [lp["norm_ffn"], lp["w_up"], lp["ffn_conv_w"], lp["ffn_conv_b"], lp["w_down"], lp["norm_ple"],
          lp["w_ple_gate"], lp["w_ple"], final_norm]
    return pl.pallas_call(
        functools.partial(_ffn_body, per, final),
        grid=(t // tm,),
        in_specs=[row(d),
                  pl.BlockSpec((HALO, d), lambda i: (jnp.maximum(i * hb - 1, 0), 0)),
                  pl.BlockSpec((HALO, d), lambda i: (jnp.minimum((i + 1) * hb, nb - 1), 0)),
                  row(p.shape[1])] + [full(a) for a in ws],
        out_specs=row(d),
        out_shape=jax.ShapeDtypeStruct((t, d), F32),
        compiler_params=_cp(("parallel",)),
        name="ffn_ple",
    )(h, h, h, p, *ws)


def _prepare_params(prm):
    d_model = prm["w_in"].shape[1]
    depth = prm["w_in"].shape[0]
    off_b = 2 * MIX_W
    off_gate = off_b + 4 * MIX_W
    off_c = off_gate + 4 * N_HEADS
    off_d = off_c + 3 * MIX_W
    off_g = off_d + 5 * MIX_W

    def regroup(a):
        sl = lambda lo, w: a[..., lo:lo + w]
        bf = jnp.concatenate([sl(off_g, 4 * d_model), sl(0, 2 * MIX_W), sl(off_b, 4 * MIX_W), sl(off_d, MIX_W),
                              sl(off_d + 3 * MIX_W, 2 * MIX_W)], axis=-1)
        gates = sl(off_gate, 4 * N_HEADS).reshape(a.shape[:-1] + (4, N_HEADS))
        gates = jnp.swapaxes(gates, -1, -2)
        gates = jnp.pad(gates, [(0, 0)] * (gates.ndim - 1) + [(0, HEAD_DIM - 4)])
        fp = jnp.concatenate([sl(off_d + MIX_W, 2 * MIX_W), gates.reshape(a.shape[:-1] + (N_HEADS * HEAD_DIM,))],
                             axis=-1)
        return bf, sl(off_c, 3 * MIX_W), fp

    w_bf, w_hy, w_fp = regroup(prm["w_in"])
    b_bf, b_hy, b_fp = regroup(prm["b_in"][:, None, :])
    gw = jnp.transpose(prm["rglru_w"], (0, 3, 4, 1, 2, 5)).reshape(depth, N_HEADS, HEAD_DIM, 4 * HEAD_DIM)
    gb = prm["rglru_b"].reshape(depth, 2, 2, N_HEADS, HEAD_DIM)
    gb = jnp.transpose(gb, (0, 3, 1, 2, 4)).reshape(depth, N_HEADS, 1, 4 * HEAD_DIM)
    row = lambda a: a[:, None, :]
    out = dict(
        norm_mix=row(prm["norm_mix"]), w_bf=w_bf.astype(BF16), b_bf=b_bf, w_hy=w_hy.astype(BF16), b_hy=b_hy,
        w_fp=w_fp.astype(BF16), b_fp=b_fp,
        conv_a_w=prm["conv_a_w"], conv_a_b=row(prm["conv_a_b"]), rglru_w=gw.astype(BF16), rglru_b=gb,
        rglru_lam=prm["rglru_lam"], mlstm_norm=row(prm["mlstm_norm"]),
        hyena_conv_w=prm["hyena_conv_w"], hyena_conv_b=row(prm["hyena_conv_b"]),
        hyena_w1=prm["hyena_w1"], hyena_b1=prm["hyena_b1"], hyena_freq1=prm["hyena_freq1"],
        hyena_w2=prm["hyena_w2"], hyena_b2=prm["hyena_b2"], hyena_freq2=prm["hyena_freq2"],
        hyena_w3=prm["hyena_w3"], hyena_skip=prm["hyena_skip"], hgrn_norm=row(prm["hgrn_norm"]),
        w_branch=prm["w_branch"].astype(BF16), w_out=prm["w_out"].astype(BF16),
        norm_ffn=row(prm["norm_ffn"]), w_up=prm["w_up"].astype(BF16), ffn_conv_w=prm["ffn_conv_w"],
        ffn_conv_b=row(prm["ffn_conv_b"]), w_down=prm["w_down"].astype(BF16), norm_ple=row(prm["norm_ple"]),
        w_ple_gate=prm["w_ple_gate"].astype(BF16), w_ple=prm["w_ple"].astype(BF16),
    )
    return out


def _trunk(x, p, prm, lb_logits, final_norm):
    bsz, s, d = x.shape
    depth = p.shape[0]
    h = x.reshape(bsz * s, d)
    tab = _dft_tables(s)
    for i in range(depth):
        lp = {k: v[i] for k, v in prm.items()}
        pbf, phy, pfp = _proj(h, lp["norm_mix"], [(lp["w_bf"], lp["b_bf"]), (lp["w_hy"], lp["b_hy"]),
                                                  (lp["w_fp"], lp["b_fp"])], (BF16, BF16, F32))
        gspec, hb0 = _hyena_spectrum(lp, s, tab)
        br_b, br_d = _mlstm_hgrn(pbf, pfp, lp, lb_logits, i, bsz, s)
        branches = (_rglru(pbf, lp, bsz, s), br_b, _hyena(phy, lp, gspec, hb0, tab, bsz, s), br_d)
        h = _merge(h, pbf, branches, lp)
        h = _ffn(h, p[i].reshape(bsz * s, -1), lp, final_norm, s, i == depth - 1)
    return h.reshape(bsz, s, d)


def kernel(x_prompt, x_sample, p_prompt, p_sample, norm_mix, w_in, b_in, conv_a_w, conv_a_b, rglru_w, rglru_b, rglru_lam, mlstm_norm, hyena_conv_w, hyena_conv_b, hyena_w1, hyena_b1, hyena_freq1, hyena_w2, hyena_b2, hyena_freq2, hyena_w3, hyena_skip, hgrn_lb_logits, hgrn_norm, w_branch, w_out, norm_ffn, w_up, ffn_conv_w, ffn_conv_b, w_down, norm_ple, w_ple_gate, w_ple, final_norm):
    prm = _prepare_params(dict(
        norm_mix=norm_mix, w_in=w_in, b_in=b_in, conv_a_w=conv_a_w, conv_a_b=conv_a_b, rglru_w=rglru_w,
        rglru_b=rglru_b, rglru_lam=rglru_lam, mlstm_norm=mlstm_norm, hyena_conv_w=hyena_conv_w,
        hyena_conv_b=hyena_conv_b, hyena_w1=hyena_w1, hyena_b1=hyena_b1, hyena_freq1=hyena_freq1,
        hyena_w2=hyena_w2, hyena_b2=hyena_b2, hyena_freq2=hyena_freq2, hyena_w3=hyena_w3,
        hyena_skip=hyena_skip, hgrn_norm=hgrn_norm, w_branch=w_branch, w_out=w_out, norm_ffn=norm_ffn,
        w_up=w_up, ffn_conv_w=ffn_conv_w, ffn_conv_b=ffn_conv_b, w_down=w_down, norm_ple=norm_ple,
        w_ple_gate=w_ple_gate, w_ple=w_ple))
    fn = final_norm[None, :]
    y_prompt = _trunk(x_prompt, p_prompt, prm, hgrn_lb_logits, fn)
    y_sample = _trunk(x_sample, p_sample, prm, hgrn_lb_logits, fn)
    return (y_prompt, y_sample)
```
